```python
import math
import jax, jax.numpy as jnp
from jax import lax
import numpy as np

D_MODEL = 2048
BATCH = 4
SEQ = 2048
DEPTH = 4
DEC_BATCH = 128
DEC_SEQ = 4
PAST_LEN = 16384
PAGE_SIZE = 128

GLA_HEADS = 4
GLA_DK = 128
GLA_DV = 256
GLA_KW = GLA_HEADS * GLA_DK
GLA_VW = GLA_HEADS * GLA_DV
GLA_RANK = 16
GLA_GATE_NORM = 16.0
GLA_CHUNK = 32

RW_HEADS = 16
RW_N = 64
RW_W = RW_HEADS * RW_N
RW_DECAY_R = 64
RW_A_R = 64
RW_V_R = 32
RW_G_R = 160
RW_GN_EPS = 64e-5

MEM_LEN = 256
X_HEADS = 4
X_DH = 256
X_W = X_HEADS * X_DH

N_BRANCH = 3
D_FF = ((8 * D_MODEL + 3 * 256 - 1) // (3 * 256)) * 256

GLA_COLS = 2 * GLA_KW + 2 * GLA_VW + GLA_RANK
RW_COLS = 3 * RW_W + RW_DECAY_R + RW_A_R + RW_G_R
IN_COLS = GLA_COLS + RW_COLS + X_W + N_BRANCH * D_MODEL
NORM_EPS = 1e-6

kernel_name = "gla_rwkv7_memxattn_hybrid_step"


def _split(z, sizes):
    offs = []
    o = 0
    for s in sizes[:-1]:
        o += s
        offs.append(o)
    return jnp.split(z, offs, axis=-1)


def _heads(z, n):
    return z.reshape(z.shape[0], z.shape[1], n, -1)


def rmsnorm(x, g):
    xf = x.astype(jnp.float32)
    y = xf * lax.rsqrt(jnp.mean(xf * xf, axis=-1, keepdims=True) + NORM_EPS)
    return (y * g.astype(jnp.float32)).astype(x.dtype)


def gla_chunked(q, k, v, log_a, s0):
    b_, t, h, _ = q.shape
    c = math.gcd(t, GLA_CHUNK)
    n = t // c

    def to_chunks(z):
        return z.astype(jnp.float32).reshape(b_, n, c, h, z.shape[-1]).transpose(1, 0, 3, 2, 4)

    mask = jnp.tril(jnp.ones((c, c), dtype=bool))[:, :, None]

    def step(s, inp):
        qc, kc, vc, gc = inp
        bc = jnp.cumsum(gc, axis=2)
        o_inter = jnp.einsum('bhid,bhde->bhie', qc * jnp.exp(bc), s)
        diff = bc[:, :, :, None, :] - bc[:, :, None, :, :]
        dec = jnp.exp(jnp.where(mask, diff, -jnp.inf))
        att = jnp.einsum('bhid,bhjd,bhijd->bhij', qc, kc, dec)
        o = o_inter + jnp.einsum('bhij,bhje->bhie', att, vc)
        blast = bc[:, :, -1:, :]
        s_new = s * jnp.exp(blast[:, :, 0, :])[..., None] + jnp.einsum('bhjd,bhje->bhde', kc * jnp.exp(blast - bc), vc)
        return s_new, o

    s_fin, o = lax.scan(step, s0.astype(jnp.float32), (to_chunks(q), to_chunks(k), to_chunks(v), to_chunks(log_a)))
    o = o.transpose(1, 0, 3, 2, 4).reshape(b_, t, h, -1)
    return o, s_fin


def rwkv7_scan(r, w, k, v, kk, a, s0):
    def step(s, inp):
        r_t, w_t, k_t, v_t, kk_t, a_t = inp
        sk = jnp.einsum('bhvk,bhk->bhv', s, kk_t)
        s = s * w_t[:, :, None, :] - sk[..., None] * (kk_t * a_t)[:, :, None, :] + v_t[..., None] * k_t[:, :, None, :]
        return s, jnp.einsum('bhvk,bhk->bhv', s, r_t)

    seq = tuple(z.astype(jnp.float32).transpose(1, 0, 2, 3) for z in (r, w, k, v, kk, a))
    s_fin, y = lax.scan(step, s0.astype(jnp.float32), seq)
    return y.transpose(1, 0, 2, 3), s_fin


def hybrid_layer(x, l, gla_s0, rw_s0, shift0, mem_k, mem_v, v_first, p):
    f32 = jnp.float32
    bsz, t, _ = x.shape
    h = rmsnorm(x, p['g_norm1'][l])
    proj = h @ p['w_in'][l]
    gla_p, rw_p, xq, gate_p = _split(proj, [GLA_COLS, RW_COLS, X_W, N_BRANCH * D_MODEL])

    q, k, v, g, ga = _split(gla_p, [GLA_KW, GLA_KW, GLA_VW, GLA_VW, GLA_RANK])
    log_a = jax.nn.log_sigmoid((ga @ p['gla_wa2'][l] + p['gla_ba2'][l]).astype(f32)) / GLA_GATE_NORM
    o, gla_s = gla_chunked(_heads(q, GLA_HEADS) * (GLA_DK ** -0.5), _heads(k, GLA_HEADS),
                           _heads(v, GLA_HEADS), _heads(log_a, GLA_HEADS), gla_s0)
    o_gla = (rmsnorm(o, p['gla_onorm'][l]).reshape(bsz, t, GLA_VW) * jax.nn.silu(g.astype(f32))).astype(x.dtype)

    prev = jnp.concatenate([shift0[:, None, :].astype(rw_p.dtype), rw_p[:, :-1]], axis=1)
    feat = rw_p + p['rw_mu'][l] * (prev - rw_p)
    new_shift = rw_p[:, -1]
    r, kr, vr, xw, xa, xg = _split(feat, [RW_W, RW_W, RW_W, RW_DECAY_R, RW_A_R, RW_G_R])
    w_log = -jax.nn.softplus(-(p['rw_w0'][l] + jnp.tanh(xw) @ p['rw_w2'][l]).astype(f32)) - 0.5
    decay = jnp.exp(-jnp.exp(w_log))
    a = jax.nn.sigmoid((p['rw_a0'][l] + xa @ p['rw_a2'][l]).astype(f32))
    gate = jax.nn.sigmoid(xg) @ p['rw_g2'][l]
    if l == 0:
        v_first = vr
    else:
        nu = jax.nn.sigmoid(p['rw_v0'][l - 1] + (h @ p['rw_v1'][l - 1]) @ p['rw_v2'][l - 1])
        vr = vr + (v_first - vr) * nu
    kk = _heads((kr * p['rw_kk'][l]).astype(f32), RW_HEADS)
    kk = kk / jnp.maximum(jnp.sqrt(jnp.sum(kk * kk, axis=-1, keepdims=True)), 1e-12)
    kr = kr.astype(f32) * (1.0 + (a - 1.0) * p['rw_ka'][l].astype(f32))
    r_h, k_h, v_h, a_h = _heads(r.astype(f32), RW_HEADS), _heads(kr, RW_HEADS), _heads(vr.astype(f32), RW_HEADS), _heads(a, RW_HEADS)
    y, rw_s = rwkv7_scan(r_h, _heads(decay, RW_HEADS), k_h, v_h, kk, a_h, rw_s0)
    mu = jnp.mean(y, axis=-1, keepdims=True)
    var = jnp.mean((y - mu) ** 2, axis=-1, keepdims=True)
    yn = ((y - mu) * lax.rsqrt(var + RW_GN_EPS)).reshape(bsz, t, RW_W) * p['rw_lnx_w'][l] + p['rw_lnx_b'][l]
    bonus = jnp.sum(r_h * k_h * p['rw_rk'][l].astype(f32), axis=-1, keepdims=True) * v_h
    o_rw = ((yn + bonus.reshape(bsz, t, RW_W)) * gate.astype(f32)).astype(x.dtype)

    qx = _heads(xq, X_HEADS).astype(f32)
    sc = jnp.einsum('bthd,bmhd->bhtm', qx, mem_k.astype(f32)) * (X_DH ** -0.5)
    pr = jax.nn.softmax(sc, axis=-1)
    o_x = jnp.einsum('bhtm,bmhd->bthd', pr, mem_v.astype(f32)).reshape(bsz, t, X_W).astype(x.dtype)

    br = jnp.stack([o_gla, o_rw, o_x], axis=2)
    gates = jax.nn.sigmoid(gate_p.reshape(bsz, t, N_BRANCH, D_MODEL))
    merged = jnp.sum(gates * jnp.einsum('btnc,ncd->btnd', br, p['w_branch'][l]), axis=2)
    x = x + (merged @ p['w_out'][l]).astype(x.dtype)

    h2 = rmsnorm(x, p['g_norm2'][l])
    x = x + ((jax.nn.silu(h2 @ p['w_ff_gate'][l]) * (h2 @ p['w_ff_up'][l])) @ p['w_ff_down'][l]).astype(x.dtype)
    return x, gla_s.astype(x.dtype), rw_s.astype(x.dtype), new_shift.astype(x.dtype), v_first


def run_trunk(x, gla_s, rw_s, shift, mem_k, mem_v, p):
    v_first = None
    g_list, r_list, s_list = [], [], []
    for l in range(DEPTH):
        x, g_s, r_s, sh, v_first = hybrid_layer(x, l, gla_s[l], rw_s[l], shift[l], mem_k[l], mem_v[l], v_first, p)
        g_list.append(g_s)
        r_list.append(r_s)
        s_list.append(sh)
    y = rmsnorm(x, p['g_final'])
    return y, jnp.stack(g_list), jnp.stack(r_list), jnp.stack(s_list)


def setup_inputs(seed: int = 0) -> dict:
    key = jax.random.key(seed)
    ks = iter(jax.random.split(key, 48))
    f32 = jnp.float32

    def nrm(shape, scale):
        return scale * jax.random.normal(next(ks), shape, f32)

    def gain(shape):
        return 1.0 + 0.02 * jax.random.normal(next(ks), shape, f32)

    d = D_MODEL
    return {
        "x_prompt": nrm((BATCH, SEQ, d), 1.0),
        "x_sample": nrm((DEC_BATCH, DEC_SEQ, d), 1.0),
        "mem_prompt": nrm((BATCH, MEM_LEN, d), 1.0),
        "state_gla": nrm((DEPTH, DEC_BATCH, GLA_HEADS, GLA_DK, GLA_DV), 1.0),
        "state_rwkv": nrm((DEPTH, DEC_BATCH, RW_HEADS, RW_N, RW_N), 1.0),
        "state_rwkv_shift": nrm((DEPTH, DEC_BATCH, RW_COLS), 1.0),
        "cache_mem_k": nrm((DEPTH, DEC_BATCH, MEM_LEN, X_HEADS, X_DH), 1.0),
        "cache_mem_v": nrm((DEPTH, DEC_BATCH, MEM_LEN, X_HEADS, X_DH), 1.0),
        "g_norm1": gain((DEPTH, d)),
        "w_in": nrm((DEPTH, d, IN_COLS), d ** -0.5),
        "gla_wa2": nrm((DEPTH, GLA_RANK, GLA_KW), GLA_RANK ** -0.5),
        "gla_ba2": nrm((DEPTH, GLA_KW), 0.1),
        "gla_onorm": gain((DEPTH, GLA_DV)),
        "rw_mu": jax.random.uniform(next(ks), (DEPTH, RW_COLS), f32, 0.0, 1.0),
        "rw_w0": jax.random.uniform(next(ks), (DEPTH, RW_W), f32, -6.5, -1.5),
        "rw_w2": nrm((DEPTH, RW_DECAY_R, RW_W), 0.1 * RW_DECAY_R ** -0.5),
        "rw_a0": nrm((DEPTH, RW_W), 0.1),
        "rw_a2": nrm((DEPTH, RW_A_R, RW_W), 0.5 * RW_A_R ** -0.5),
        "rw_g2": nrm((DEPTH, RW_G_R, RW_W), RW_G_R ** -0.5),
        "rw_kk": 0.85 + 0.02 * jax.random.normal(next(ks), (DEPTH, RW_W), f32),
        "rw_ka": gain((DEPTH, RW_W)),
        "rw_rk": nrm((DEPTH, RW_HEADS, RW_N), 0.1),
        "rw_lnx_w": gain((DEPTH, RW_W)),
        "rw_lnx_b": nrm((DEPTH, RW_W), 0.02),
        "rw_v0": nrm((DEPTH - 1, RW_W), 0.1),
        "rw_v1": nrm((DEPTH - 1, d, RW_V_R), d ** -0.5),
        "rw_v2": nrm((DEPTH - 1, RW_V_R, RW_W), 0.5 * RW_V_R ** -0.5),
        "g_mem": gain((DEPTH, d)),
        "w_mk": nrm((DEPTH, d, X_W), d ** -0.5),
        "w_mv": nrm((DEPTH, d, X_W), d ** -0.5),
        "w_branch": nrm((DEPTH, N_BRANCH, 1024, d), 1024 ** -0.5),
        "w_out": nrm((DEPTH, d, d), d ** -0.5),
        "g_norm2": gain((DEPTH, d)),
        "w_ff_gate": nrm((DEPTH, d, D_FF), d ** -0.5),
        "w_ff_up": nrm((DEPTH, d, D_FF), d ** -0.5),
        "w_ff_down": nrm((DEPTH, D_FF, d), D_FF ** -0.5),
        "g_final": gain((d,)),
    }


def reference(x_prompt, x_sample, mem_prompt, state_gla, state_rwkv, state_rwkv_shift, cache_mem_k, cache_mem_v,
              g_norm1, w_in, gla_wa2, gla_ba2, gla_onorm, rw_mu, rw_w0, rw_w2, rw_a0, rw_a2, rw_g2, rw_kk, rw_ka,
              rw_rk, rw_lnx_w, rw_lnx_b, rw_v0, rw_v1, rw_v2, g_mem, w_mk, w_mv, w_branch, w_out, g_norm2,
              w_ff_gate, w_ff_up, w_ff_down, g_final):
    p = dict(g_norm1=g_norm1, w_in=w_in, gla_wa2=gla_wa2, gla_ba2=gla_ba2, gla_onorm=gla_onorm, rw_mu=rw_mu,
             rw_w0=rw_w0, rw_w2=rw_w2, rw_a0=rw_a0, rw_a2=rw_a2, rw_g2=rw_g2, rw_kk=rw_kk, rw_ka=rw_ka, rw_rk=rw_rk,
             rw_lnx_w=rw_lnx_w, rw_lnx_b=rw_lnx_b, rw_v0=rw_v0, rw_v1=rw_v1, rw_v2=rw_v2, w_branch=w_branch,
             w_out=w_out, g_norm2=g_norm2, w_ff_gate=w_ff_gate, w_ff_up=w_ff_up, w_ff_down=w_ff_down,
             g_final=g_final)
    dt = x_prompt.dtype
    bp = x_prompt.shape[0]

    mem_n = rmsnorm(mem_prompt[None], g_mem[:, None, None, :])
    p_mem_k = jnp.einsum('lbmd,lde->lbme', mem_n, w_mk).reshape(DEPTH, bp, MEM_LEN, X_HEADS, X_DH)
    p_mem_v = jnp.einsum('lbmd,lde->lbme', mem_n, w_mv).reshape(DEPTH, bp, MEM_LEN, X_HEADS, X_DH)
    gla0 = jnp.zeros((DEPTH, bp, GLA_HEADS, GLA_DK, GLA_DV), dt)
    rw0 = jnp.zeros((DEPTH, bp, RW_HEADS, RW_N, RW_N), dt)
    sh0 = jnp.zeros((DEPTH, bp, RW_COLS), dt)
    y_prompt, p_gla, p_rwkv, p_shift = run_trunk(x_prompt, gla0, rw0, sh0, p_mem_k, p_mem_v, p)

    y_sample, s_gla, s_rwkv, s_shift = run_trunk(x_sample, state_gla, state_rwkv, state_rwkv_shift,
                                                 cache_mem_k, cache_mem_v, p)
    return (y_prompt, y_sample, p_gla, p_rwkv, p_shift, p_mem_k, p_mem_v, s_gla, s_rwkv, s_shift)
```

```python
import functools

import jax
import jax.numpy as jnp
from jax import lax
from jax.experimental import pallas as pl
from jax.experimental.pallas import tpu as pltpu

F32 = jnp.float32
BF16 = jnp.bfloat16

GLA_HEADS, GLA_DK, GLA_DV = 4, 128, 256
GLA_KW, GLA_VW, GLA_RANK = GLA_HEADS * GLA_DK, GLA_HEADS * GLA_DV, 16
GLA_GATE_NORM = 16.0
RW_HEADS, RW_N = 16, 64
RW_W = RW_HEADS * RW_N
RW_DECAY_R, RW_A_R, RW_V_R, RW_G_R = 64, 64, 32, 160
RW_GN_EPS = 64e-5
MEM_LEN, X_HEADS, X_DH = 256, 4, 256
X_W = X_HEADS * X_DH
N_BRANCH = 3
NORM_EPS = 1e-6
GLA_COLS = 2 * GLA_KW + 2 * GLA_VW + GLA_RANK
RW_COLS = 3 * RW_W + RW_DECAY_R + RW_A_R + RW_G_R

LANES = 128
SUBLANES = 8
VMEM_LIMIT_BYTES = 56 * 1024 * 1024

C_GQ, C_GK, C_GV, C_GG = 0, 512, 1024, 2048
C_RR, C_RK, C_RV, C_XQ, C_GATE = 3072, 4096, 5120, 6144, 7168
SMALL_W = 1024
S_GA, S_XW, S_XA, S_XG, S_V1 = 0, 128, 256, 384, 640
SHIFT_W = 3 * RW_W + SMALL_W

SAMPLE_T_PAD = 8
RW_SLAB = 64
GLA_CHUNK = 64


def _cp(sem):
    return pltpu.CompilerParams(dimension_semantics=sem, vmem_limit_bytes=VMEM_LIMIT_BYTES)


def _pick(n, target, mult):
    best = None
    for d in range(mult, min(n, target) + 1, mult):
        if n % d == 0:
            best = d
    assert best is not None, (n, target, mult)
    return best


def _bdot(a, b):
    return jnp.dot(a.astype(BF16), b.astype(BF16), preferred_element_type=F32)


def _bdot_nt(a, b):
    return lax.dot_general(a.astype(BF16), b.astype(BF16), (((1,), (1,)), ((), ())), preferred_element_type=F32)


def _bdot_tn(a, b):
    return lax.dot_general(a.astype(BF16), b.astype(BF16), (((0,), (0,)), ((), ())), preferred_element_type=F32)


def _split2(a):
    hi = a.astype(BF16)
    lo = (a - hi.astype(F32)).astype(BF16)
    return hi, lo


def _dot3(a, b):
    ah, al = _split2(a)
    bh, bl = _split2(b)
    d = functools.partial(jnp.dot, preferred_element_type=F32)
    return d(ah, bh) + (d(ah, bl) + d(al, bh))


def _softplus(y):
    return jnp.maximum(y, 0.0) + jnp.log(1.0 + jnp.exp(-jnp.abs(y)))


def _sigmoid(y):
    return 1.0 / (1.0 + jnp.exp(-y))


def _cumsum_rows(x, block):
    n = x.shape[0]
    pos = lax.broadcasted_iota(jnp.int32, x.shape, 0) % block
    d = 1
    while d < block:
        x = x + jnp.where(pos >= d, pltpu.roll(x, d, axis=0), 0.0)
        d *= 2
    del n
    return x


def _rms_rows(x_ref, g_ref, h_ref, rows, eps):
    n = x_ref.shape[0] // rows

    def body(r, c):
        sl = pl.ds(pl.multiple_of(r * rows, rows), rows)
        x = x_ref[sl, :]
        ms = jnp.mean(x * x, axis=-1, keepdims=True)
        h_ref[sl, :] = (x * lax.rsqrt(ms + eps) * g_ref[...]).astype(h_ref.dtype)
        return c

    lax.fori_loop(0, n, body, 0)


def _norm_mm_body(x_ref, g_ref, w_ref, o_ref, h_ref, *, rows):
    @pl.when(pl.program_id(1) == 0)
    def _():
        _rms_rows(x_ref, g_ref, h_ref, rows, NORM_EPS)

    o_ref[...] = jnp.dot(h_ref[...], w_ref[...].astype(BF16), preferred_element_type=F32).astype(o_ref.dtype)


def _norm_mm(x, g3, w3, l, *, tm, tn, name):
    m, d = x.shape
    n = w3.shape[-1]
    return pl.pallas_call(
        functools.partial(_norm_mm_body, rows=_pick(tm, 128, SUBLANES)),
        grid=(m // tm, n // tn),
        in_specs=[pl.BlockSpec((tm, d), lambda i, j: (i, 0)),
                  pl.BlockSpec((None, 1, d), lambda i, j: (l, 0, 0)),
                  pl.BlockSpec((None, d, tn), lambda i, j: (l, 0, j))],
        out_specs=pl.BlockSpec((tm, tn), lambda i, j: (i, j)),
        out_shape=jax.ShapeDtypeStruct((m, n), F32),
        scratch_shapes=[pltpu.VMEM((tm, d), BF16)],
        compiler_params=_cp(("parallel", "arbitrary")),
        name=name,
    )(x, g3, w3)


def _merge_body(og_ref, or_ref, ox_ref, gg_ref, gr_ref, gx_ref, w_ref, o_ref):
    acc = _sigmoid(gg_ref[...]) * _bdot(og_ref[...], w_ref[0])
    acc += _sigmoid(gr_ref[...]) * _bdot(or_ref[...], w_ref[1])
    acc += _sigmoid(gx_ref[...]) * _bdot(ox_ref[...], w_ref[2])
    o_ref[...] = acc.astype(o_ref.dtype)


def _merge(o_gla, o_rw, o_x, proj, w_branch, l, *, tm, tn):
    m, bw = o_gla.shape
    d = w_branch.shape[-1]
    gate_blk = [(C_GATE + b * d) // tn for b in range(N_BRANCH)]
    o_spec = pl.BlockSpec((tm, bw), lambda i, j: (i, 0))
    return pl.pallas_call(
        _merge_body,
        grid=(m // tm, d // tn),
        in_specs=[o_spec, o_spec, o_spec]
        + [pl.BlockSpec((tm, tn), functools.partial(lambda i, j, off: (i, off + j), off=gate_blk[b]))
           for b in range(N_BRANCH)]
        + [pl.BlockSpec((None, N_BRANCH, bw, tn), lambda i, j: (l, 0, 0, j))],
        out_specs=pl.BlockSpec((tm, tn), lambda i, j: (i, j)),
        out_shape=jax.ShapeDtypeStruct((m, d), BF16),
        compiler_params=_cp(("parallel", "arbitrary")),
        name="branch_merge",
    )(o_gla, o_rw, o_x, proj, proj, proj, w_branch)


def _mm_res_body(a_ref, w_ref, r_ref, o_ref, acc_ref):
    k = pl.program_id(2)

    @pl.when(k == 0)
    def _():
        acc_ref[...] = r_ref[...]

    acc_ref[...] += _bdot(a_ref[...], w_ref[...])

    @pl.when(k == pl.num_programs(2) - 1)
    def _():
        o_ref[...] = acc_ref[...]


def _mm_res(a, w3, l, res, *, tm, tn, tk, name):
    m, kd = a.shape
    n = w3.shape[-1]
    return pl.pallas_call(
        _mm_res_body,
        grid=(m // tm, n // tn, kd // tk),
        in_specs=[pl.BlockSpec((tm, tk), lambda i, j, k: (i, k)),
                  pl.BlockSpec((None, tk, tn), lambda i, j, k: (l, k, j)),
                  pl.BlockSpec((tm, tn), lambda i, j, k: (i, j))],
        out_specs=pl.BlockSpec((tm, tn), lambda i, j, k: (i, j)),
        out_shape=jax.ShapeDtypeStruct((m, n), F32),
        scratch_shapes=[pltpu.VMEM((tm, tn), F32)],
        compiler_params=_cp(("parallel", "arbitrary", "arbitrary")),
        name=name,
    )(a, w3, res)


def _ffn_up_body(x_ref, g_ref, wg_ref, wu_ref, o_ref, h_ref, *, rows):
    @pl.when(pl.program_id(1) == 0)
    def _():
        _rms_rows(x_ref, g_ref, h_ref, rows, NORM_EPS)

    h = h_ref[...]
    a = jnp.dot(h, wg_ref[...].astype(BF16), preferred_element_type=F32)
    u = jnp.dot(h, wu_ref[...].astype(BF16), preferred_element_type=F32)
    o_ref[...] = (a * _sigmoid(a) * u).astype(o_ref.dtype)


def _ffn_up(x, g3, wg, wu, l, *, tm, tn):
    m, d = x.shape
    f = wg.shape[-1]
    w_spec = pl.BlockSpec((None, d, tn), lambda i, j: (l, 0, j))
    return pl.pallas_call(
        functools.partial(_ffn_up_body, rows=_pick(tm, 128, SUBLANES)),
        grid=(m // tm, f // tn),
        in_specs=[pl.BlockSpec((tm, d), lambda i, j: (i, 0)),
                  pl.BlockSpec((None, 1, d), lambda i, j: (l, 0, 0)),
                  w_spec, w_spec],
        out_specs=pl.BlockSpec((tm, tn), lambda i, j: (i, j)),
        out_shape=jax.ShapeDtypeStruct((m, f), BF16),
        scratch_shapes=[pltpu.VMEM((tm, d), BF16)],
        compiler_params=_cp(("parallel", "arbitrary")),
        name="ffn_up",
    )(x, g3, wg, wu)


def _final_norm_body(x_ref, g_ref, o_ref):
    x = x_ref[...]
    ms = jnp.mean(x * x, axis=-1, keepdims=True)
    o_ref[...] = x * lax.rsqrt(ms + NORM_EPS) * g_ref[...]


def _final_norm(x, g2):
    m, d = x.shape
    tm = _pick(m, 256, SUBLANES)
    return pl.pallas_call(
        _final_norm_body,
        grid=(m // tm,),
        in_specs=[pl.BlockSpec((tm, d), lambda i: (i, 0)), pl.BlockSpec((1, d), lambda i: (0, 0))],
        out_specs=pl.BlockSpec((tm, d), lambda i: (i, 0)),
        out_shape=jax.ShapeDtypeStruct((m, d), F32),
        compiler_params=_cp(("parallel",)),
        name="final_norm",
    )(x, g2)


def _gla_body(q_ref, k_ref, v_ref, g_ref, ga_ref, wa2_ref, ba2_ref, on_ref, s0_ref, o_ref, s_ref, of_ref, *, bb, ch,
              t_valid):
    @pl.when(pl.program_id(1) == 0)
    def _():
        s_ref[...] = s0_ref[...]

    row = lax.broadcasted_iota(jnp.int32, (ch, ch), 0)
    col = lax.broadcasted_iota(jnp.int32, (ch, ch), 1)
    causal = col <= row
    for s in range(bb):
        rows = slice(s * ch, (s + 1) * ch)
        z = _bdot(ga_ref[rows, :], wa2_ref[...]) + ba2_ref[...]
        la = -_softplus(-z) * (1.0 / GLA_GATE_NORM)
        kk = k_ref[rows, :]
        if t_valid < ch:
            valid = lax.broadcasted_iota(jnp.int32, la.shape, 0) < t_valid
            la = jnp.where(valid, la, 0.0)
            kk = jnp.where(valid, kk, 0.0)
        bc = _cumsum_rows(la, ch)
        blast = bc[ch - 1:ch, :]
        qe = q_ref[rows, :] * (GLA_DK ** -0.5) * jnp.exp(bc)
        ke = kk * jnp.exp(-bc)
        kl = kk * jnp.exp(blast - bc)
        dec_row = jnp.exp(blast)
        for h in range(GLA_HEADS):
            ks = slice(h * GLA_DK, (h + 1) * GLA_DK)
            vs = slice(h * GLA_DV, (h + 1) * GLA_DV)
            vh = v_ref[rows, vs]
            st = s_ref[s, h]
            att = jnp.where(causal, _bdot_nt(qe[:, ks], ke[:, ks]), 0.0)
            o = _bdot(qe[:, ks], st) + _bdot(att, vh)
            dec_col = jnp.broadcast_to(dec_row[:, ks], (GLA_DK, GLA_DK)).T
            dec = jnp.concatenate([dec_col] * (GLA_DV // GLA_DK), axis=1)
            s_ref[s, h] = st * dec + _bdot_tn(kl[:, ks], vh)
            ms = jnp.mean(o * o, axis=-1, keepdims=True)
            gh = g_ref[rows, vs]
            of_ref[rows, vs] = o * lax.rsqrt(ms + NORM_EPS) * on_ref[...] * (gh * _sigmoid(gh))
    o_ref[...] = of_ref[...].astype(o_ref.dtype)


def _gla(proj, wa2p, ba2, onorm, s0, l, *, r0, nseq, t, ch, bb, t_valid):
    rb = bb * ch
    nc = t // ch
    assert bb == 1 or nc == 1
    assert r0 % rb == 0

    def rblk(ib, c):
        return r0 // rb + ib * nc + c

    def sec(width, off):
        return pl.BlockSpec((rb, width), lambda ib, c: (rblk(ib, c), off // width))

    st_spec = pl.BlockSpec((bb, GLA_HEADS, GLA_DK, GLA_DV), lambda ib, c: (ib, 0, 0, 0))
    return pl.pallas_call(
        functools.partial(_gla_body, bb=bb, ch=ch, t_valid=t_valid),
        grid=(nseq // bb, nc),
        in_specs=[sec(GLA_KW, C_GQ), sec(GLA_KW, C_GK), sec(GLA_VW, C_GV), sec(GLA_VW, C_GG),
                  sec(LANES, C_GATE + (proj.shape[1] - C_GATE - SMALL_W) + S_GA),
                  pl.BlockSpec((None, LANES, GLA_KW), lambda ib, c: (l, 0, 0)),
                  pl.BlockSpec((None, 1, GLA_KW), lambda ib, c: (l, 0, 0)),
                  pl.BlockSpec((None, 1, GLA_DV), lambda ib, c: (l, 0, 0)),
                  st_spec],
        out_specs=[pl.BlockSpec((rb, GLA_VW), lambda ib, c: (ib * nc + c, 0)), st_spec],
        out_shape=[jax.ShapeDtypeStruct((nseq * t, GLA_VW), BF16),
                   jax.ShapeDtypeStruct((nseq, GLA_HEADS, GLA_DK, GLA_DV), F32)],
        scratch_shapes=[pltpu.VMEM((rb, GLA_VW), F32)],
        compiler_params=_cp(("parallel", "arbitrary")),
        name="gla",
    )(proj, proj, proj, proj, proj, wa2p, ba2, onorm, s0)


def _rwkv_body(*refs, bb, ch, t_valid, first):
    if first:
        (pr_ref, pk_ref, pv_ref, ps_ref, mu_ref, sh_ref, w0_ref, w2_ref, a0_ref, a2_ref, g2_ref,
         kkw_ref, kaw_ref, rk_ref, lnw_ref, lnb_ref, s0_ref, o_ref, so_ref, vf_out_ref, st_ref, carry_ref) = refs
    else:
        (pr_ref, pk_ref, pv_ref, ps_ref, vf_ref, mu_ref, sh_ref, w0_ref, w2_ref, a0_ref, a2_ref, g2_ref,
         v0_ref, v2_ref, kkw_ref, kaw_ref, rk_ref, lnw_ref, lnb_ref, s0_ref, o_ref, so_ref, st_ref, carry_ref) = refs
    nr = bb * ch
    npair = RW_HEADS // 2
    c_id = pl.program_id(1)
    lane = lax.broadcasted_iota(jnp.int32, (nr, LANES), 1)
    head_a = lane < RW_N
    r2 = lax.broadcasted_iota(jnp.int32, (2 * nr, 2 * nr), 0)
    c2 = lax.broadcasted_iota(jnp.int32, (2 * nr, 2 * nr), 1)
    same_blk = (r2 // ch) == (c2 // ch)
    strict = same_blk & (c2 < r2)
    incl = same_blk & (c2 <= r2)
    eye = (r2 == c2).astype(F32)
    blockdiag = (lax.broadcasted_iota(jnp.int32, (LANES, LANES), 0) < RW_N) == \
                (lax.broadcasted_iota(jnp.int32, (LANES, LANES), 1) < RW_N)
    zeros64 = jnp.zeros((RW_N, RW_N), F32)

    @pl.when(c_id == 0)
    def _():
        carry_ref[...] = sh_ref[:, 0, :]
        for s in range(bb):
            for p in range(npair):
                top = jnp.concatenate([s0_ref[s, 2 * p], zeros64], axis=1)
                bot = jnp.concatenate([zeros64, s0_ref[s, 2 * p + 1]], axis=1)
                st_ref[s, p] = jnp.concatenate([top, bot], axis=0)

    rowpos = lax.broadcasted_iota(jnp.int32, (nr, RW_W), 0) % ch

    def shifted(p_ref, off):
        p = p_ref[...]
        prev = pltpu.roll(p, 1, axis=0)
        first_rows = jnp.concatenate(
            [jnp.broadcast_to(carry_ref[s:s + 1, off:off + RW_W], (ch, RW_W)) for s in range(bb)], axis=0)
        prev = jnp.where(rowpos == 0, first_rows, prev)
        return p, p + mu_ref[:, off:off + RW_W] * (prev - p)

    p_r, f_r = shifted(pr_ref, 0)
    p_k, f_k = shifted(pk_ref, RW_W)
    p_v, f_v = shifted(pv_ref, 2 * RW_W)
    p_s, f_s = shifted(ps_ref, 3 * RW_W)
    for s in range(bb):
        last = s * ch + ch - 1
        carry_ref[s:s + 1, 0:RW_W] = p_r[last:last + 1, :]
        carry_ref[s:s + 1, RW_W:2 * RW_W] = p_k[last:last + 1, :]
        carry_ref[s:s + 1, 2 * RW_W:3 * RW_W] = p_v[last:last + 1, :]
        carry_ref[s:s + 1, 3 * RW_W:4 * RW_W] = p_s[last:last + 1, :]

    xw = f_s[:, S_XW:S_XW + LANES]
    xa = f_s[:, S_XA:S_XA + LANES]
    xg = f_s[:, S_XG:S_XG + 2 * LANES]
    w_log = -_softplus(-(w0_ref[...] + _bdot(jnp.tanh(xw), w2_ref[...]))) - 0.5
    lw = -jnp.exp(w_log)
    a = _sigmoid(a0_ref[...] + _bdot(xa, a2_ref[...]))
    gate = _bdot(_sigmoid(xg), g2_ref[...])
    if first:
        vr = f_v
        vf_out_ref[...] = vr
    else:
        nu = _sigmoid(v0_ref[...] + _bdot(p_s[:, S_V1:S_V1 + LANES], v2_ref[...]))
        vr = f_v + (vf_ref[...] - f_v) * nu
    kk = f_k * kkw_ref[...]
    kmod = f_k * (1.0 + (a - 1.0) * kaw_ref[...])
    if t_valid < ch:
        valid = rowpos < t_valid
        lw = jnp.where(valid, lw, 0.0)
        a = jnp.where(valid, a, 0.0)
        kmod = jnp.where(valid, kmod, 0.0)
    g_in = _cumsum_rows(lw, ch)
    g_ex = g_in - lw
    g_end = jnp.concatenate(
        [jnp.broadcast_to(g_in[s * ch + ch - 1:s * ch + ch, :], (ch, RW_W)) for s in range(bb)], axis=0)
    e_in, e_ex, e_neg, e_tail = jnp.exp(g_in), jnp.exp(g_ex), jnp.exp(-g_in), jnp.exp(g_end - g_in)
    e_end = jnp.exp(g_end)

    def stack(x):
        return jnp.concatenate([jnp.where(head_a, x, 0.0), jnp.where(head_a, 0.0, x)], axis=0)

    def unstack(x):
        return x[:nr] + x[nr:]

    def pair_sum(x):
        sa = jnp.sum(jnp.where(head_a, x, 0.0), axis=-1, keepdims=True)
        sb = jnp.sum(jnp.where(head_a, 0.0, x), axis=-1, keepdims=True)
        return jnp.where(head_a, sa, sb)

    for p in range(npair):
        ls = slice(p * LANES, (p + 1) * LANES)
        kk_p = kk[:, ls]
        kap = kk_p / jnp.maximum(jnp.sqrt(pair_sum(kk_p * kk_p)), 1e-12)
        r_p, k_p, v_p, a_p = f_r[:, ls], kmod[:, ls], vr[:, ls], a[:, ls]
        kap_t = kap * e_ex[:, ls]
        r_t = r_p * e_in[:, ls]
        b_t = kap * a_p * e_neg[:, ls]
        k_t = k_p * e_neg[:, ls]
        b_h = kap * a_p * e_tail[:, ls]
        k_h = k_p * e_tail[:, ls]

        lhs4 = jnp.concatenate([stack(kap_t), stack(r_t)], axis=0)
        rhs4 = jnp.concatenate([stack(b_t), stack(k_t)], axis=0)
        m4 = _bdot_nt(lhs4, rhs4)
        n2 = 2 * nr
        n_mat = jnp.where(strict, m4[:n2, :n2], 0.0)
        a_ak = jnp.where(strict, m4[:n2, n2:], 0.0)
        a_rb = jnp.where(incl, m4[n2:, :n2], 0.0)
        a_rk = jnp.where(incl, m4[n2:, n2:], 0.0)

        dinv = eye
        m = 1
        while m < ch:
            low = same_blk & ((r2 // (2 * m)) == (c2 // (2 * m))) & (((r2 // m) % 2) == 1) & (((c2 // m) % 2) == 0)
            l_m = jnp.where(low, n_mat, 0.0)
            dinv = dinv - _dot3(_dot3(dinv, l_m), dinv)
            m *= 2

        v_st = stack(v_p)
        ks_parts, rs_parts = [], []
        for s in range(bb):
            rows = slice(s * ch, (s + 1) * ch)
            st = st_ref[s, p]
            both = _bdot_nt(jnp.concatenate([kap_t[rows], r_t[rows]], axis=0), st)
            ks_parts.append(both[:ch])
            rs_parts.append(both[ch:])
        ks_all = ks_parts[0] if bb == 1 else jnp.concatenate(ks_parts, axis=0)
        rs_all = rs_parts[0] if bb == 1 else jnp.concatenate(rs_parts, axis=0)
        rhs = stack(ks_all) + _bdot(a_ak, v_st)
        u_st = -_dot3(dinv, rhs)
        uv = jnp.concatenate([u_st, v_st], axis=0)
        y = unstack(stack(rs_all) + _bdot(jnp.concatenate([a_rb, a_rk], axis=1), uv))
        u_p = unstack(u_st)
        for s in range(bb):
            rows = slice(s * ch, (s + 1) * ch)
            upd = _bdot_tn(jnp.concatenate([u_p[rows], v_p[rows]], axis=0),
                           jnp.concatenate([b_h[rows], k_h[rows]], axis=0))
            dec = e_end[s * ch:s * ch + 1, ls]
            st_ref[s, p] = st_ref[s, p] * dec + jnp.where(blockdiag, upd, 0.0)

        mean = pair_sum(y) * (1.0 / RW_N)
        yc = y - mean
        var = pair_sum(yc * yc) * (1.0 / RW_N)
        yn = yc * lax.rsqrt(var + RW_GN_EPS) * lnw_ref[:, ls] + lnb_ref[:, ls]
        bonus = pair_sum(r_p * k_p * rk_ref[:, ls]) * v_p
        o_ref[:, ls] = ((yn + bonus) * gate[:, ls]).astype(o_ref.dtype)

    @pl.when(c_id == pl.num_programs(1) - 1)
    def _():
        for s in range(bb):
            for p in range(npair):
                st = st_ref[s, p]
                so_ref[s, 2 * p] = st[:RW_N, :RW_N]
                so_ref[s, 2 * p + 1] = st[RW_N:, RW_N:]


def _rwkv(proj, vf, shift, s0, wts, l, *, r0, nseq, t, ch, bb, t_valid):
    first = vf is None
    rb = bb * ch
    nc = t // ch
    assert rb == RW_SLAB and (bb == 1 or nc == 1) and r0 % rb == 0
    small_off = proj.shape[1] - SMALL_W

    def rblk(ib, c):
        return r0 // rb + ib * nc + c

    def sec(off):
        return pl.BlockSpec((rb, RW_W), lambda ib, c: (rblk(ib, c), off // RW_W))

    def vec(width):
        return pl.BlockSpec((None, 1, width), lambda ib, c: (l, 0, 0))

    def mat(rows):
        return pl.BlockSpec((None, rows, RW_W), lambda ib, c: (l, 0, 0))

    out_rows = pl.BlockSpec((rb, RW_W), lambda ib, c: (ib * nc + c, 0))
    st_spec = pl.BlockSpec((bb, RW_HEADS, RW_N, RW_N), lambda ib, c: (ib, 0, 0, 0))
    in_specs = [sec(C_RR), sec(C_RK), sec(C_RV), sec(small_off)]
    args = [proj, proj, proj, proj]
    if not first:
        in_specs.append(out_rows)
        args.append(vf)
    in_specs += [vec(SHIFT_W), pl.BlockSpec((bb, 1, SHIFT_W), lambda ib, c: (ib, 0, 0)),
                 vec(RW_W), mat(LANES), vec(RW_W), mat(LANES), mat(2 * LANES)]
    args += [wts["mu"], shift, wts["w0"], wts["w2"], wts["a0"], wts["a2"], wts["g2"]]
    if not first:
        in_specs += [pl.BlockSpec((None, 1, RW_W), lambda ib, c: (l - 1, 0, 0)),
                     pl.BlockSpec((None, LANES, RW_W), lambda ib, c: (l - 1, 0, 0))]
        args += [wts["v0"], wts["v2"]]
    in_specs += [vec(RW_W)] * 5 + [st_spec]
    args += [wts["kk"], wts["ka"], wts["rk"], wts["lnw"], wts["lnb"], s0]
    out_specs = [out_rows, st_spec]
    out_shape = [jax.ShapeDtypeStruct((nseq * t, RW_W), BF16),
                 jax.ShapeDtypeStruct((nseq, RW_HEADS, RW_N, RW_N), F32)]
    if first:
        out_specs.append(out_rows)
        out_shape.append(jax.ShapeDtypeStruct((nseq * t, RW_W), F32))
    res = pl.pallas_call(
        functools.partial(_rwkv_body, bb=bb, ch=ch, t_valid=t_valid, first=first),
        grid=(nseq // bb, nc),
        in_specs=in_specs,
        out_specs=out_specs,
        out_shape=out_shape,
        scratch_shapes=[pltpu.VMEM((bb, RW_HEADS // 2, LANES, LANES), F32), pltpu.VMEM((bb, SHIFT_W), F32)],
        compiler_params=_cp(("parallel", "arbitrary")),
        name="rwkv7",
    )(*args)
    return (res[0], res[1], res[2]) if first else (res[0], res[1], None)


def _xattn_body(q_ref, k_ref, v_ref, o_ref, of_ref, *, bb, tq):
    for s in range(bb):
        rows = slice(s * tq, (s + 1) * tq)
        for h in range(X_HEADS):
            cs = slice(h * X_DH, (h + 1) * X_DH)
            sc = _bdot_nt(q_ref[rows, cs], k_ref[s, :, cs]) * (X_DH ** -0.5)
            sc = sc - jnp.max(sc, axis=-1, keepdims=True)
            e = jnp.exp(sc)
            pr = e / jnp.sum(e, axis=-1, keepdims=True)
            of_ref[rows, cs] = _bdot(pr, v_ref[s, :, cs])
    o_ref[...] = of_ref[...].astype(o_ref.dtype)


def _xattn(proj, mem_k, mem_v, l, *, r0, nseq, t, tq, bb):
    rb = bb * tq
    nq = t // tq
    assert (bb == 1 or nq == 1) and r0 % rb == 0
    kv_spec = pl.BlockSpec((None, bb, MEM_LEN, X_W), lambda ib, c: (l, ib, 0, 0))
    return pl.pallas_call(
        functools.partial(_xattn_body, bb=bb, tq=tq),
        grid=(nseq // bb, nq),
        in_specs=[pl.BlockSpec((rb, X_W), lambda ib, c: (r0 // rb + ib * nq + c, C_XQ // X_W)), kv_spec, kv_spec],
        out_specs=pl.BlockSpec((rb, X_W), lambda ib, c: (ib * nq + c, 0)),
        out_shape=jax.ShapeDtypeStruct((nseq * t, X_W), BF16),
        scratch_shapes=[pltpu.VMEM((rb, X_W), F32)],
        compiler_params=_cp(("parallel", "arbitrary")),
        name="mem_xattn",
    )(proj, mem_k, mem_v)


def _pad_last(a, width):
    return jnp.pad(a, [(0, 0)] * (a.ndim - 1) + [(0, width - a.shape[-1])])


def _pad_rows(a, rows):
    return jnp.pad(a, [(0, 0)] * (a.ndim - 2) + [(0, rows - a.shape[-2]), (0, 0)])


def _small_section(ga, xw, xa, xg, v1):
    return jnp.concatenate([_pad_last(ga, S_XW - S_GA), _pad_last(xw, S_XA - S_XW), _pad_last(xa, S_XG - S_XA),
                            _pad_last(xg, S_V1 - S_XG), _pad_last(v1, SMALL_W - S_V1)], axis=-1)


def _relayout_in_cols(a, v1):
    rw = GLA_COLS
    sm = rw + 3 * RW_W
    xq = rw + RW_COLS
    gt = xq + X_W
    small = _small_section(a[..., 3072:GLA_COLS], a[..., sm:sm + RW_DECAY_R],
                           a[..., sm + RW_DECAY_R:sm + RW_DECAY_R + RW_A_R],
                           a[..., sm + RW_DECAY_R + RW_A_R:xq], v1)
    return jnp.concatenate([a[..., 0:3072], a[..., rw:sm], a[..., xq:gt], a[..., gt:], small], axis=-1)


def _shift_layout(a):
    z16 = jnp.zeros(a.shape[:-1] + (GLA_RANK,), a.dtype)
    z32 = jnp.zeros(a.shape[:-1] + (RW_V_R,), a.dtype)
    o = 3 * RW_W
    small = _small_section(z16, a[..., o:o + RW_DECAY_R], a[..., o + RW_DECAY_R:o + RW_DECAY_R + RW_A_R],
                           a[..., o + RW_DECAY_R + RW_A_R:], z32)
    return jnp.concatenate([a[..., :o], small], axis=-1)


def _shift_from_proj(rows, small_off):
    sm = rows[:, small_off:]
    return jnp.concatenate([rows[:, C_RR:C_RR + 3 * RW_W], sm[:, S_XW:S_XW + RW_DECAY_R], sm[:, S_XA:S_XA + RW_A_R],
                            sm[:, S_XG:S_XG + RW_G_R]], axis=-1)


def kernel(x_prompt, x_sample, mem_prompt, state_gla, state_rwkv, state_rwkv_shift, cache_mem_k, cache_mem_v,
           g_norm1, w_in, gla_wa2, gla_ba2, gla_onorm, rw_mu, rw_w0, rw_w2, rw_a0, rw_a2, rw_g2, rw_kk, rw_ka,
           rw_rk, rw_lnx_w, rw_lnx_b, rw_v0, rw_v1, rw_v2, g_mem, w_mk, w_mv, w_branch, w_out, g_norm2,
           w_ff_gate, w_ff_up, w_ff_down, g_final):
    depth = w_in.shape[0]
    bp, tp, d = x_prompt.shape
    bs, ts, _ = x_sample.shape
    assert d % 1024 == 0 and ts <= SAMPLE_T_PAD and tp % RW_SLAB == 0 and bs % (RW_SLAB // SAMPLE_T_PAD) == 0
    mp, ms = bp * tp, bs * SAMPLE_T_PAD
    m = mp + ms
    f = w_ff_gate.shape[-1]

    v1_all = jnp.concatenate([jnp.zeros((1, d, RW_V_R), F32), rw_v1], axis=0)
    w_in_r = _relayout_in_cols(w_in, v1_all).astype(BF16)
    n_cols = w_in_r.shape[-1]
    small_off = n_cols - SMALL_W
    r3 = lambda a: a.reshape(a.shape[0], 1, -1)
    rw_wts = dict(mu=r3(_shift_layout(rw_mu)), w0=r3(rw_w0), w2=_pad_rows(rw_w2, LANES), a0=r3(rw_a0),
                  a2=_pad_rows(rw_a2, LANES), g2=_pad_rows(rw_g2, 2 * LANES), kk=r3(rw_kk), ka=r3(rw_ka),
                  rk=r3(rw_rk), lnw=r3(rw_lnx_w), lnb=r3(rw_lnx_b), v0=r3(rw_v0), v2=_pad_rows(rw_v2, LANES))
    wa2p = _pad_rows(gla_wa2, LANES)
    ba2, onorm = r3(gla_ba2), r3(gla_onorm)
    g1, g2n, gm = r3(g_norm1), r3(g_norm2), r3(g_mem)

    xs_pad = jnp.pad(x_sample, ((0, 0), (0, SAMPLE_T_PAD - ts), (0, 0)))
    x = jnp.concatenate([x_prompt.reshape(mp, d), xs_pad.reshape(ms, d)], axis=0)

    tm = _pick(m, 1152, 128) if m % 128 == 0 else _pick(m, 1152, 16)
    tn_in = _pick(n_cols, 512, LANES)
    tn_d = _pick(d, 512, LANES)
    tn_f = _pick(f, 512, LANES)

    mem_rows = mem_prompt.reshape(bp * MEM_LEN, d)
    tmm = _pick(bp * MEM_LEN, 1024, SUBLANES)
    p_mem_k = jnp.stack([_norm_mm(mem_rows, gm, w_mk, l, tm=tmm, tn=tn_d, name="mem_k") for l in range(depth)])
    p_mem_v = jnp.stack([_norm_mm(mem_rows, gm, w_mv, l, tm=tmm, tn=tn_d, name="mem_v") for l in range(depth)])
    p_mem_k = p_mem_k.reshape(depth, bp, MEM_LEN, X_W)
    p_mem_v = p_mem_v.reshape(depth, bp, MEM_LEN, X_W)
    s_mem_k = cache_mem_k.reshape(depth, bs, MEM_LEN, X_W)
    s_mem_v = cache_mem_v.reshape(depth, bs, MEM_LEN, X_W)

    gla0 = jnp.zeros((bp, GLA_HEADS, GLA_DK, GLA_DV), F32)
    rw0 = jnp.zeros((bp, RW_HEADS, RW_N, RW_N), F32)
    sh0 = jnp.zeros((bp, 1, SHIFT_W), F32)
    s_shift_in = _shift_layout(state_rwkv_shift).reshape(depth, bs, 1, SHIFT_W)

    bb_s = RW_SLAB // SAMPLE_T_PAD
    tq_p = _pick(tp, 512, SUBLANES)
    vf_p = vf_s = None
    outs = dict(p_gla=[], p_rwkv=[], p_shift=[], s_gla=[], s_rwkv=[], s_shift=[])
    for l in range(depth):
        proj = _norm_mm(x, g1, w_in_r, l, tm=tm, tn=tn_in, name="in_proj")

        og_p, sg_p = _gla(proj, wa2p, ba2, onorm, gla0, l, r0=0, nseq=bp, t=tp, ch=GLA_CHUNK, bb=1, t_valid=GLA_CHUNK)
        og_s, sg_s = _gla(proj, wa2p, ba2, onorm, state_gla[l], l, r0=mp, nseq=bs, t=SAMPLE_T_PAD,
                          ch=SAMPLE_T_PAD, bb=bb_s, t_valid=ts)
        or_p, sr_p, vf_new_p = _rwkv(proj, vf_p, sh0, rw0, rw_wts, l, r0=0, nseq=bp, t=tp, ch=RW_SLAB, bb=1,
                                     t_valid=RW_SLAB)
        or_s, sr_s, vf_new_s = _rwkv(proj, vf_s, s_shift_in[l], state_rwkv[l], rw_wts, l, r0=mp, nseq=bs,
                                     t=SAMPLE_T_PAD, ch=SAMPLE_T_PAD, bb=bb_s, t_valid=ts)
        if l == 0:
            vf_p, vf_s = vf_new_p, vf_new_s
        ox_p = _xattn(proj, p_mem_k, p_mem_v, l, r0=0, nseq=bp, t=tp, tq=tq_p, bb=1)
        ox_s = _xattn(proj, s_mem_k, s_mem_v, l, r0=mp, nseq=bs, t=SAMPLE_T_PAD, tq=SAMPLE_T_PAD, bb=bb_s)

        o_gla = jnp.concatenate([og_p, og_s], axis=0)
        o_rw = jnp.concatenate([or_p, or_s], axis=0)
        o_x = jnp.concatenate([ox_p, ox_s], axis=0)
        merged = _merge(o_gla, o_rw, o_x, proj, w_branch, l, tm=tm, tn=tn_d)
        x = _mm_res(merged, w_out, l, x, tm=tm, tn=tn_d, tk=d, name="out_proj")
        act = _ffn_up(x, g2n, w_ff_gate, w_ff_up, l, tm=tm, tn=tn_f)
        x = _mm_res(act, w_ff_down, l, x, tm=tm, tn=tn_d, tk=_pick(f, 2048, LANES), name="ffn_down")

        outs["p_gla"].append(sg_p)
        outs["s_gla"].append(sg_s)
        outs["p_rwkv"].append(sr_p)
        outs["s_rwkv"].append(sr_s)
        last_p = proj[:mp].reshape(bp, tp, n_cols)[:, tp - 1]
        last_s = proj[mp:].reshape(bs, SAMPLE_T_PAD, n_cols)[:, ts - 1]
        outs["p_shift"].append(_shift_from_proj(last_p, small_off))
        outs["s_shift"].append(_shift_from_proj(last_s, small_off))

    y = _final_norm(x, g_final.reshape(1, d))
    y_prompt = y[:mp].reshape(bp, tp, d)
    y_sample = y[mp:].reshape(bs, SAMPLE_T_PAD, d)[:, :ts]
    st = {k: jnp.stack(v) for k, v in outs.items()}
    return (y_prompt, y_sample, st["p_gla"], st["p_rwkv"], st["p_shift"],
            p_mem_k.reshape(depth, bp, MEM_LEN, X_HEADS, X_DH), p_mem_v.reshape(depth, bp, MEM_LEN, X_HEADS, X_DH),
            st["s_gla"], st["s_rwkv"], st["s_shift"])
```

```python
import functools

import jax
import jax.numpy as jnp
from jax import lax
from jax.experimental import pallas as pl
from jax.experimental.pallas import tpu as pltpu

F32 = jnp.float32
BF16 = jnp.bfloat16

GLA_HEADS, GLA_DK, GLA_DV = 4, 128, 256
GLA_KW, GLA_VW, GLA_RANK = GLA_HEADS * GLA_DK, GLA_HEADS * GLA_DV, 16
GLA_GATE_NORM = 16.0
RW_HEADS, RW_N = 16, 64
RW_W = RW_HEADS * RW_N
RW_DECAY_R, RW_A_R, RW_V_R, RW_G_R = 64, 64, 32, 160
RW_GN_EPS = 64e-5
MEM_LEN, X_HEADS, X_DH = 256, 4, 256
X_W = X_HEADS * X_DH
N_BRANCH = 3
NORM_EPS = 1e-6
GLA_COLS = 2 * GLA_KW + 2 * GLA_VW + GLA_RANK
RW_COLS = 3 * RW_W + RW_DECAY_R + RW_A_R + RW_G_R

LANES = 128
SUBLANES = 8
VMEM_LIMIT_BYTES = 56 * 1024 * 1024

C_GQ, C_GK, C_GV, C_GG = 0, 512, 1024, 2048
C_RR, C_RK, C_RV, C_XQ, C_GATE = 3072, 4096, 5120, 6144, 7168
SMALL_W = 1024
S_GA, S_XW, S_XA, S_XG, S_V1 = 0, 128, 256, 384, 640
SHIFT_W = 3 * RW_W + SMALL_W

SAMPLE_T_PAD = 8
RW_SLAB = 64
RW_PAIR_GROUP = 4
GLA_CHUNK = 64


def _cp(sem):
    return pltpu.CompilerParams(dimension_semantics=sem, vmem_limit_bytes=VMEM_LIMIT_BYTES)


def _pick(n, target, mult):
    best = None
    for d in range(mult, min(n, target) + 1, mult):
        if n % d == 0:
            best = d
    assert best is not None, (n, target, mult)
    return best


def _bdot(a, b):
    return jnp.dot(a.astype(BF16), b.astype(BF16), preferred_element_type=F32)


def _bdot_nt(a, b):
    return lax.dot_general(a.astype(BF16), b.astype(BF16), (((1,), (1,)), ((), ())), preferred_element_type=F32)


def _bdot_tn(a, b):
    return lax.dot_general(a.astype(BF16), b.astype(BF16), (((0,), (0,)), ((), ())), preferred_element_type=F32)


def _softplus(y):
    return jnp.maximum(y, 0.0) + jnp.log(1.0 + jnp.exp(-jnp.abs(y)))


def _sigmoid(y):
    return 1.0 / (1.0 + jnp.exp(-y))


def _cumsum_rows(x, block):
    pos = lax.broadcasted_iota(jnp.int32, x.shape, 0) % block
    d = 1
    while d < block:
        x = x + jnp.where(pos >= d, pltpu.roll(x, d, axis=0), 0.0)
        d *= 2
    return x


def _last_row_of_each(x, bb, ch):
    parts = [jnp.broadcast_to(x[s * ch + ch - 1:s * ch + ch, :], (ch, x.shape[1])) for s in range(bb)]
    return parts[0] if bb == 1 else jnp.concatenate(parts, axis=0)


def _alias_inputs(bufs):
    arrays, specs, alias = [], [], {}
    for out_idx, buf in enumerate(bufs):
        if buf is not None:
            alias[len(arrays)] = out_idx
            arrays.append(buf)
            specs.append(pl.BlockSpec(memory_space=pl.ANY))
    return arrays, specs, alias


def _rms_rows(x_ref, g_ref, h_ref, rows, eps):
    n = x_ref.shape[0] // rows

    def body(r, c):
        sl = pl.ds(pl.multiple_of(r * rows, rows), rows)
        x = x_ref[sl, :]
        ms = jnp.mean(x * x, axis=-1, keepdims=True)
        h_ref[sl, :] = (x * lax.rsqrt(ms + eps) * g_ref[...]).astype(h_ref.dtype)
        return c

    lax.fori_loop(0, n, body, 0)


def _norm_mm_body(x_ref, g_ref, w_ref, o_ref, h_ref, *, rows):
    @pl.when(pl.program_id(1) == 0)
    def _():
        _rms_rows(x_ref, g_ref, h_ref, rows, NORM_EPS)

    o_ref[...] = jnp.dot(h_ref[...], w_ref[...].astype(BF16), preferred_element_type=F32).astype(o_ref.dtype)


def _norm_mm(x, g3, w3, l, *, tm, tn, name):
    m, d = x.shape
    n = w3.shape[-1]
    return pl.pallas_call(
        functools.partial(_norm_mm_body, rows=_pick(tm, 128, SUBLANES)),
        grid=(m // tm, n // tn),
        in_specs=[pl.BlockSpec((tm, d), lambda i, j: (i, 0)),
                  pl.BlockSpec((None, 1, d), lambda i, j: (l, 0, 0)),
                  pl.BlockSpec((None, d, tn), lambda i, j: (l, 0, j))],
        out_specs=pl.BlockSpec((tm, tn), lambda i, j: (i, j)),
        out_shape=jax.ShapeDtypeStruct((m, n), F32),
        scratch_shapes=[pltpu.VMEM((tm, d), BF16)],
        compiler_params=_cp(("parallel", "arbitrary")),
        name=name,
    )(x, g3, w3)


def _merge_body(og_ref, or_ref, ox_ref, gg_ref, gr_ref, gx_ref, w_ref, o_ref):
    acc = _sigmoid(gg_ref[...]) * _bdot(og_ref[...], w_ref[0])
    acc += _sigmoid(gr_ref[...]) * _bdot(or_ref[...], w_ref[1])
    acc += _sigmoid(gx_ref[...]) * _bdot(ox_ref[...], w_ref[2])
    o_ref[...] = acc.astype(o_ref.dtype)


def _merge(o_gla, o_rw, o_x, proj, w_branch, l, *, tm, tn):
    m, bw = o_gla.shape
    d = w_branch.shape[-1]
    gate_blk = [(C_GATE + b * d) // tn for b in range(N_BRANCH)]
    o_spec = pl.BlockSpec((tm, bw), lambda i, j: (i, 0))
    return pl.pallas_call(
        _merge_body,
        grid=(m // tm, d // tn),
        in_specs=[o_spec, o_spec, o_spec]
        + [pl.BlockSpec((tm, tn), functools.partial(lambda i, j, off: (i, off + j), off=gate_blk[b]))
           for b in range(N_BRANCH)]
        + [pl.BlockSpec((None, N_BRANCH, bw, tn), lambda i, j: (l, 0, 0, j))],
        out_specs=pl.BlockSpec((tm, tn), lambda i, j: (i, j)),
        out_shape=jax.ShapeDtypeStruct((m, d), BF16),
        compiler_params=_cp(("parallel", "arbitrary")),
        name="branch_merge",
    )(o_gla, o_rw, o_x, proj, proj, proj, w_branch)


def _mm_res_body(a_ref, w_ref, r_ref, o_ref, acc_ref):
    k = pl.program_id(2)

    @pl.when(k == 0)
    def _():
        acc_ref[...] = r_ref[...]

    acc_ref[...] += _bdot(a_ref[...], w_ref[...])

    @pl.when(k == pl.num_programs(2) - 1)
    def _():
        o_ref[...] = acc_ref[...]


def _mm_res(a, w3, l, res, *, tm, tn, tk, name):
    m, kd = a.shape
    n = w3.shape[-1]
    return pl.pallas_call(
        _mm_res_body,
        grid=(m // tm, n // tn, kd // tk),
        in_specs=[pl.BlockSpec((tm, tk), lambda i, j, k: (i, k)),
                  pl.BlockSpec((None, tk, tn), lambda i, j, k: (l, k, j)),
                  pl.BlockSpec((tm, tn), lambda i, j, k: (i, j))],
        out_specs=pl.BlockSpec((tm, tn), lambda i, j, k: (i, j)),
        out_shape=jax.ShapeDtypeStruct((m, n), F32),
        scratch_shapes=[pltpu.VMEM((tm, tn), F32)],
        compiler_params=_cp(("parallel", "arbitrary", "arbitrary")),
        name=name,
    )(a, w3, res)


def _ffn_up_body(x_ref, g_ref, wg_ref, wu_ref, o_ref, h_ref, *, rows):
    @pl.when(pl.program_id(1) == 0)
    def _():
        _rms_rows(x_ref, g_ref, h_ref, rows, NORM_EPS)

    h = h_ref[...]
    a = jnp.dot(h, wg_ref[...].astype(BF16), preferred_element_type=F32)
    u = jnp.dot(h, wu_ref[...].astype(BF16), preferred_element_type=F32)
    o_ref[...] = (a * _sigmoid(a) * u).astype(o_ref.dtype)


def _ffn_up(x, g3, wg, wu, l, *, tm, tn):
    m, d = x.shape
    f = wg.shape[-1]
    w_spec = pl.BlockSpec((None, d, tn), lambda i, j: (l, 0, j))
    return pl.pallas_call(
        functools.partial(_ffn_up_body, rows=_pick(tm, 128, SUBLANES)),
        grid=(m // tm, f // tn),
        in_specs=[pl.BlockSpec((tm, d), lambda i, j: (i, 0)),
                  pl.BlockSpec((None, 1, d), lambda i, j: (l, 0, 0)),
                  w_spec, w_spec],
        out_specs=pl.BlockSpec((tm, tn), lambda i, j: (i, j)),
        out_shape=jax.ShapeDtypeStruct((m, f), BF16),
        scratch_shapes=[pltpu.VMEM((tm, d), BF16)],
        compiler_params=_cp(("parallel", "arbitrary")),
        name="ffn_up",
    )(x, g3, wg, wu)


def _final_norm_body(x_ref, g_ref, o_ref):
    x = x_ref[...]
    ms = jnp.mean(x * x, axis=-1, keepdims=True)
    o_ref[...] = x * lax.rsqrt(ms + NORM_EPS) * g_ref[...]


def _final_norm(x, g2):
    m, d = x.shape
    tm = _pick(m, 256, SUBLANES)
    return pl.pallas_call(
        _final_norm_body,
        grid=(m // tm,),
        in_specs=[pl.BlockSpec((tm, d), lambda i: (i, 0)), pl.BlockSpec((1, d), lambda i: (0, 0))],
        out_specs=pl.BlockSpec((tm, d), lambda i: (i, 0)),
        out_shape=jax.ShapeDtypeStruct((m, d), F32),
        compiler_params=_cp(("parallel",)),
        name="final_norm",
    )(x, g2)


def _gla_body(*refs, names, bb, ch, t_valid, zero_init):
    r = dict(zip(names, refs))
    nr = bb * ch
    s_ref = r["s_out"]

    @pl.when(pl.program_id(1) == 0)
    def _():
        if zero_init:
            s_ref[...] = jnp.zeros(s_ref.shape, F32)
        else:
            s_ref[...] = r["s0"][...]

    row = lax.broadcasted_iota(jnp.int32, (ch, ch), 0)
    col = lax.broadcasted_iota(jnp.int32, (ch, ch), 1)
    causal = col <= row
    z = _bdot(r["ga"][...], r["wa2"][...]) + r["ba2"][...]
    la = -_softplus(-z) * (1.0 / GLA_GATE_NORM)
    kk = r["k"][...]
    if t_valid < ch:
        valid = (lax.broadcasted_iota(jnp.int32, la.shape, 0) % ch) < t_valid
        la = jnp.where(valid, la, 0.0)
        kk = jnp.where(valid, kk, 0.0)
    bc = _cumsum_rows(la, ch)
    b_end = _last_row_of_each(bc, bb, ch)
    qe = r["q"][...] * (GLA_DK ** -0.5) * jnp.exp(bc)
    ke = kk * jnp.exp(-bc)
    kl = kk * jnp.exp(b_end - bc)
    dec_rows = jnp.exp(b_end)

    units = [(s, h) for s in range(bb) for h in range(GLA_HEADS)]

    def sl(s, h):
        return (slice(s * ch, (s + 1) * ch), slice(h * GLA_DK, (h + 1) * GLA_DK),
                slice(h * GLA_DV, (h + 1) * GLA_DV))

    att, o_int, upd = {}, {}, {}
    for s, h in units:
        rows, ks, vs = sl(s, h)
        att[s, h] = jnp.where(causal, _bdot_nt(qe[rows, ks], ke[rows, ks]), 0.0)
        o_int[s, h] = _bdot(qe[rows, ks], s_ref[s, h])
    for s, h in units:
        rows, ks, vs = sl(s, h)
        vh = r["v"][rows, vs]
        o = o_int[s, h] + _bdot(att[s, h], vh)
        upd[s, h] = _bdot_tn(kl[rows, ks], vh)
        ms = jnp.mean(o * o, axis=-1, keepdims=True)
        gh = r["g"][rows, vs]
        r["of"][rows, vs] = o * lax.rsqrt(ms + NORM_EPS) * r["on"][...] * (gh * _sigmoid(gh))
    for s, h in units:
        rows, ks, vs = sl(s, h)
        dec_col = jnp.broadcast_to(dec_rows[s * ch:s * ch + 1, ks], (GLA_DK, GLA_DK)).T
        dec = jnp.concatenate([dec_col] * (GLA_DV // GLA_DK), axis=1)
        s_ref[s, h] = s_ref[s, h] * dec + upd[s, h]
    r["o"][...] = r["of"][...].astype(r["o"].dtype)
    del nr


def _gla(proj, wa2p, ba2, onorm, s0, o_buf, st_buf, l, *, depth, m_total, r0, nseq, t, ch, bb, t_valid):
    rb = bb * ch
    nc = t // ch
    assert (bb == 1 or nc == 1) and r0 % rb == 0

    def rblk(ib, c):
        return r0 // rb + ib * nc + c

    def sec(width, off):
        return pl.BlockSpec((rb, width), lambda ib, c: (rblk(ib, c), off // width))

    st_spec = pl.BlockSpec((None, bb, GLA_HEADS, GLA_DK, GLA_DV), lambda ib, c: (l, ib, 0, 0, 0))
    names = ["q", "k", "v", "g", "ga", "wa2", "ba2", "on"]
    args = [proj, proj, proj, proj, proj, wa2p, ba2, onorm]
    in_specs = [sec(GLA_KW, C_GQ), sec(GLA_KW, C_GK), sec(GLA_VW, C_GV), sec(GLA_VW, C_GG),
                sec(LANES, proj.shape[1] - SMALL_W + S_GA),
                pl.BlockSpec((None, LANES, GLA_KW), lambda ib, c: (l, 0, 0)),
                pl.BlockSpec((None, 1, GLA_KW), lambda ib, c: (l, 0, 0)),
                pl.BlockSpec((None, 1, GLA_DV), lambda ib, c: (l, 0, 0))]
    if s0 is not None:
        names.append("s0")
        args.append(s0)
        in_specs.append(st_spec)
    al_arrays, al_specs, alias = _alias_inputs([o_buf, st_buf])
    alias = {len(args) + k: v for k, v in alias.items()}
    names += ["alias%d" % i for i in range(len(al_arrays))] + ["o", "s_out", "of"]
    return pl.pallas_call(
        functools.partial(_gla_body, names=names, bb=bb, ch=ch, t_valid=t_valid, zero_init=s0 is None),
        grid=(nseq // bb, nc),
        in_specs=in_specs + al_specs,
        out_specs=[pl.BlockSpec((rb, GLA_VW), lambda ib, c: (rblk(ib, c), 0)), st_spec],
        out_shape=[jax.ShapeDtypeStruct((m_total, GLA_VW), BF16),
                   jax.ShapeDtypeStruct((depth, nseq, GLA_HEADS, GLA_DK, GLA_DV), F32)],
        scratch_shapes=[pltpu.VMEM((rb, GLA_VW), F32)],
        input_output_aliases=alias,
        compiler_params=_cp(("parallel", "arbitrary")),
        name="gla",
    )(*args, *al_arrays)


def _rwkv_body(*refs, names, bb, ch, t_valid, first, zero_init):
    r = dict(zip(names, refs))
    st_ref, carry_ref = r["st"], r["carry"]
    nr = bb * ch
    n2 = 2 * nr
    npair = RW_HEADS // 2
    c_id = pl.program_id(1)
    last_chunk = c_id == pl.num_programs(1) - 1
    lane = lax.broadcasted_iota(jnp.int32, (nr, LANES), 1)
    head_a = lane < RW_N
    r2 = lax.broadcasted_iota(jnp.int32, (n2, n2), 0)
    c2 = lax.broadcasted_iota(jnp.int32, (n2, n2), 1)
    same_blk = (r2 // ch) == (c2 // ch)
    strict = same_blk & (c2 < r2)
    incl = same_blk & (c2 <= r2)
    eye = (r2 == c2).astype(F32)
    blockdiag = (lax.broadcasted_iota(jnp.int32, (LANES, LANES), 0) < RW_N) == \
                (lax.broadcasted_iota(jnp.int32, (LANES, LANES), 1) < RW_N)
    zeros64 = jnp.zeros((RW_N, RW_N), F32)

    @pl.when(c_id == 0)
    def _():
        carry_ref[...] = r["sh"][:, 0, :]
        for s in range(bb):
            for p in range(npair):
                if zero_init:
                    st_ref[s, p] = jnp.zeros((LANES, LANES), F32)
                else:
                    top = jnp.concatenate([r["s0"][s, 2 * p], zeros64], axis=1)
                    bot = jnp.concatenate([zeros64, r["s0"][s, 2 * p + 1]], axis=1)
                    st_ref[s, p] = jnp.concatenate([top, bot], axis=0)

    rowpos = lax.broadcasted_iota(jnp.int32, (nr, RW_W), 0) % ch
    sections = (("pr", 0), ("pk", RW_W), ("pv", 2 * RW_W), ("ps", 3 * RW_W))

    def shifted(name, off):
        p = r[name][...]
        prev = pltpu.roll(p, 1, axis=0)
        parts = [jnp.broadcast_to(carry_ref[s:s + 1, off:off + RW_W], (ch, RW_W)) for s in range(bb)]
        first_rows = parts[0] if bb == 1 else jnp.concatenate(parts, axis=0)
        prev = jnp.where(rowpos == 0, first_rows, prev)
        return p, p + r["mu"][:, off:off + RW_W] * (prev - p)

    raw, feat = {}, {}
    for name, off in sections:
        raw[name], feat[name] = shifted(name, off)
    for name, off in sections:
        for s in range(bb):
            last = s * ch + ch - 1
            carry_ref[s:s + 1, off:off + RW_W] = raw[name][last:last + 1, :]

    @pl.when(last_chunk)
    def _():
        for name, off in sections:
            for s in range(bb):
                tv = s * ch + t_valid - 1
                r["sh_out"][s, :, off:off + RW_W] = raw[name][tv:tv + 1, :]

    f_r, f_k, f_v, f_s = feat["pr"], feat["pk"], feat["pv"], feat["ps"]
    xw = f_s[:, S_XW:S_XW + LANES]
    xa = f_s[:, S_XA:S_XA + LANES]
    xg = f_s[:, S_XG:S_XG + 2 * LANES]
    w_log = -_softplus(-(r["w0"][...] + _bdot(jnp.tanh(xw), r["w2"][...]))) - 0.5
    lw = -jnp.exp(w_log)
    a = _sigmoid(r["a0"][...] + _bdot(xa, r["a2"][...]))
    gate = _bdot(_sigmoid(xg), r["g2"][...])
    if first:
        vr = f_v
        r["vf_out"][...] = vr
    else:
        nu = _sigmoid(r["v0"][...] + _bdot(raw["ps"][:, S_V1:S_V1 + LANES], r["v2"][...]))
        vr = f_v + (r["vf"][...] - f_v) * nu
    kk = f_k * r["kkw"][...]
    kmod = f_k * (1.0 + (a - 1.0) * r["kaw"][...])
    if t_valid < ch:
        valid = rowpos < t_valid
        lw = jnp.where(valid, lw, 0.0)
        a = jnp.where(valid, a, 0.0)
        kmod = jnp.where(valid, kmod, 0.0)
    g_in = _cumsum_rows(lw, ch)
    g_ex = g_in - lw
    g_end = _last_row_of_each(g_in, bb, ch)
    e_in, e_ex, e_neg, e_tail = jnp.exp(g_in), jnp.exp(g_ex), jnp.exp(-g_in), jnp.exp(g_end - g_in)
    e_end = jnp.exp(g_end)

    def stack(x):
        return jnp.concatenate([jnp.where(head_a, x, 0.0), jnp.where(head_a, 0.0, x)], axis=0)

    def unstack(x):
        return x[:nr] + x[nr:]

    def pair_sum(x):
        sa = jnp.sum(jnp.where(head_a, x, 0.0), axis=-1, keepdims=True)
        sb = jnp.sum(jnp.where(head_a, 0.0, x), axis=-1, keepdims=True)
        return jnp.where(head_a, sa, sb)

    def seq_rows(s):
        return slice(s * ch, (s + 1) * ch)

    for p0 in range(0, npair, RW_PAIR_GROUP):
        group = range(p0, p0 + RW_PAIR_GROUP)
        q = {p: {} for p in group}
        for p in group:
            ls = slice(p * LANES, (p + 1) * LANES)
            d = q[p]
            kk_p = kk[:, ls]
            kap = kk_p / jnp.maximum(jnp.sqrt(pair_sum(kk_p * kk_p)), 1e-12)
            d["r"], d["k"], d["v"] = f_r[:, ls], kmod[:, ls], vr[:, ls]
            d["kap_t"] = kap * e_ex[:, ls]
            d["r_t"] = d["r"] * e_in[:, ls]
            ba = kap * a[:, ls]
            d["b_h"] = ba * e_tail[:, ls]
            d["k_h"] = d["k"] * e_tail[:, ls]
            lhs4 = jnp.concatenate([stack(d["kap_t"]), stack(d["r_t"])], axis=0)
            rhs4 = jnp.concatenate([stack(ba * e_neg[:, ls]), stack(d["k"] * e_neg[:, ls])], axis=0)
            m4 = _bdot_nt(lhs4, rhs4)
            d["n"] = jnp.where(strict, m4[:n2, :n2], 0.0)
            d["a_ak"] = jnp.where(strict, m4[:n2, n2:], 0.0)
            d["a_r"] = jnp.concatenate([jnp.where(incl, m4[n2:, :n2], 0.0), jnp.where(incl, m4[n2:, n2:], 0.0)], axis=1)
            d["dinv"] = eye
            d["v_st"] = stack(d["v"])
        for p in group:
            d = q[p]
            ks_parts, rs_parts = [], []
            for s in range(bb):
                both = _bdot_nt(jnp.concatenate([d["kap_t"][seq_rows(s)], d["r_t"][seq_rows(s)]], axis=0), st_ref[s, p])
                ks_parts.append(both[:ch])
                rs_parts.append(both[ch:])
            d["ks"] = ks_parts[0] if bb == 1 else jnp.concatenate(ks_parts, axis=0)
            d["rs"] = rs_parts[0] if bb == 1 else jnp.concatenate(rs_parts, axis=0)
            d["rhs"] = stack(d["ks"]) + _bdot(d["a_ak"], d["v_st"])
        m = 1
        while m < ch:
            low = same_blk & ((r2 // (2 * m)) == (c2 // (2 * m))) & (((r2 // m) % 2) == 1) & (((c2 // m) % 2) == 0)
            tmp = {p: _bdot(q[p]["dinv"], jnp.where(low, q[p]["n"], 0.0)) for p in group}
            for p in group:
                q[p]["dinv"] = q[p]["dinv"] - _bdot(tmp[p], q[p]["dinv"])
            m *= 2
        for p in group:
            d = q[p]
            d["u_st"] = -_bdot(d["dinv"], d["rhs"])
        for p in group:
            d = q[p]
            ls = slice(p * LANES, (p + 1) * LANES)
            y = unstack(stack(d["rs"]) + _bdot(d["a_r"], jnp.concatenate([d["u_st"], d["v_st"]], axis=0)))
            u_p = unstack(d["u_st"])
            for s in range(bb):
                rows = seq_rows(s)
                upd = _bdot_tn(jnp.concatenate([u_p[rows], d["v"][rows]], axis=0),
                               jnp.concatenate([d["b_h"][rows], d["k_h"][rows]], axis=0))
                st_ref[s, p] = st_ref[s, p] * e_end[s * ch:s * ch + 1, ls] + jnp.where(blockdiag, upd, 0.0)
            mean = pair_sum(y) * (1.0 / RW_N)
            yc = y - mean
            var = pair_sum(yc * yc) * (1.0 / RW_N)
            yn = yc * lax.rsqrt(var + RW_GN_EPS) * r["lnw"][:, ls] + r["lnb"][:, ls]
            bonus = pair_sum(d["r"] * d["k"] * r["rk"][:, ls]) * d["v"]
            r["o"][:, ls] = ((yn + bonus) * gate[:, ls]).astype(r["o"].dtype)

    @pl.when(last_chunk)
    def _():
        for s in range(bb):
            for p in range(npair):
                st = st_ref[s, p]
                r["s_out"][s, 2 * p] = st[:RW_N, :RW_N]
                r["s_out"][s, 2 * p + 1] = st[RW_N:, RW_N:]


def _rwkv(proj, vf, shift, s0, o_buf, st_buf, wts, l, *, depth, m_total, r0, nseq, t, ch, bb, t_valid):
    first = vf is None
    rb = bb * ch
    nc = t // ch
    assert rb == RW_SLAB and (bb == 1 or nc == 1) and r0 % rb == 0
    small_off = proj.shape[1] - SMALL_W

    def rblk(ib, c):
        return r0 // rb + ib * nc + c

    def sec(off):
        return pl.BlockSpec((rb, RW_W), lambda ib, c: (rblk(ib, c), off // RW_W))

    def vec(width, dl=0):
        return pl.BlockSpec((None, 1, width), lambda ib, c: (l - dl, 0, 0))

    def mat(rows, dl=0):
        return pl.BlockSpec((None, rows, RW_W), lambda ib, c: (l - dl, 0, 0))

    group_rows = pl.BlockSpec((rb, RW_W), lambda ib, c: (ib * nc + c, 0))
    st_spec = pl.BlockSpec((None, bb, RW_HEADS, RW_N, RW_N), lambda ib, c: (l, ib, 0, 0, 0))
    sh_spec = pl.BlockSpec((bb, 1, SHIFT_W), lambda ib, c: (ib, 0, 0))
    names = ["pr", "pk", "pv", "ps", "mu", "sh", "w0", "w2", "a0", "a2", "g2", "kkw", "kaw", "rk", "lnw", "lnb"]
    args = [proj, proj, proj, proj, wts["mu"], shift, wts["w0"], wts["w2"], wts["a0"], wts["a2"], wts["g2"],
            wts["kk"], wts["ka"], wts["rk"], wts["lnw"], wts["lnb"]]
    in_specs = [sec(C_RR), sec(C_RK), sec(C_RV), sec(small_off), vec(SHIFT_W), sh_spec,
                vec(RW_W), mat(LANES), vec(RW_W), mat(LANES), mat(2 * LANES)] + [vec(RW_W)] * 5
    if not first:
        names += ["vf", "v0", "v2"]
        args += [vf, wts["v0"], wts["v2"]]
        in_specs += [group_rows, vec(RW_W, 1), mat(LANES, 1)]
    if s0 is not None:
        names.append("s0")
        args.append(s0)
        in_specs.append(st_spec)
    al_arrays, al_specs, alias = _alias_inputs([o_buf, st_buf])
    alias = {len(args) + k: v for k, v in alias.items()}
    names += ["alias%d" % i for i in range(len(al_arrays))] + ["o", "s_out", "sh_out"]
    out_specs = [pl.BlockSpec((rb, RW_W), lambda ib, c: (rblk(ib, c), 0)), st_spec, sh_spec]
    out_shape = [jax.ShapeDtypeStruct((m_total, RW_W), BF16),
                 jax.ShapeDtypeStruct((depth, nseq, RW_HEADS, RW_N, RW_N), F32),
                 jax.ShapeDtypeStruct((nseq, 1, SHIFT_W), F32)]
    if first:
        names.append("vf_out")
        out_specs.append(group_rows)
        out_shape.append(jax.ShapeDtypeStruct((nseq * t, RW_W), F32))
    names += ["st", "carry"]
    res = pl.pallas_call(
        functools.partial(_rwkv_body, names=names, bb=bb, ch=ch, t_valid=t_valid, first=first,
                          zero_init=s0 is None),
        grid=(nseq // bb, nc),
        in_specs=in_specs + al_specs,
        out_specs=out_specs,
        out_shape=out_shape,
        scratch_shapes=[pltpu.VMEM((bb, RW_HEADS // 2, LANES, LANES), F32), pltpu.VMEM((bb, SHIFT_W), F32)],
        input_output_aliases=alias,
        compiler_params=_cp(("parallel", "arbitrary")),
        name="rwkv7",
    )(*args, *al_arrays)
    return res[0], res[1], res[2], (res[3] if first else None)


def _xattn_body(*refs, names, bb, tq):
    r = dict(zip(names, refs))

    def kv(ref, s, h):
        return ref[s, :, h * X_DH:(h + 1) * X_DH]

    units = [(s, h) for s in range(bb) for h in range(X_HEADS)]
    pr = {}
    for s, h in units:
        sc = _bdot_nt(r["q"][s * tq:(s + 1) * tq, h * X_DH:(h + 1) * X_DH], kv(r["k"], s, h)) * (X_DH ** -0.5)
        e = jnp.exp(sc - jnp.max(sc, axis=-1, keepdims=True))
        pr[s, h] = e / jnp.sum(e, axis=-1, keepdims=True)
    for s, h in units:
        r["of"][s * tq:(s + 1) * tq, h * X_DH:(h + 1) * X_DH] = _bdot(pr[s, h], kv(r["v"], s, h))
    r["o"][...] = r["of"][...].astype(r["o"].dtype)


def _xattn(proj, mem_k, mem_v, o_buf, l, *, m_total, r0, nseq, t, tq, bb):
    rb = bb * tq
    nq = t // tq
    assert (bb == 1 or nq == 1) and r0 % rb == 0
    kv_spec = pl.BlockSpec((None, bb, MEM_LEN, X_W), lambda ib, c: (l, ib, 0, 0))
    rows = lambda ib, c: r0 // rb + ib * nq + c
    al_arrays, al_specs, alias = _alias_inputs([o_buf])
    alias = {3 + k: v for k, v in alias.items()}
    names = ["q", "k", "v"] + ["alias%d" % i for i in range(len(al_arrays))] + ["o", "of"]
    return pl.pallas_call(
        functools.partial(_xattn_body, names=names, bb=bb, tq=tq),
        grid=(nseq // bb, nq),
        in_specs=[pl.BlockSpec((rb, X_W), lambda ib, c: (rows(ib, c), C_XQ // X_W)), kv_spec, kv_spec] + al_specs,
        out_specs=pl.BlockSpec((rb, X_W), lambda ib, c: (rows(ib, c), 0)),
        out_shape=jax.ShapeDtypeStruct((m_total, X_W), BF16),
        scratch_shapes=[pltpu.VMEM((rb, X_W), F32)],
        input_output_aliases=alias,
        compiler_params=_cp(("parallel", "arbitrary")),
        name="mem_xattn",
    )(proj, mem_k, mem_v, *al_arrays)


def _pad_last(a, width):
    return jnp.pad(a, [(0, 0)] * (a.ndim - 1) + [(0, width - a.shape[-1])])


def _pad_rows(a, rows):
    return jnp.pad(a, [(0, 0)] * (a.ndim - 2) + [(0, rows - a.shape[-2]), (0, 0)])


def _small_section(ga, xw, xa, xg, v1):
    return jnp.concatenate([_pad_last(ga, S_XW - S_GA), _pad_last(xw, S_XA - S_XW), _pad_last(xa, S_XG - S_XA),
                            _pad_last(xg, S_V1 - S_XG), _pad_last(v1, SMALL_W - S_V1)], axis=-1)


def _relayout_in_cols(a, v1):
    rw = GLA_COLS
    sm = rw + 3 * RW_W
    xq = rw + RW_COLS
    gt = xq + X_W
    small = _small_section(a[..., 3072:GLA_COLS], a[..., sm:sm + RW_DECAY_R],
                           a[..., sm + RW_DECAY_R:sm + RW_DECAY_R + RW_A_R],
                           a[..., sm + RW_DECAY_R + RW_A_R:xq], v1)
    return jnp.concatenate([a[..., 0:3072], a[..., rw:sm], a[..., xq:gt], a[..., gt:], small], axis=-1)


def _shift_layout(a):
    z16 = jnp.zeros(a.shape[:-1] + (GLA_RANK,), a.dtype)
    z32 = jnp.zeros(a.shape[:-1] + (RW_V_R,), a.dtype)
    o = 3 * RW_W
    small = _small_section(z16, a[..., o:o + RW_DECAY_R], a[..., o + RW_DECAY_R:o + RW_DECAY_R + RW_A_R],
                           a[..., o + RW_DECAY_R + RW_A_R:], z32)
    return jnp.concatenate([a[..., :o], small], axis=-1)


def _shift_unlayout(a):
    o = 3 * RW_W
    return jnp.concatenate([a[..., :o], a[..., o + S_XW:o + S_XW + RW_DECAY_R], a[..., o + S_XA:o + S_XA + RW_A_R],
                            a[..., o + S_XG:o + S_XG + RW_G_R]], axis=-1)


def kernel(x_prompt, x_sample, mem_prompt, state_gla, state_rwkv, state_rwkv_shift, cache_mem_k, cache_mem_v,
           g_norm1, w_in, gla_wa2, gla_ba2, gla_onorm, rw_mu, rw_w0, rw_w2, rw_a0, rw_a2, rw_g2, rw_kk, rw_ka,
           rw_rk, rw_lnx_w, rw_lnx_b, rw_v0, rw_v1, rw_v2, g_mem, w_mk, w_mv, w_branch, w_out, g_norm2,
           w_ff_gate, w_ff_up, w_ff_down, g_final):
    depth = w_in.shape[0]
    bp, tp, d = x_prompt.shape
    bs, ts, _ = x_sample.shape
    assert d % 1024 == 0 and ts <= SAMPLE_T_PAD and tp % RW_SLAB == 0 and bs % (RW_SLAB // SAMPLE_T_PAD) == 0
    mp, ms = bp * tp, bs * SAMPLE_T_PAD
    m = mp + ms
    f = w_ff_gate.shape[-1]

    v1_all = jnp.concatenate([jnp.zeros((1, d, RW_V_R), F32), rw_v1], axis=0)
    w_in_r = _relayout_in_cols(w_in, v1_all).astype(BF16)
    n_cols = w_in_r.shape[-1]
    r3 = lambda a: a.reshape(a.shape[0], 1, -1)
    rw_wts = dict(mu=r3(_shift_layout(rw_mu)), w0=r3(rw_w0), w2=_pad_rows(rw_w2, LANES), a0=r3(rw_a0),
                  a2=_pad_rows(rw_a2, LANES), g2=_pad_rows(rw_g2, 2 * LANES), kk=r3(rw_kk), ka=r3(rw_ka),
                  rk=r3(rw_rk), lnw=r3(rw_lnx_w), lnb=r3(rw_lnx_b), v0=r3(rw_v0), v2=_pad_rows(rw_v2, LANES))
    wa2p = _pad_rows(gla_wa2, LANES)
    ba2, onorm = r3(gla_ba2), r3(gla_onorm)
    g1, g2n, gm = r3(g_norm1), r3(g_norm2), r3(g_mem)

    xs_pad = jnp.pad(x_sample, ((0, 0), (0, SAMPLE_T_PAD - ts), (0, 0)))
    x = jnp.concatenate([x_prompt.reshape(mp, d), xs_pad.reshape(ms, d)], axis=0)

    tm = _pick(m, 1152, 128) if m % 128 == 0 else _pick(m, 1152, 16)
    tn_in = _pick(n_cols, 512, LANES)
    tn_d = _pick(d, 512, LANES)
    tn_f = _pick(f, 512, LANES)

    mem_rows = mem_prompt.reshape(bp * MEM_LEN, d)
    tmm = _pick(bp * MEM_LEN, 1024, SUBLANES)
    p_mem_k = jnp.stack([_norm_mm(mem_rows, gm, w_mk, l, tm=tmm, tn=tn_d, name="mem_k") for l in range(depth)])
    p_mem_v = jnp.stack([_norm_mm(mem_rows, gm, w_mv, l, tm=tmm, tn=tn_d, name="mem_v") for l in range(depth)])
    p_mem_k = p_mem_k.reshape(depth, bp, MEM_LEN, X_W)
    p_mem_v = p_mem_v.reshape(depth, bp, MEM_LEN, X_W)
    s_mem_k = cache_mem_k.reshape(depth, bs, MEM_LEN, X_W)
    s_mem_v = cache_mem_v.reshape(depth, bs, MEM_LEN, X_W)

    sh0 = jnp.zeros((bp, 1, SHIFT_W), F32)
    s_shift_in = _shift_layout(state_rwkv_shift).reshape(depth, bs, 1, SHIFT_W)

    bb_s = RW_SLAB // SAMPLE_T_PAD
    tq_p = _pick(tp, 512, SUBLANES)
    common = dict(depth=depth, m_total=m)
    prompt = dict(r0=0, nseq=bp, t=tp)
    sample = dict(r0=mp, nseq=bs, t=SAMPLE_T_PAD)
    vf_p = vf_s = None
    pg_all = sg_all = pr_all = sr_all = None
    p_shift, s_shift = [], []
    for l in range(depth):
        proj = _norm_mm(x, g1, w_in_r, l, tm=tm, tn=tn_in, name="in_proj")

        o_gla, pg_all = _gla(proj, wa2p, ba2, onorm, None, None, pg_all, l, **common, **prompt,
                             ch=GLA_CHUNK, bb=1, t_valid=GLA_CHUNK)
        o_gla, sg_all = _gla(proj, wa2p, ba2, onorm, state_gla, o_gla, sg_all, l, **common, **sample,
                             ch=SAMPLE_T_PAD, bb=bb_s, t_valid=ts)
        o_rw, pr_all, sh_p, vf_new_p = _rwkv(proj, vf_p, sh0, None, None, pr_all, rw_wts, l, **common, **prompt,
                                             ch=RW_SLAB, bb=1, t_valid=RW_SLAB)
        o_rw, sr_all, sh_s, vf_new_s = _rwkv(proj, vf_s, s_shift_in[l], state_rwkv, o_rw, sr_all, rw_wts, l,
                                             **common, **sample, ch=SAMPLE_T_PAD, bb=bb_s, t_valid=ts)
        if l == 0:
            vf_p, vf_s = vf_new_p, vf_new_s
        o_x = _xattn(proj, p_mem_k, p_mem_v, None, l, m_total=m, **prompt, tq=tq_p, bb=1)
        o_x = _xattn(proj, s_mem_k, s_mem_v, o_x, l, m_total=m, **sample, tq=SAMPLE_T_PAD, bb=bb_s)

        merged = _merge(o_gla, o_rw, o_x, proj, w_branch, l, tm=tm, tn=tn_d)
        x = _mm_res(merged, w_out, l, x, tm=tm, tn=tn_d, tk=d, name="out_proj")
        act = _ffn_up(x, g2n, w_ff_gate, w_ff_up, l, tm=tm, tn=tn_f)
        x = _mm_res(act, w_ff_down, l, x, tm=tm, tn=tn_d, tk=_pick(f, 2048, LANES), name="ffn_down")
        p_shift.append(_shift_unlayout(sh_p[:, 0]))
        s_shift.append(_shift_unlayout(sh_s[:, 0]))

    y = _final_norm(x, g_final.reshape(1, d))
    y_prompt = y[:mp].reshape(bp, tp, d)
    y_sample = y[mp:].reshape(bs, SAMPLE_T_PAD, d)[:, :ts]
    return (y_prompt, y_sample, pg_all, pr_all, jnp.stack(p_shift),
            p_mem_k.reshape(depth, bp, MEM_LEN, X_HEADS, X_DH), p_mem_v.reshape(depth, bp, MEM_LEN, X_HEADS, X_DH),
            sg_all, sr_all, jnp.stack(s_shift))
```

```python
import functools

import jax
import jax.numpy as jnp
from jax import lax
from jax.experimental import pallas as pl
from jax.experimental.pallas import tpu as pltpu

F32 = jnp.float32
BF16 = jnp.bfloat16

GLA_HEADS, GLA_DK, GLA_DV = 4, 128, 256
GLA_KW, GLA_VW, GLA_RANK = GLA_HEADS * GLA_DK, GLA_HEADS * GLA_DV, 16
GLA_GATE_NORM = 16.0
RW_HEADS, RW_N = 16, 64
RW_W = RW_HEADS * RW_N
RW_DECAY_R, RW_A_R, RW_V_R, RW_G_R = 64, 64, 32, 160
RW_GN_EPS = 64e-5
MEM_LEN, X_HEADS, X_DH = 256, 4, 256
X_W = X_HEADS * X_DH
N_BRANCH = 3
NORM_EPS = 1e-6
GLA_COLS = 2 * GLA_KW + 2 * GLA_VW + GLA_RANK
RW_COLS = 3 * RW_W + RW_DECAY_R + RW_A_R + RW_G_R

LANES = 128
SUBLANES = 8
VMEM_LIMIT_BYTES = 56 * 1024 * 1024

C_GQ, C_GK, C_GV, C_GG = 0, 512, 1024, 2048
C_RR, C_RK, C_RV, C_XQ, C_SMALL = 3072, 4096, 5120, 6144, 7168
SMALL_W = 1024
IN_TILE = 512
S_GA, S_XW, S_XA, S_XG, S_V1 = 0, 128, 256, 384, 640
SHIFT_W = 3 * RW_W + SMALL_W

SAMPLE_T_PAD = 8
RW_SLAB = 64
RW_PAIR_GROUP = 8
GLA_CHUNK = 64


def _cp(sem):
    return pltpu.CompilerParams(dimension_semantics=sem, vmem_limit_bytes=VMEM_LIMIT_BYTES)


def _pick(n, target, mult):
    best = None
    for d in range(mult, min(n, target) + 1, mult):
        if n % d == 0:
            best = d
    assert best is not None, (n, target, mult)
    return best


def _bdot(a, b):
    return jnp.dot(a.astype(BF16), b.astype(BF16), preferred_element_type=F32)


def _bdot_nt(a, b):
    return lax.dot_general(a.astype(BF16), b.astype(BF16), (((1,), (1,)), ((), ())), preferred_element_type=F32)


def _bdot_tn(a, b):
    return lax.dot_general(a.astype(BF16), b.astype(BF16), (((0,), (0,)), ((), ())), preferred_element_type=F32)


def _softplus(y):
    return jnp.maximum(y, 0.0) + jnp.log(1.0 + jnp.exp(-jnp.abs(y)))


def _sigmoid(y):
    return 1.0 / (1.0 + jnp.exp(-y))


def _cumsum_rows(x, block):
    pos = lax.broadcasted_iota(jnp.int32, x.shape, 0) % block
    d = 1
    while d < block:
        x = x + jnp.where(pos >= d, pltpu.roll(x, d, axis=0), 0.0)
        d *= 2
    return x


def _last_row_of_each(x, bb, ch):
    parts = [jnp.broadcast_to(x[s * ch + ch - 1:s * ch + ch, :], (ch, x.shape[1])) for s in range(bb)]
    return parts[0] if bb == 1 else jnp.concatenate(parts, axis=0)


def _alias_inputs(bufs):
    arrays, specs, alias = [], [], {}
    for out_idx, buf in enumerate(bufs):
        if buf is not None:
            alias[len(arrays)] = out_idx
            arrays.append(buf)
            specs.append(pl.BlockSpec(memory_space=pl.ANY))
    return arrays, specs, alias


def _rms_rows(x_ref, g_ref, h_ref, rows, eps):
    n = x_ref.shape[0] // rows

    def body(r, c):
        sl = pl.ds(pl.multiple_of(r * rows, rows), rows)
        x = x_ref[sl, :]
        ms = jnp.mean(x * x, axis=-1, keepdims=True)
        h_ref[sl, :] = (x * lax.rsqrt(ms + eps) * g_ref[...]).astype(h_ref.dtype)
        return c

    lax.fori_loop(0, n, body, 0)


def _norm_mm_body(x_ref, g_ref, w_ref, o_ref, h_ref, *, rows):
    @pl.when(pl.program_id(1) == 0)
    def _():
        _rms_rows(x_ref, g_ref, h_ref, rows, NORM_EPS)

    o_ref[...] = jnp.dot(h_ref[...], w_ref[...].astype(BF16), preferred_element_type=F32).astype(o_ref.dtype)


def _norm_mm(x, g3, w3, l, *, tm, tn, name):
    m, d = x.shape
    n = w3.shape[-1]
    return pl.pallas_call(
        functools.partial(_norm_mm_body, rows=_pick(tm, 128, SUBLANES)),
        grid=(m // tm, n // tn),
        in_specs=[pl.BlockSpec((tm, d), lambda i, j: (i, 0)),
                  pl.BlockSpec((None, 1, d), lambda i, j: (l, 0, 0)),
                  pl.BlockSpec((None, d, tn), lambda i, j: (l, 0, j))],
        out_specs=pl.BlockSpec((tm, tn), lambda i, j: (i, j)),
        out_shape=jax.ShapeDtypeStruct((m, n), F32),
        scratch_shapes=[pltpu.VMEM((tm, d), BF16)],
        compiler_params=_cp(("parallel", "arbitrary")),
        name=name,
    )(x, g3, w3)


def _pack_w_body(*refs, shift, nblk):
    o_ref = refs[-1]
    x = jnp.concatenate([b[...] for b in refs[:nblk]], axis=1)
    o_ref[...] = x[:, shift:shift + IN_TILE].astype(o_ref.dtype)


def _pack_w_in(w_in):
    depth, d, _ = w_in.shape
    tr = _pick(d, 1024, SUBLANES)
    per = IN_TILE // LANES
    xq0 = GLA_COLS + RW_COLS
    regions = [(0, 0, 3072), (GLA_COLS, 3072, 3 * RW_W), (xq0, 3072 + 3 * RW_W, X_W + N_BRANCH * d)]
    n_out = regions[-1][1] + regions[-1][2]
    out = None
    for src0, dst0, width in regions:
        q0, shift = divmod(src0, LANES)
        jt0 = dst0 // IN_TILE

        def src(k, q0=q0):
            return pl.BlockSpec((None, tr, LANES), lambda l, i, j: (l, i, q0 + per * j + k))

        al_arrays, al_specs, alias = _alias_inputs([out])
        out = pl.pallas_call(
            functools.partial(_pack_w_body, shift=shift, nblk=per + 1),
            grid=(depth, d // tr, width // IN_TILE),
            in_specs=[src(k) for k in range(per + 1)] + al_specs,
            out_specs=pl.BlockSpec((None, tr, IN_TILE), lambda l, i, j, jt0=jt0: (l, i, jt0 + j)),
            out_shape=jax.ShapeDtypeStruct((depth, d, n_out), BF16),
            input_output_aliases={per + 1 + k: v for k, v in alias.items()},
            compiler_params=_cp(("parallel", "parallel", "arbitrary")),
            name="pack_w_in",
        )(*([w_in] * (per + 1)), *al_arrays)
    return out


def _in_proj_body(x_ref, g_ref, wm_ref, ws_ref, o_ref, og_ref, h_ref, *, rows, n_main, n_small):
    j = pl.program_id(1)

    @pl.when(j == 0)
    def _():
        _rms_rows(x_ref, g_ref, h_ref, rows, NORM_EPS)

    @pl.when(j < n_main)
    def _():
        o_ref[...] = jnp.dot(h_ref[...], wm_ref[...], preferred_element_type=F32)

    @pl.when((j >= n_main) & (j < n_main + n_small))
    def _():
        o_ref[...] = jnp.dot(h_ref[...], ws_ref[...], preferred_element_type=F32)

    @pl.when(j >= n_main + n_small)
    def _():
        og_ref[...] = _sigmoid(jnp.dot(h_ref[...], wm_ref[...], preferred_element_type=F32)).astype(og_ref.dtype)


def _in_proj(x, g3, w_main, w_small, l, *, tm):
    m, d = x.shape
    tn = IN_TILE
    n_main, n_small = C_SMALL // tn, SMALL_W // tn
    n_gate = w_main.shape[-1] // tn - n_main
    return pl.pallas_call(
        functools.partial(_in_proj_body, rows=_pick(tm, 128, SUBLANES), n_main=n_main, n_small=n_small),
        grid=(m // tm, n_main + n_small + n_gate),
        in_specs=[pl.BlockSpec((tm, d), lambda i, j: (i, 0)),
                  pl.BlockSpec((None, 1, d), lambda i, j: (l, 0, 0)),
                  pl.BlockSpec((None, d, tn),
                               lambda i, j: (l, 0, jnp.where(j < n_main, j, jnp.maximum(j - n_small, n_main - 1)))),
                  pl.BlockSpec((None, d, tn), lambda i, j: (l, 0, jnp.clip(j - n_main, 0, n_small - 1)))],
        out_specs=[pl.BlockSpec((tm, tn), lambda i, j: (i, jnp.minimum(j, n_main + n_small - 1))),
                   pl.BlockSpec((tm, tn), lambda i, j: (i, jnp.maximum(j - n_main - n_small, 0)))],
        out_shape=[jax.ShapeDtypeStruct((m, (n_main + n_small) * tn), F32),
                   jax.ShapeDtypeStruct((m, n_gate * tn), BF16)],
        scratch_shapes=[pltpu.VMEM((tm, d), BF16)],
        compiler_params=_cp(("parallel", "arbitrary")),
        name="in_proj",
    )(x, g3, w_main, w_small)


def _merge_body(og_ref, or_ref, ox_ref, gg_ref, gr_ref, gx_ref, w_ref, o_ref):
    acc = gg_ref[...].astype(F32) * _bdot(og_ref[...], w_ref[0])
    acc += gr_ref[...].astype(F32) * _bdot(or_ref[...], w_ref[1])
    acc += gx_ref[...].astype(F32) * _bdot(ox_ref[...], w_ref[2])
    o_ref[...] = acc.astype(o_ref.dtype)


def _merge(o_gla, o_rw, o_x, gates, w_branch, l, *, tm, tn):
    m, bw = o_gla.shape
    d = w_branch.shape[-1]
    gate_blk = [b * d // tn for b in range(N_BRANCH)]
    o_spec = pl.BlockSpec((tm, bw), lambda i, j: (i, 0))
    return pl.pallas_call(
        _merge_body,
        grid=(m // tm, d // tn),
        in_specs=[o_spec, o_spec, o_spec]
        + [pl.BlockSpec((tm, tn), functools.partial(lambda i, j, off: (i, off + j), off=gate_blk[b]))
           for b in range(N_BRANCH)]
        + [pl.BlockSpec((None, N_BRANCH, bw, tn), lambda i, j: (l, 0, 0, j))],
        out_specs=pl.BlockSpec((tm, tn), lambda i, j: (i, j)),
        out_shape=jax.ShapeDtypeStruct((m, d), BF16),
        compiler_params=_cp(("parallel", "arbitrary")),
        name="branch_merge",
    )(o_gla, o_rw, o_x, gates, gates, gates, w_branch)


def _mm_res_body(a_ref, w_ref, r_ref, o_ref, acc_ref):
    k = pl.program_id(2)

    @pl.when(k == 0)
    def _():
        acc_ref[...] = r_ref[...]

    acc_ref[...] += _bdot(a_ref[...], w_ref[...])

    @pl.when(k == pl.num_programs(2) - 1)
    def _():
        o_ref[...] = acc_ref[...]


def _mm_res(a, w3, l, res, *, tm, tn, tk, name):
    m, kd = a.shape
    n = w3.shape[-1]
    return pl.pallas_call(
        _mm_res_body,
        grid=(m // tm, n // tn, kd // tk),
        in_specs=[pl.BlockSpec((tm, tk), lambda i, j, k: (i, k)),
                  pl.BlockSpec((None, tk, tn), lambda i, j, k: (l, k, j)),
                  pl.BlockSpec((tm, tn), lambda i, j, k: (i, j))],
        out_specs=pl.BlockSpec((tm, tn), lambda i, j, k: (i, j)),
        out_shape=jax.ShapeDtypeStruct((m, n), F32),
        scratch_shapes=[pltpu.VMEM((tm, tn), F32)],
        compiler_params=_cp(("parallel", "arbitrary", "arbitrary")),
        name=name,
    )(a, w3, res)


def _ffn_up_body(x_ref, g_ref, wg_ref, wu_ref, o_ref, h_ref, *, rows):
    @pl.when(pl.program_id(1) == 0)
    def _():
        _rms_rows(x_ref, g_ref, h_ref, rows, NORM_EPS)

    h = h_ref[...]
    a = jnp.dot(h, wg_ref[...].astype(BF16), preferred_element_type=F32)
    u = jnp.dot(h, wu_ref[...].astype(BF16), preferred_element_type=F32)
    o_ref[...] = (a * _sigmoid(a) * u).astype(o_ref.dtype)


def _ffn_up(x, g3, wg, wu, l, *, tm, tn):
    m, d = x.shape
    f = wg.shape[-1]
    w_spec = pl.BlockSpec((None, d, tn), lambda i, j: (l, 0, j))
    return pl.pallas_call(
        functools.partial(_ffn_up_body, rows=_pick(tm, 128, SUBLANES)),
        grid=(m // tm, f // tn),
        in_specs=[pl.BlockSpec((tm, d), lambda i, j: (i, 0)),
                  pl.BlockSpec((None, 1, d), lambda i, j: (l, 0, 0)),
                  w_spec, w_spec],
        out_specs=pl.BlockSpec((tm, tn), lambda i, j: (i, j)),
        out_shape=jax.ShapeDtypeStruct((m, f), BF16),
        scratch_shapes=[pltpu.VMEM((tm, d), BF16)],
        compiler_params=_cp(("parallel", "arbitrary")),
        name="ffn_up",
    )(x, g3, wg, wu)


def _final_norm_body(x_ref, g_ref, o_ref):
    x = x_ref[...]
    ms = jnp.mean(x * x, axis=-1, keepdims=True)
    o_ref[...] = x * lax.rsqrt(ms + NORM_EPS) * g_ref[...]


def _final_norm(x, g2):
    m, d = x.shape
    tm = _pick(m, 256, SUBLANES)
    return pl.pallas_call(
        _final_norm_body,
        grid=(m // tm,),
        in_specs=[pl.BlockSpec((tm, d), lambda i: (i, 0)), pl.BlockSpec((1, d), lambda i: (0, 0))],
        out_specs=pl.BlockSpec((tm, d), lambda i: (i, 0)),
        out_shape=jax.ShapeDtypeStruct((m, d), F32),
        compiler_params=_cp(("parallel",)),
        name="final_norm",
    )(x, g2)


def _gla_body(*refs, names, bb, ch, t_valid, zero_init):
    r = dict(zip(names, refs))
    nr = bb * ch
    s_ref = r["s_out"]

    @pl.when(pl.program_id(1) == 0)
    def _():
        if zero_init:
            s_ref[...] = jnp.zeros(s_ref.shape, F32)
        else:
            s_ref[...] = r["s0"][...]

    row = lax.broadcasted_iota(jnp.int32, (ch, ch), 0)
    col = lax.broadcasted_iota(jnp.int32, (ch, ch), 1)
    causal = col <= row
    z = _bdot(r["ga"][...], r["wa2"][...]) + r["ba2"][...]
    la = -_softplus(-z) * (1.0 / GLA_GATE_NORM)
    kk = r["k"][...]
    if t_valid < ch:
        valid = (lax.broadcasted_iota(jnp.int32, la.shape, 0) % ch) < t_valid
        la = jnp.where(valid, la, 0.0)
        kk = jnp.where(valid, kk, 0.0)
    bc = _cumsum_rows(la, ch)
    b_end = _last_row_of_each(bc, bb, ch)
    qe = r["q"][...] * (GLA_DK ** -0.5) * jnp.exp(bc)
    ke = kk * jnp.exp(-bc)
    kl = kk * jnp.exp(b_end - bc)
    dec_rows = jnp.exp(b_end)

    units = [(s, h) for s in range(bb) for h in range(GLA_HEADS)]

    def sl(s, h):
        return (slice(s * ch, (s + 1) * ch), slice(h * GLA_DK, (h + 1) * GLA_DK),
                slice(h * GLA_DV, (h + 1) * GLA_DV))

    att, o_int, upd = {}, {}, {}
    for s, h in units:
        rows, ks, vs = sl(s, h)
        att[s, h] = jnp.where(causal, _bdot_nt(qe[rows, ks], ke[rows, ks]), 0.0)
        o_int[s, h] = _bdot(qe[rows, ks], s_ref[s, h])
    for s, h in units:
        rows, ks, vs = sl(s, h)
        vh = r["v"][rows, vs]
        o = o_int[s, h] + _bdot(att[s, h], vh)
        upd[s, h] = _bdot_tn(kl[rows, ks], vh)
        ms = jnp.mean(o * o, axis=-1, keepdims=True)
        gh = r["g"][rows, vs]
        r["of"][rows, vs] = o * lax.rsqrt(ms + NORM_EPS) * r["on"][...] * (gh * _sigmoid(gh))
    for s, h in units:
        rows, ks, vs = sl(s, h)
        dec_col = jnp.broadcast_to(dec_rows[s * ch:s * ch + 1, ks], (GLA_DK, GLA_DK)).T
        dec = jnp.concatenate([dec_col] * (GLA_DV // GLA_DK), axis=1)
        s_ref[s, h] = s_ref[s, h] * dec + upd[s, h]
    r["o"][...] = r["of"][...].astype(r["o"].dtype)
    del nr


def _gla(proj, wa2p, ba2, onorm, s0, o_buf, st_buf, l, *, depth, m_total, r0, nseq, t, ch, bb, t_valid):
    rb = bb * ch
    nc = t // ch
    assert (bb == 1 or nc == 1) and r0 % rb == 0

    def rblk(ib, c):
        return r0 // rb + ib * nc + c

    def sec(width, off):
        return pl.BlockSpec((rb, width), lambda ib, c: (rblk(ib, c), off // width))

    st_spec = pl.BlockSpec((None, bb, GLA_HEADS, GLA_DK, GLA_DV), lambda ib, c: (l, ib, 0, 0, 0))
    names = ["q", "k", "v", "g", "ga", "wa2", "ba2", "on"]
    args = [proj, proj, proj, proj, proj, wa2p, ba2, onorm]
    in_specs = [sec(GLA_KW, C_GQ), sec(GLA_KW, C_GK), sec(GLA_VW, C_GV), sec(GLA_VW, C_GG),
                sec(LANES, proj.shape[1] - SMALL_W + S_GA),
                pl.BlockSpec((None, LANES, GLA_KW), lambda ib, c: (l, 0, 0)),
                pl.BlockSpec((None, 1, GLA_KW), lambda ib, c: (l, 0, 0)),
                pl.BlockSpec((None, 1, GLA_DV), lambda ib, c: (l, 0, 0))]
    if s0 is not None:
        names.append("s0")
        args.append(s0)
        in_specs.append(st_spec)
    al_arrays, al_specs, alias = _alias_inputs([o_buf, st_buf])
    alias = {len(args) + k: v for k, v in alias.items()}
    names += ["alias%d" % i for i in range(len(al_arrays))] + ["o", "s_out", "of"]
    return pl.pallas_call(
        functools.partial(_gla_body, names=names, bb=bb, ch=ch, t_valid=t_valid, zero_init=s0 is None),
        grid=(nseq // bb, nc),
        in_specs=in_specs + al_specs,
        out_specs=[pl.BlockSpec((rb, GLA_VW), lambda ib, c: (rblk(ib, c), 0)), st_spec],
        out_shape=[jax.ShapeDtypeStruct((m_total, GLA_VW), BF16),
                   jax.ShapeDtypeStruct((depth, nseq, GLA_HEADS, GLA_DK, GLA_DV), F32)],
        scratch_shapes=[pltpu.VMEM((rb, GLA_VW), F32)],
        input_output_aliases=alias,
        compiler_params=_cp(("parallel", "arbitrary")),
        name="gla",
    )(*args, *al_arrays)


def _rwkv_body(*refs, names, bb, ch, t_valid, first, zero_init):
    r = dict(zip(names, refs))
    st_ref, carry_ref = r["st"], r["carry"]
    nr = bb * ch
    n2 = 2 * nr
    npair = RW_HEADS // 2
    c_id = pl.program_id(1)
    last_chunk = c_id == pl.num_programs(1) - 1
    lane = lax.broadcasted_iota(jnp.int32, (nr, LANES), 1)
    head_a = lane < RW_N
    r2 = lax.broadcasted_iota(jnp.int32, (n2, n2), 0)
    c2 = lax.broadcasted_iota(jnp.int32, (n2, n2), 1)
    same_blk = (r2 // ch) == (c2 // ch)
    strict = same_blk & (c2 < r2)
    incl = same_blk & (c2 <= r2)
    eye = (r2 == c2).astype(F32)
    blockdiag = (lax.broadcasted_iota(jnp.int32, (LANES, LANES), 0) < RW_N) == \
                (lax.broadcasted_iota(jnp.int32, (LANES, LANES), 1) < RW_N)
    zeros64 = jnp.zeros((RW_N, RW_N), F32)

    @pl.when(c_id == 0)
    def _():
        carry_ref[...] = r["sh"][:, 0, :]
        for s in range(bb):
            for p in range(npair):
                if zero_init:
                    st_ref[s, p] = jnp.zeros((LANES, LANES), F32)
                else:
                    top = jnp.concatenate([r["s0"][s, 2 * p], zeros64], axis=1)
                    bot = jnp.concatenate([zeros64, r["s0"][s, 2 * p + 1]], axis=1)
                    st_ref[s, p] = jnp.concatenate([top, bot], axis=0)

    rowpos = lax.broadcasted_iota(jnp.int32, (nr, RW_W), 0) % ch
    sections = (("pr", 0), ("pk", RW_W), ("pv", 2 * RW_W), ("ps", 3 * RW_W))

    def shifted(name, off):
        p = r[name][...]
        prev = pltpu.roll(p, 1, axis=0)
        parts = [jnp.broadcast_to(carry_ref[s:s + 1, off:off + RW_W], (ch, RW_W)) for s in range(bb)]
        first_rows = parts[0] if bb == 1 else jnp.concatenate(parts, axis=0)
        prev = jnp.where(rowpos == 0, first_rows, prev)
        return p, p + r["mu"][:, off:off + RW_W] * (prev - p)

    raw, feat = {}, {}
    for name, off in sections:
        raw[name], feat[name] = shifted(name, off)
    for name, off in sections:
        for s in range(bb):
            last = s * ch + ch - 1
            carry_ref[s:s + 1, off:off + RW_W] = raw[name][last:last + 1, :]

    @pl.when(last_chunk)
    def _():
        for name, off in sections:
            for s in range(bb):
                tv = s * ch + t_valid - 1
                r["sh_out"][s, :, off:off + RW_W] = raw[name][tv:tv + 1, :]

    f_r, f_k, f_v, f_s = feat["pr"], feat["pk"], feat["pv"], feat["ps"]
    xw = f_s[:, S_XW:S_XW + LANES]
    xa = f_s[:, S_XA:S_XA + LANES]
    xg = f_s[:, S_XG:S_XG + 2 * LANES]
    w_log = -_softplus(-(r["w0"][...] + _bdot(jnp.tanh(xw), r["w2"][...]))) - 0.5
    lw = -jnp.exp(w_log)
    a = _sigmoid(r["a0"][...] + _bdot(xa, r["a2"][...]))
    gate = _bdot(_sigmoid(xg), r["g2"][...])
    if first:
        vr = f_v
        r["vf_out"][...] = vr
    else:
        nu = _sigmoid(r["v0"][...] + _bdot(raw["ps"][:, S_V1:S_V1 + LANES], r["v2"][...]))
        vr = f_v + (r["vf"][...] - f_v) * nu
    kk = f_k * r["kkw"][...]
    kmod = f_k * (1.0 + (a - 1.0) * r["kaw"][...])
    if t_valid < ch:
        valid = rowpos < t_valid
        lw = jnp.where(valid, lw, 0.0)
        a = jnp.where(valid, a, 0.0)
        kmod = jnp.where(valid, kmod, 0.0)
    g_in = _cumsum_rows(lw, ch)
    g_ex = g_in - lw
    g_end = _last_row_of_each(g_in, bb, ch)
    e_in, e_ex, e_neg, e_tail = jnp.exp(g_in), jnp.exp(g_ex), jnp.exp(-g_in), jnp.exp(g_end - g_in)
    e_end = jnp.exp(g_end)

    def stack(x):
        return jnp.concatenate([jnp.where(head_a, x, 0.0), jnp.where(head_a, 0.0, x)], axis=0)

    def unstack(x):
        return x[:nr] + x[nr:]

    def pair_sum(x):
        sa = jnp.sum(jnp.where(head_a, x, 0.0), axis=-1, keepdims=True)
        sb = jnp.sum(jnp.where(head_a, 0.0, x), axis=-1, keepdims=True)
        return jnp.where(head_a, sa, sb)

    def seq_rows(s):
        return slice(s * ch, (s + 1) * ch)

    for p0 in range(0, npair, RW_PAIR_GROUP):
        group = range(p0, p0 + RW_PAIR_GROUP)
        q = {p: {} for p in group}
        for p in group:
            ls = slice(p * LANES, (p + 1) * LANES)
            d = q[p]
            kk_p = kk[:, ls]
            kap = kk_p / jnp.maximum(jnp.sqrt(pair_sum(kk_p * kk_p)), 1e-12)
            d["r"], d["k"], d["v"] = f_r[:, ls], kmod[:, ls], vr[:, ls]
            d["kap_t"] = kap * e_ex[:, ls]
            d["r_t"] = d["r"] * e_in[:, ls]
            ba = kap * a[:, ls]
            d["b_h"] = ba * e_tail[:, ls]
            d["k_h"] = d["k"] * e_tail[:, ls]
            lhs4 = jnp.concatenate([stack(d["kap_t"]), stack(d["r_t"])], axis=0)
            rhs4 = jnp.concatenate([stack(ba * e_neg[:, ls]), stack(d["k"] * e_neg[:, ls])], axis=0)
            m4 = _bdot_nt(lhs4, rhs4)
            d["n"] = jnp.where(strict, m4[:n2, :n2], 0.0)
            d["a_ak"] = jnp.where(strict, m4[:n2, n2:], 0.0)
            d["a_r"] = jnp.concatenate([jnp.where(incl, m4[n2:, :n2], 0.0), jnp.where(incl, m4[n2:, n2:], 0.0)], axis=1)
            d["dinv"] = eye
            d["v_st"] = stack(d["v"])
        for p in group:
            d = q[p]
            ks_parts, rs_parts = [], []
            for s in range(bb):
                both = _bdot_nt(jnp.concatenate([d["kap_t"][seq_rows(s)], d["r_t"][seq_rows(s)]], axis=0), st_ref[s, p])
                ks_parts.append(both[:ch])
                rs_parts.append(both[ch:])
            d["ks"] = ks_parts[0] if bb == 1 else jnp.concatenate(ks_parts, axis=0)
            d["rs"] = rs_parts[0] if bb == 1 else jnp.concatenate(rs_parts, axis=0)
            d["rhs"] = stack(d["ks"]) + _bdot(d["a_ak"], d["v_st"])
        m = 1
        while m < ch:
            low = same_blk & ((r2 // (2 * m)) == (c2 // (2 * m))) & (((r2 // m) % 2) == 1) & (((c2 // m) % 2) == 0)
            tmp = {p: _bdot(q[p]["dinv"], jnp.where(low, q[p]["n"], 0.0)) for p in group}
            for p in group:
                q[p]["dinv"] = q[p]["dinv"] - _bdot(tmp[p], q[p]["dinv"])
            m *= 2
        for p in group:
            d = q[p]
            d["u_st"] = -_bdot(d["dinv"], d["rhs"])
        for p in group:
            d = q[p]
            ls = slice(p * LANES, (p + 1) * LANES)
            y = unstack(stack(d["rs"]) + _bdot(d["a_r"], jnp.concatenate([d["u_st"], d["v_st"]], axis=0)))
            u_p = unstack(d["u_st"])
            for s in range(bb):
                rows = seq_rows(s)
                upd = _bdot_tn(jnp.concatenate([u_p[rows], d["v"][rows]], axis=0),
                               jnp.concatenate([d["b_h"][rows], d["k_h"][rows]], axis=0))
                st_ref[s, p] = st_ref[s, p] * e_end[s * ch:s * ch + 1, ls] + jnp.where(blockdiag, upd, 0.0)
            mean = pair_sum(y) * (1.0 / RW_N)
            yc = y - mean
            var = pair_sum(yc * yc) * (1.0 / RW_N)
            yn = yc * lax.rsqrt(var + RW_GN_EPS) * r["lnw"][:, ls] + r["lnb"][:, ls]
            bonus = pair_sum(d["r"] * d["k"] * r["rk"][:, ls]) * d["v"]
            r["o"][:, ls] = ((yn + bonus) * gate[:, ls]).astype(r["o"].dtype)

    @pl.when(last_chunk)
    def _():
        for s in range(bb):
            for p in range(npair):
                st = st_ref[s, p]
                r["s_out"][s, 2 * p] = st[:RW_N, :RW_N]
                r["s_out"][s, 2 * p + 1] = st[RW_N:, RW_N:]


def _rwkv(proj, vf, shift, s0, o_buf, st_buf, wts, l, *, depth, m_total, r0, nseq, t, ch, bb, t_valid):
    first = vf is None
    rb = bb * ch
    nc = t // ch
    assert rb == RW_SLAB and (bb == 1 or nc == 1) and r0 % rb == 0
    small_off = proj.shape[1] - SMALL_W

    def rblk(ib, c):
        return r0 // rb + ib * nc + c

    def sec(off):
        return pl.BlockSpec((rb, RW_W), lambda ib, c: (rblk(ib, c), off // RW_W))

    def vec(width, dl=0):
        return pl.BlockSpec((None, 1, width), lambda ib, c: (l - dl, 0, 0))

    def mat(rows, dl=0):
        return pl.BlockSpec((None, rows, RW_W), lambda ib, c: (l - dl, 0, 0))

    group_rows = pl.BlockSpec((rb, RW_W), lambda ib, c: (ib * nc + c, 0))
    st_spec = pl.BlockSpec((None, bb, RW_HEADS, RW_N, RW_N), lambda ib, c: (l, ib, 0, 0, 0))
    sh_spec = pl.BlockSpec((bb, 1, SHIFT_W), lambda ib, c: (ib, 0, 0))
    names = ["pr", "pk", "pv", "ps", "mu", "sh", "w0", "w2", "a0", "a2", "g2", "kkw", "kaw", "rk", "lnw", "lnb"]
    args = [proj, proj, proj, proj, wts["mu"], shift, wts["w0"], wts["w2"], wts["a0"], wts["a2"], wts["g2"],
            wts["kk"], wts["ka"], wts["rk"], wts["lnw"], wts["lnb"]]
    in_specs = [sec(C_RR), sec(C_RK), sec(C_RV), sec(small_off), vec(SHIFT_W), sh_spec,
                vec(RW_W), mat(LANES), vec(RW_W), mat(LANES), mat(2 * LANES)] + [vec(RW_W)] * 5
    if not first:
        names += ["vf", "v0", "v2"]
        args += [vf, wts["v0"], wts["v2"]]
        in_specs += [group_rows, vec(RW_W, 1), mat(LANES, 1)]
    if s0 is not None:
        names.append("s0")
        args.append(s0)
        in_specs.append(st_spec)
    al_arrays, al_specs, alias = _alias_inputs([o_buf, st_buf])
    alias = {len(args) + k: v for k, v in alias.items()}
    names += ["alias%d" % i for i in range(len(al_arrays))] + ["o", "s_out", "sh_out"]
    out_specs = [pl.BlockSpec((rb, RW_W), lambda ib, c: (rblk(ib, c), 0)), st_spec, sh_spec]
    out_shape = [jax.ShapeDtypeStruct((m_total, RW_W), BF16),
                 jax.ShapeDtypeStruct((depth, nseq, RW_HEADS, RW_N, RW_N), F32),
                 jax.ShapeDtypeStruct((nseq, 1, SHIFT_W), F32)]
    if first:
        names.append("vf_out")
        out_specs.append(group_rows)
        out_shape.append(jax.ShapeDtypeStruct((nseq * t, RW_W), F32))
    names += ["st", "carry"]
    res = pl.pallas_call(
        functools.partial(_rwkv_body, names=names, bb=bb, ch=ch, t_valid=t_valid, first=first,
                          zero_init=s0 is None),
        grid=(nseq // bb, nc),
        in_specs=in_specs + al_specs,
        out_specs=out_specs,
        out_shape=out_shape,
        scratch_shapes=[pltpu.VMEM((bb, RW_HEADS // 2, LANES, LANES), F32), pltpu.VMEM((bb, SHIFT_W), F32)],
        input_output_aliases=alias,
        compiler_params=_cp(("parallel", "arbitrary")),
        name="rwkv7",
    )(*args, *al_arrays)
    return res[0], res[1], res[2], (res[3] if first else None)


def _xattn_body(*refs, names, bb, tq):
    r = dict(zip(names, refs))

    def kv(ref, s, h):
        return ref[s, :, h * X_DH:(h + 1) * X_DH]

    units = [(s, h) for s in range(bb) for h in range(X_HEADS)]
    pr = {}
    for s, h in units:
        sc = _bdot_nt(r["q"][s * tq:(s + 1) * tq, h * X_DH:(h + 1) * X_DH], kv(r["k"], s, h)) * (X_DH ** -0.5)
        e = jnp.exp(sc - jnp.max(sc, axis=-1, keepdims=True))
        pr[s, h] = e / jnp.sum(e, axis=-1, keepdims=True)
    for s, h in units:
        r["of"][s * tq:(s + 1) * tq, h * X_DH:(h + 1) * X_DH] = _bdot(pr[s, h], kv(r["v"], s, h))
    r["o"][...] = r["of"][...].astype(r["o"].dtype)


def _xattn(proj, mem_k, mem_v, o_buf, l, *, m_total, r0, nseq, t, tq, bb):
    rb = bb * tq
    nq = t // tq
    assert (bb == 1 or nq == 1) and r0 % rb == 0
    kv_spec = pl.BlockSpec((None, bb, MEM_LEN, X_W), lambda ib, c: (l, ib, 0, 0))
    rows = lambda ib, c: r0 // rb + ib * nq + c
    al_arrays, al_specs, alias = _alias_inputs([o_buf])
    alias = {3 + k: v for k, v in alias.items()}
    names = ["q", "k", "v"] + ["alias%d" % i for i in range(len(al_arrays))] + ["o", "of"]
    return pl.pallas_call(
        functools.partial(_xattn_body, names=names, bb=bb, tq=tq),
        grid=(nseq // bb, nq),
        in_specs=[pl.BlockSpec((rb, X_W), lambda ib, c: (rows(ib, c), C_XQ // X_W)), kv_spec, kv_spec] + al_specs,
        out_specs=pl.BlockSpec((rb, X_W), lambda ib, c: (rows(ib, c), 0)),
        out_shape=jax.ShapeDtypeStruct((m_total, X_W), BF16),
        scratch_shapes=[pltpu.VMEM((rb, X_W), F32)],
        input_output_aliases=alias,
        compiler_params=_cp(("parallel", "arbitrary")),
        name="mem_xattn",
    )(proj, mem_k, mem_v, *al_arrays)


def _xattn_cache_body(q_ref, k_hbm, v_hbm, alias_ref, o_ref, of_ref, kbuf, vbuf, sem, *, l, bb, tq):
    del alias_ref
    i = pl.program_id(0)
    n = pl.num_programs(0)

    def copies(step, slot):
        out = []
        for h in range(X_HEADS):
            for t, (src, dst) in enumerate(((k_hbm, kbuf), (v_hbm, vbuf))):
                out.append(pltpu.make_async_copy(src.at[l, pl.ds(step * bb, bb), :, h, :], dst.at[slot, h],
                                                 sem.at[slot, t, h]))
        return out

    @pl.when(i == 0)
    def _():
        for c in copies(0, 0):
            c.start()

    @pl.when(i + 1 < n)
    def _():
        for c in copies(i + 1, (i + 1) % 2):
            c.start()

    slot = i % 2
    for c in copies(i, slot):
        c.wait()

    units = [(s, h) for s in range(bb) for h in range(X_HEADS)]
    pr = {}
    for s, h in units:
        sc = _bdot_nt(q_ref[s * tq:(s + 1) * tq, h * X_DH:(h + 1) * X_DH], kbuf[slot, h, s]) * (X_DH ** -0.5)
        e = jnp.exp(sc - jnp.max(sc, axis=-1, keepdims=True))
        pr[s, h] = e / jnp.sum(e, axis=-1, keepdims=True)
    for s, h in units:
        of_ref[s * tq:(s + 1) * tq, h * X_DH:(h + 1) * X_DH] = _bdot(pr[s, h], vbuf[slot, h, s])
    o_ref[...] = of_ref[...].astype(o_ref.dtype)


def _xattn_cache(proj, cache_k, cache_v, o_buf, l, *, r0, nseq, tq, bb):
    rb = bb * tq
    assert r0 % rb == 0 and nseq % bb == 0
    rows = lambda ib: (r0 // rb + ib, 0)
    any_spec = pl.BlockSpec(memory_space=pl.ANY)
    buf = pltpu.VMEM((2, X_HEADS, bb, MEM_LEN, X_DH), F32)
    return pl.pallas_call(
        functools.partial(_xattn_cache_body, l=l, bb=bb, tq=tq),
        grid=(nseq // bb,),
        in_specs=[pl.BlockSpec((rb, X_W), lambda ib: (r0 // rb + ib, C_XQ // X_W)), any_spec, any_spec, any_spec],
        out_specs=pl.BlockSpec((rb, X_W), rows),
        out_shape=jax.ShapeDtypeStruct(o_buf.shape, o_buf.dtype),
        scratch_shapes=[pltpu.VMEM((rb, X_W), F32), buf, buf, pltpu.SemaphoreType.DMA((2, 2, X_HEADS))],
        input_output_aliases={3: 0},
        compiler_params=_cp(("arbitrary",)),
        name="cache_xattn",
    )(proj, cache_k, cache_v, o_buf)


def _pad_last(a, width):
    return jnp.pad(a, [(0, 0)] * (a.ndim - 1) + [(0, width - a.shape[-1])])


def _pad_rows(a, rows):
    return jnp.pad(a, [(0, 0)] * (a.ndim - 2) + [(0, rows - a.shape[-2]), (0, 0)])


def _small_section(ga, xw, xa, xg, v1):
    return jnp.concatenate([_pad_last(ga, S_XW - S_GA), _pad_last(xw, S_XA - S_XW), _pad_last(xa, S_XG - S_XA),
                            _pad_last(xg, S_V1 - S_XG), _pad_last(v1, SMALL_W - S_V1)], axis=-1)


def _small_weight(w_in, v1):
    sm = GLA_COLS + 3 * RW_W
    xq = GLA_COLS + RW_COLS
    return _small_section(w_in[..., 3072:GLA_COLS], w_in[..., sm:sm + RW_DECAY_R],
                          w_in[..., sm + RW_DECAY_R:sm + RW_DECAY_R + RW_A_R],
                          w_in[..., sm + RW_DECAY_R + RW_A_R:xq], v1)


def _shift_layout(a):
    z16 = jnp.zeros(a.shape[:-1] + (GLA_RANK,), a.dtype)
    z32 = jnp.zeros(a.shape[:-1] + (RW_V_R,), a.dtype)
    o = 3 * RW_W
    small = _small_section(z16, a[..., o:o + RW_DECAY_R], a[..., o + RW_DECAY_R:o + RW_DECAY_R + RW_A_R],
                           a[..., o + RW_DECAY_R + RW_A_R:], z32)
    return jnp.concatenate([a[..., :o], small], axis=-1)


def _shift_unlayout(a):
    o = 3 * RW_W
    return jnp.concatenate([a[..., :o], a[..., o + S_XW:o + S_XW + RW_DECAY_R], a[..., o + S_XA:o + S_XA + RW_A_R],
                            a[..., o + S_XG:o + S_XG + RW_G_R]], axis=-1)


def kernel(x_prompt, x_sample, mem_prompt, state_gla, state_rwkv, state_rwkv_shift, cache_mem_k, cache_mem_v,
           g_norm1, w_in, gla_wa2, gla_ba2, gla_onorm, rw_mu, rw_w0, rw_w2, rw_a0, rw_a2, rw_g2, rw_kk, rw_ka,
           rw_rk, rw_lnx_w, rw_lnx_b, rw_v0, rw_v1, rw_v2, g_mem, w_mk, w_mv, w_branch, w_out, g_norm2,
           w_ff_gate, w_ff_up, w_ff_down, g_final):
    depth = w_in.shape[0]
    bp, tp, d = x_prompt.shape
    bs, ts, _ = x_sample.shape
    assert d % 1024 == 0 and ts <= SAMPLE_T_PAD and tp % RW_SLAB == 0 and bs % (RW_SLAB // SAMPLE_T_PAD) == 0
    mp, ms = bp * tp, bs * SAMPLE_T_PAD
    m = mp + ms
    f = w_ff_gate.shape[-1]

    v1_all = jnp.concatenate([jnp.zeros((1, d, RW_V_R), F32), rw_v1], axis=0)
    w_main = _pack_w_in(w_in)
    w_small = _small_weight(w_in, v1_all).astype(BF16)
    r3 = lambda a: a.reshape(a.shape[0], 1, -1)
    rw_wts = dict(mu=r3(_shift_layout(rw_mu)), w0=r3(rw_w0), w2=_pad_rows(rw_w2, LANES), a0=r3(rw_a0),
                  a2=_pad_rows(rw_a2, LANES), g2=_pad_rows(rw_g2, 2 * LANES), kk=r3(rw_kk), ka=r3(rw_ka),
                  rk=r3(rw_rk), lnw=r3(rw_lnx_w), lnb=r3(rw_lnx_b), v0=r3(rw_v0), v2=_pad_rows(rw_v2, LANES))
    wa2p = _pad_rows(gla_wa2, LANES)
    ba2, onorm = r3(gla_ba2), r3(gla_onorm)
    g1, g2n, gm = r3(g_norm1), r3(g_norm2), r3(g_mem)

    xs_pad = jnp.pad(x_sample, ((0, 0), (0, SAMPLE_T_PAD - ts), (0, 0)))
    x = jnp.concatenate([x_prompt.reshape(mp, d), xs_pad.reshape(ms, d)], axis=0)

    tm = _pick(m, 1152, 128) if m % 128 == 0 else _pick(m, 1152, 16)
    tn_d = _pick(d, 512, LANES)
    tn_f = _pick(f, 512, LANES)

    mem_rows = mem_prompt.reshape(bp * MEM_LEN, d)
    tmm = _pick(bp * MEM_LEN, 1024, SUBLANES)
    p_mem_k = jnp.stack([_norm_mm(mem_rows, gm, w_mk, l, tm=tmm, tn=tn_d, name="mem_k") for l in range(depth)])
    p_mem_v = jnp.stack([_norm_mm(mem_rows, gm, w_mv, l, tm=tmm, tn=tn_d, name="mem_v") for l in range(depth)])
    p_mem_k = p_mem_k.reshape(depth, bp, MEM_LEN, X_W)
    p_mem_v = p_mem_v.reshape(depth, bp, MEM_LEN, X_W)

    sh0 = jnp.zeros((bp, 1, SHIFT_W), F32)
    s_shift_in = _shift_layout(state_rwkv_shift).reshape(depth, bs, 1, SHIFT_W)

    bb_s = RW_SLAB // SAMPLE_T_PAD
    tq_p = _pick(tp, 512, SUBLANES)
    common = dict(depth=depth, m_total=m)
    prompt = dict(r0=0, nseq=bp, t=tp)
    sample = dict(r0=mp, nseq=bs, t=SAMPLE_T_PAD)
    vf_p = vf_s = None
    pg_all = sg_all = pr_all = sr_all = None
    p_shift, s_shift = [], []
    for l in range(depth):
        proj, gates = _in_proj(x, g1, w_main, w_small, l, tm=tm)

        o_gla, pg_all = _gla(proj, wa2p, ba2, onorm, None, None, pg_all, l, **common, **prompt,
                             ch=GLA_CHUNK, bb=1, t_valid=GLA_CHUNK)
        o_gla, sg_all = _gla(proj, wa2p, ba2, onorm, state_gla, o_gla, sg_all, l, **common, **sample,
                             ch=SAMPLE_T_PAD, bb=bb_s, t_valid=ts)
        o_rw, pr_all, sh_p, vf_new_p = _rwkv(proj, vf_p, sh0, None, None, pr_all, rw_wts, l, **common, **prompt,
                                             ch=RW_SLAB, bb=1, t_valid=RW_SLAB)
        o_rw, sr_all, sh_s, vf_new_s = _rwkv(proj, vf_s, s_shift_in[l], state_rwkv, o_rw, sr_all, rw_wts, l,
                                             **common, **sample, ch=SAMPLE_T_PAD, bb=bb_s, t_valid=ts)
        if l == 0:
            vf_p, vf_s = vf_new_p, vf_new_s
        o_x = _xattn(proj, p_mem_k, p_mem_v, None, l, m_total=m, **prompt, tq=tq_p, bb=1)
        o_x = _xattn_cache(proj, cache_mem_k, cache_mem_v, o_x, l, r0=mp, nseq=bs, tq=SAMPLE_T_PAD, bb=bb_s)

        merged = _merge(o_gla, o_rw, o_x, gates, w_branch, l, tm=tm, tn=tn_d)
        x = _mm_res(merged, w_out, l, x, tm=tm, tn=tn_d, tk=d, name="out_proj")
        act = _ffn_up(x, g2n, w_ff_gate, w_ff_up, l, tm=tm, tn=tn_f)
        x = _mm_res(act, w_ff_down, l, x, tm=tm, tn=tn_d, tk=_pick(f, 2048, LANES), name="ffn_down")
        p_shift.append(_shift_unlayout(sh_p[:, 0]))
        s_shift.append(_shift_unlayout(sh_s[:, 0]))

    y = _final_norm(x, g_final.reshape(1, d))
    y_prompt = y[:mp].reshape(bp, tp, d)
    y_sample = y[mp:].reshape(bs, SAMPLE_T_PAD, d)[:, :ts]
    return (y_prompt, y_sample, pg_all, pr_all, jnp.stack(p_shift),
            p_mem_k.reshape(depth, bp, MEM_LEN, X_HEADS, X_DH), p_mem_v.reshape(depth, bp, MEM_LEN, X_HEADS, X_DH),
            sg_all, sr_all, jnp.stack(s_shift))
```

```python
import functools

import jax
import jax.numpy as jnp
from jax import lax
from jax.experimental import pallas as pl
from jax.experimental.pallas import tpu as pltpu

F32 = jnp.float32
BF16 = jnp.bfloat16

GLA_HEADS, GLA_DK, GLA_DV = 4, 128, 256
GLA_KW, GLA_VW, GLA_RANK = GLA_HEADS * GLA_DK, GLA_HEADS * GLA_DV, 16
GLA_GATE_NORM = 16.0
RW_HEADS, RW_N = 16, 64
RW_W = RW_HEADS * RW_N
RW_DECAY_R, RW_A_R, RW_V_R, RW_G_R = 64, 64, 32, 160
RW_GN_EPS = 64e-5
MEM_LEN, X_HEADS, X_DH = 256, 4, 256
X_W = X_HEADS * X_DH
N_BRANCH = 3
NORM_EPS = 1e-6
GLA_COLS = 2 * GLA_KW + 2 * GLA_VW + GLA_RANK
RW_COLS = 3 * RW_W + RW_DECAY_R + RW_A_R + RW_G_R

LANES = 128
SUBLANES = 8
VMEM_LIMIT_BYTES = 56 * 1024 * 1024

C_GQ, C_GK, C_GV, C_GG = 0, 512, 1024, 2048
C_RR, C_RK, C_RV, C_XQ, C_SMALL = 3072, 4096, 5120, 6144, 7168
SMALL_W = 1024
IN_TILE = 512
S_GA, S_XW, S_XA, S_XG, S_V1 = 0, 128, 256, 384, 640
SHIFT_W = 3 * RW_W + SMALL_W

SAMPLE_T_PAD = 8
RW_SLAB = 64
RW_PAIR_GROUP = 8
GLA_CHUNK = 64


def _cp(sem):
    return pltpu.CompilerParams(dimension_semantics=sem, vmem_limit_bytes=VMEM_LIMIT_BYTES)


def _pick(n, target, mult):
    best = None
    for d in range(mult, min(n, target) + 1, mult):
        if n % d == 0:
            best = d
    assert best is not None, (n, target, mult)
    return best


def _bdot(a, b):
    return jnp.dot(a.astype(BF16), b.astype(BF16), preferred_element_type=F32)


def _bdot_nt(a, b):
    return lax.dot_general(a.astype(BF16), b.astype(BF16), (((1,), (1,)), ((), ())), preferred_element_type=F32)


def _bdot_tn(a, b):
    return lax.dot_general(a.astype(BF16), b.astype(BF16), (((0,), (0,)), ((), ())), preferred_element_type=F32)


def _softplus(y):
    return jnp.maximum(y, 0.0) + jnp.log(1.0 + jnp.exp(-jnp.abs(y)))


def _sigmoid(y):
    return 1.0 / (1.0 + jnp.exp(-y))


def _cumsum_rows(x, block):
    pos = lax.broadcasted_iota(jnp.int32, x.shape, 0) % block
    d = 1
    while d < block:
        x = x + jnp.where(pos >= d, pltpu.roll(x, d, axis=0), 0.0)
        d *= 2
    return x


def _last_row_of_each(x, bb, ch):
    parts = [jnp.broadcast_to(x[s * ch + ch - 1:s * ch + ch, :], (ch, x.shape[1])) for s in range(bb)]
    return parts[0] if bb == 1 else jnp.concatenate(parts, axis=0)


def _alias_inputs(bufs):
    arrays, specs, alias = [], [], {}
    for out_idx, buf in enumerate(bufs):
        if buf is not None:
            alias[len(arrays)] = out_idx
            arrays.append(buf)
            specs.append(pl.BlockSpec(memory_space=pl.ANY))
    return arrays, specs, alias


def _rms_rows(x_ref, g_ref, h_ref, rows, eps):
    n = x_ref.shape[0] // rows

    def body(r, c):
        sl = pl.ds(pl.multiple_of(r * rows, rows), rows)
        x = x_ref[sl, :]
        ms = jnp.mean(x * x, axis=-1, keepdims=True)
        h_ref[sl, :] = (x * lax.rsqrt(ms + eps) * g_ref[...]).astype(h_ref.dtype)
        return c

    lax.fori_loop(0, n, body, 0)


def _norm_mm_body(x_ref, g_ref, w_ref, o_ref, h_ref, *, rows):
    @pl.when(pl.program_id(1) == 0)
    def _():
        _rms_rows(x_ref, g_ref, h_ref, rows, NORM_EPS)

    o_ref[...] = jnp.dot(h_ref[...], w_ref[...].astype(BF16), preferred_element_type=F32).astype(o_ref.dtype)


def _norm_mm(x, g3, w3, l, *, tm, tn, name):
    m, d = x.shape
    n = w3.shape[-1]
    return pl.pallas_call(
        functools.partial(_norm_mm_body, rows=_pick(tm, 128, SUBLANES)),
        grid=(m // tm, n // tn),
        in_specs=[pl.BlockSpec((tm, d), lambda i, j: (i, 0)),
                  pl.BlockSpec((None, 1, d), lambda i, j: (l, 0, 0)),
                  pl.BlockSpec((None, d, tn), lambda i, j: (l, 0, j))],
        out_specs=pl.BlockSpec((tm, tn), lambda i, j: (i, j)),
        out_shape=jax.ShapeDtypeStruct((m, n), F32),
        scratch_shapes=[pltpu.VMEM((tm, d), BF16)],
        compiler_params=_cp(("parallel", "arbitrary")),
        name=name,
    )(x, g3, w3)


def _in_proj_body(x_ref, g_ref, wm_ref, ws_ref, o_ref, og_ref, h_ref, *, rows, n_main, n_small):
    j = pl.program_id(1)

    @pl.when(j == 0)
    def _():
        _rms_rows(x_ref, g_ref, h_ref, rows, NORM_EPS)

    @pl.when(j < n_main)
    def _():
        o_ref[...] = _bdot_nt(h_ref[...], wm_ref[0])

    @pl.when((j >= n_main) & (j < n_main + n_small))
    def _():
        o_ref[...] = _bdot_nt(h_ref[...], ws_ref[...])

    @pl.when(j >= n_main + n_small)
    def _():
        og_ref[...] = _sigmoid(_bdot_nt(h_ref[...], wm_ref[0])).astype(og_ref.dtype)


def _in_proj(x, g3, w_t, w_small_t, l, *, tm):
    m, d = x.shape
    tn = IN_TILE
    n_main, n_small = C_SMALL // tn, SMALL_W // tn
    xq0 = GLA_COLS + RW_COLS
    n_gla, n_rw = C_RR // tn, (C_XQ - C_RR) // tn
    n_gate = (w_t.shape[1] - xq0 - X_W) // tn

    def w_rows(i, j):
        jm = jnp.where(j < n_main, j, jnp.maximum(j - n_small, n_main - 1))
        start = jnp.where(jm < n_gla, jm * tn,
                          jnp.where(jm < n_gla + n_rw, GLA_COLS + (jm - n_gla) * tn, xq0 + (jm - n_gla - n_rw) * tn))
        return (l, pl.multiple_of(start, 2 * SUBLANES), 0)

    return pl.pallas_call(
        functools.partial(_in_proj_body, rows=_pick(tm, 128, SUBLANES), n_main=n_main, n_small=n_small),
        grid=(m // tm, n_main + n_small + n_gate),
        in_specs=[pl.BlockSpec((tm, d), lambda i, j: (i, 0)),
                  pl.BlockSpec((None, 1, d), lambda i, j: (l, 0, 0)),
                  pl.BlockSpec((pl.Element(1), pl.Element(tn), pl.Element(d)), w_rows),
                  pl.BlockSpec((None, tn, d), lambda i, j: (l, jnp.clip(j - n_main, 0, n_small - 1), 0))],
        out_specs=[pl.BlockSpec((tm, tn), lambda i, j: (i, jnp.minimum(j, n_main + n_small - 1))),
                   pl.BlockSpec((tm, tn), lambda i, j: (i, jnp.maximum(j - n_main - n_small, 0)))],
        out_shape=[jax.ShapeDtypeStruct((m, (n_main + n_small) * tn), F32),
                   jax.ShapeDtypeStruct((m, n_gate * tn), BF16)],
        scratch_shapes=[pltpu.VMEM((tm, d), BF16)],
        compiler_params=_cp(("parallel", "arbitrary")),
        name="in_proj",
    )(x, g3, w_t, w_small_t)


def _merge_body(og_ref, or_ref, ox_ref, gg_ref, gr_ref, gx_ref, w_ref, o_ref):
    acc = gg_ref[...].astype(F32) * _bdot(og_ref[...], w_ref[0])
    acc += gr_ref[...].astype(F32) * _bdot(or_ref[...], w_ref[1])
    acc += gx_ref[...].astype(F32) * _bdot(ox_ref[...], w_ref[2])
    o_ref[...] = acc.astype(o_ref.dtype)


def _merge(o_gla, o_rw, o_x, gates, w_branch, l, *, tm, tn):
    m, bw = o_gla.shape
    d = w_branch.shape[-1]
    gate_blk = [b * d // tn for b in range(N_BRANCH)]
    o_spec = pl.BlockSpec((tm, bw), lambda i, j: (i, 0))
    return pl.pallas_call(
        _merge_body,
        grid=(m // tm, d // tn),
        in_specs=[o_spec, o_spec, o_spec]
        + [pl.BlockSpec((tm, tn), functools.partial(lambda i, j, off: (i, off + j), off=gate_blk[b]))
           for b in range(N_BRANCH)]
        + [pl.BlockSpec((None, N_BRANCH, bw, tn), lambda i, j: (l, 0, 0, j))],
        out_specs=pl.BlockSpec((tm, tn), lambda i, j: (i, j)),
        out_shape=jax.ShapeDtypeStruct((m, d), BF16),
        compiler_params=_cp(("parallel", "arbitrary")),
        name="branch_merge",
    )(o_gla, o_rw, o_x, gates, gates, gates, w_branch)


def _mm_res_body(a_ref, w_ref, r_ref, o_ref, acc_ref):
    k = pl.program_id(2)

    @pl.when(k == 0)
    def _():
        acc_ref[...] = r_ref[...]

    acc_ref[...] += _bdot(a_ref[...], w_ref[...])

    @pl.when(k == pl.num_programs(2) - 1)
    def _():
        o_ref[...] = acc_ref[...]


def _mm_res(a, w3, l, res, *, tm, tn, tk, name):
    m, kd = a.shape
    n = w3.shape[-1]
    return pl.pallas_call(
        _mm_res_body,
        grid=(m // tm, n // tn, kd // tk),
        in_specs=[pl.BlockSpec((tm, tk), lambda i, j, k: (i, k)),
                  pl.BlockSpec((None, tk, tn), lambda i, j, k: (l, k, j)),
                  pl.BlockSpec((tm, tn), lambda i, j, k: (i, j))],
        out_specs=pl.BlockSpec((tm, tn), lambda i, j, k: (i, j)),
        out_shape=jax.ShapeDtypeStruct((m, n), F32),
        scratch_shapes=[pltpu.VMEM((tm, tn), F32)],
        compiler_params=_cp(("parallel", "arbitrary", "arbitrary")),
        name=name,
    )(a, w3, res)


def _ffn_up_body(x_ref, g_ref, wg_ref, wu_ref, o_ref, h_ref, *, rows):
    @pl.when(pl.program_id(1) == 0)
    def _():
        _rms_rows(x_ref, g_ref, h_ref, rows, NORM_EPS)

    h = h_ref[...]
    a = jnp.dot(h, wg_ref[...].astype(BF16), preferred_element_type=F32)
    u = jnp.dot(h, wu_ref[...].astype(BF16), preferred_element_type=F32)
    o_ref[...] = (a * _sigmoid(a) * u).astype(o_ref.dtype)


def _ffn_up(x, g3, wg, wu, l, *, tm, tn):
    m, d = x.shape
    f = wg.shape[-1]
    w_spec = pl.BlockSpec((None, d, tn), lambda i, j: (l, 0, j))
    return pl.pallas_call(
        functools.partial(_ffn_up_body, rows=_pick(tm, 128, SUBLANES)),
        grid=(m // tm, f // tn),
        in_specs=[pl.BlockSpec((tm, d), lambda i, j: (i, 0)),
                  pl.BlockSpec((None, 1, d), lambda i, j: (l, 0, 0)),
                  w_spec, w_spec],
        out_specs=pl.BlockSpec((tm, tn), lambda i, j: (i, j)),
        out_shape=jax.ShapeDtypeStruct((m, f), BF16),
        scratch_shapes=[pltpu.VMEM((tm, d), BF16)],
        compiler_params=_cp(("parallel", "arbitrary")),
        name="ffn_up",
    )(x, g3, wg, wu)


def _final_norm_body(x_ref, g_ref, o_ref):
    x = x_ref[...]
    ms = jnp.mean(x * x, axis=-1, keepdims=True)
    o_ref[...] = x * lax.rsqrt(ms + NORM_EPS) * g_ref[...]


def _final_norm(x, g2):
    m, d = x.shape
    tm = _pick(m, 256, SUBLANES)
    return pl.pallas_call(
        _final_norm_body,
        grid=(m // tm,),
        in_specs=[pl.BlockSpec((tm, d), lambda i: (i, 0)), pl.BlockSpec((1, d), lambda i: (0, 0))],
        out_specs=pl.BlockSpec((tm, d), lambda i: (i, 0)),
        out_shape=jax.ShapeDtypeStruct((m, d), F32),
        compiler_params=_cp(("parallel",)),
        name="final_norm",
    )(x, g2)


def _gla_body(*refs, names, bb, ch, t_valid, zero_init):
    r = dict(zip(names, refs))
    nr = bb * ch
    s_ref = r["s_out"]

    @pl.when(pl.program_id(1) == 0)
    def _():
        if zero_init:
            s_ref[...] = jnp.zeros(s_ref.shape, F32)
        else:
            s_ref[...] = r["s0"][...]

    row = lax.broadcasted_iota(jnp.int32, (ch, ch), 0)
    col = lax.broadcasted_iota(jnp.int32, (ch, ch), 1)
    causal = col <= row
    z = _bdot(r["ga"][...], r["wa2"][...]) + r["ba2"][...]
    la = -_softplus(-z) * (1.0 / GLA_GATE_NORM)
    kk = r["k"][...]
    if t_valid < ch:
        valid = (lax.broadcasted_iota(jnp.int32, la.shape, 0) % ch) < t_valid
        la = jnp.where(valid, la, 0.0)
        kk = jnp.where(valid, kk, 0.0)
    bc = _cumsum_rows(la, ch)
    b_end = _last_row_of_each(bc, bb, ch)
    qe = r["q"][...] * (GLA_DK ** -0.5) * jnp.exp(bc)
    ke = kk * jnp.exp(-bc)
    kl = kk * jnp.exp(b_end - bc)
    dec_rows = jnp.exp(b_end)

    units = [(s, h) for s in range(bb) for h in range(GLA_HEADS)]

    def sl(s, h):
        return (slice(s * ch, (s + 1) * ch), slice(h * GLA_DK, (h + 1) * GLA_DK),
                slice(h * GLA_DV, (h + 1) * GLA_DV))

    att, o_int, upd = {}, {}, {}
    for s, h in units:
        rows, ks, vs = sl(s, h)
        att[s, h] = jnp.where(causal, _bdot_nt(qe[rows, ks], ke[rows, ks]), 0.0)
        o_int[s, h] = _bdot(qe[rows, ks], s_ref[s, h])
    for s, h in units:
        rows, ks, vs = sl(s, h)
        vh = r["v"][rows, vs]
        o = o_int[s, h] + _bdot(att[s, h], vh)
        upd[s, h] = _bdot_tn(kl[rows, ks], vh)
        ms = jnp.mean(o * o, axis=-1, keepdims=True)
        gh = r["g"][rows, vs]
        r["of"][rows, vs] = o * lax.rsqrt(ms + NORM_EPS) * r["on"][...] * (gh * _sigmoid(gh))
    for s, h in units:
        rows, ks, vs = sl(s, h)
        dec_col = jnp.broadcast_to(dec_rows[s * ch:s * ch + 1, ks], (GLA_DK, GLA_DK)).T
        dec = jnp.concatenate([dec_col] * (GLA_DV // GLA_DK), axis=1)
        s_ref[s, h] = s_ref[s, h] * dec + upd[s, h]
    r["o"][...] = r["of"][...].astype(r["o"].dtype)
    del nr


def _gla(proj, wa2p, ba2, onorm, s0, o_buf, st_buf, l, *, depth, m_total, r0, nseq, t, ch, bb, t_valid):
    rb = bb * ch
    nc = t // ch
    assert (bb == 1 or nc == 1) and r0 % rb == 0

    def rblk(ib, c):
        return r0 // rb + ib * nc + c

    def sec(width, off):
        return pl.BlockSpec((rb, width), lambda ib, c: (rblk(ib, c), off // width))

    st_spec = pl.BlockSpec((None, bb, GLA_HEADS, GLA_DK, GLA_DV), lambda ib, c: (l, ib, 0, 0, 0))
    names = ["q", "k", "v", "g", "ga", "wa2", "ba2", "on"]
    args = [proj, proj, proj, proj, proj, wa2p, ba2, onorm]
    in_specs = [sec(GLA_KW, C_GQ), sec(GLA_KW, C_GK), sec(GLA_VW, C_GV), sec(GLA_VW, C_GG),
                sec(LANES, proj.shape[1] - SMALL_W + S_GA),
                pl.BlockSpec((None, LANES, GLA_KW), lambda ib, c: (l, 0, 0)),
                pl.BlockSpec((None, 1, GLA_KW), lambda ib, c: (l, 0, 0)),
                pl.BlockSpec((None, 1, GLA_DV), lambda ib, c: (l, 0, 0))]
    if s0 is not None:
        names.append("s0")
        args.append(s0)
        in_specs.append(st_spec)
    al_arrays, al_specs, alias = _alias_inputs([o_buf, st_buf])
    alias = {len(args) + k: v for k, v in alias.items()}
    names += ["alias%d" % i for i in range(len(al_arrays))] + ["o", "s_out", "of"]
    return pl.pallas_call(
        functools.partial(_gla_body, names=names, bb=bb, ch=ch, t_valid=t_valid, zero_init=s0 is None),
        grid=(nseq // bb, nc),
        in_specs=in_specs + al_specs,
        out_specs=[pl.BlockSpec((rb, GLA_VW), lambda ib, c: (rblk(ib, c), 0)), st_spec],
        out_shape=[jax.ShapeDtypeStruct((m_total, GLA_VW), BF16),
                   jax.ShapeDtypeStruct((depth, nseq, GLA_HEADS, GLA_DK, GLA_DV), F32)],
        scratch_shapes=[pltpu.VMEM((rb, GLA_VW), F32)],
        input_output_aliases=alias,
        compiler_params=_cp(("parallel", "arbitrary")),
        name="gla",
    )(*args, *al_arrays)


def _rwkv_body(*refs, names, bb, ch, t_valid, first, zero_init):
    r = dict(zip(names, refs))
    st_ref, carry_ref = r["st"], r["carry"]
    nr = bb * ch
    n2 = 2 * nr
    npair = RW_HEADS // 2
    c_id = pl.program_id(1)
    last_chunk = c_id == pl.num_programs(1) - 1
    lane = lax.broadcasted_iota(jnp.int32, (nr, LANES), 1)
    head_a = lane < RW_N
    r2 = lax.broadcasted_iota(jnp.int32, (n2, n2), 0)
    c2 = lax.broadcasted_iota(jnp.int32, (n2, n2), 1)
    same_blk = (r2 // ch) == (c2 // ch)
    strict = same_blk & (c2 < r2)
    incl = same_blk & (c2 <= r2)
    eye = (r2 == c2).astype(F32)
    blockdiag = (lax.broadcasted_iota(jnp.int32, (LANES, LANES), 0) < RW_N) == \
                (lax.broadcasted_iota(jnp.int32, (LANES, LANES), 1) < RW_N)
    zeros64 = jnp.zeros((RW_N, RW_N), F32)

    @pl.when(c_id == 0)
    def _():
        carry_ref[...] = r["sh"][:, 0, :]
        for s in range(bb):
            for p in range(npair):
                if zero_init:
                    st_ref[s, p] = jnp.zeros((LANES, LANES), F32)
                else:
                    top = jnp.concatenate([r["s0"][s, 2 * p], zeros64], axis=1)
                    bot = jnp.concatenate([zeros64, r["s0"][s, 2 * p + 1]], axis=1)
                    st_ref[s, p] = jnp.concatenate([top, bot], axis=0)

    rowpos = lax.broadcasted_iota(jnp.int32, (nr, RW_W), 0) % ch
    sections = (("pr", 0), ("pk", RW_W), ("pv", 2 * RW_W), ("ps", 3 * RW_W))

    def shifted(name, off):
        p = r[name][...]
        prev = pltpu.roll(p, 1, axis=0)
        parts = [jnp.broadcast_to(carry_ref[s:s + 1, off:off + RW_W], (ch, RW_W)) for s in range(bb)]
        first_rows = parts[0] if bb == 1 else jnp.concatenate(parts, axis=0)
        prev = jnp.where(rowpos == 0, first_rows, prev)
        return p, p + r["mu"][:, off:off + RW_W] * (prev - p)

    raw, feat = {}, {}
    for name, off in sections:
        raw[name], feat[name] = shifted(name, off)
    for name, off in sections:
        for s in range(bb):
            last = s * ch + ch - 1
            carry_ref[s:s + 1, off:off + RW_W] = raw[name][last:last + 1, :]

    @pl.when(last_chunk)
    def _():
        for name, off in sections:
            for s in range(bb):
                tv = s * ch + t_valid - 1
                r["sh_out"][s, :, off:off + RW_W] = raw[name][tv:tv + 1, :]

    f_r, f_k, f_v, f_s = feat["pr"], feat["pk"], feat["pv"], feat["ps"]
    xw = f_s[:, S_XW:S_XW + LANES]
    xa = f_s[:, S_XA:S_XA + LANES]
    xg = f_s[:, S_XG:S_XG + 2 * LANES]
    w_log = -_softplus(-(r["w0"][...] + _bdot(jnp.tanh(xw), r["w2"][...]))) - 0.5
    lw = -jnp.exp(w_log)
    a = _sigmoid(r["a0"][...] + _bdot(xa, r["a2"][...]))
    gate = _bdot(_sigmoid(xg), r["g2"][...])
    if first:
        vr = f_v
        r["vf_out"][...] = vr
    else:
        nu = _sigmoid(r["v0"][...] + _bdot(raw["ps"][:, S_V1:S_V1 + LANES], r["v2"][...]))
        vr = f_v + (r["vf"][...] - f_v) * nu
    kk = f_k * r["kkw"][...]
    kmod = f_k * (1.0 + (a - 1.0) * r["kaw"][...])
    if t_valid < ch:
        valid = rowpos < t_valid
        lw = jnp.where(valid, lw, 0.0)
        a = jnp.where(valid, a, 0.0)
        kmod = jnp.where(valid, kmod, 0.0)
    g_in = _cumsum_rows(lw, ch)
    g_ex = g_in - lw
    g_end = _last_row_of_each(g_in, bb, ch)
    e_in, e_ex, e_neg, e_tail = jnp.exp(g_in), jnp.exp(g_ex), jnp.exp(-g_in), jnp.exp(g_end - g_in)
    e_end = jnp.exp(g_end)

    def stack(x):
        return jnp.concatenate([jnp.where(head_a, x, 0.0), jnp.where(head_a, 0.0, x)], axis=0)

    def unstack(x):
        return x[:nr] + x[nr:]

    def pair_sum(x):
        sa = jnp.sum(jnp.where(head_a, x, 0.0), axis=-1, keepdims=True)
        sb = jnp.sum(jnp.where(head_a, 0.0, x), axis=-1, keepdims=True)
        return jnp.where(head_a, sa, sb)

    def seq_rows(s):
        return slice(s * ch, (s + 1) * ch)

    for p0 in range(0, npair, RW_PAIR_GROUP):
        group = range(p0, p0 + RW_PAIR_GROUP)
        q = {p: {} for p in group}
        for p in group:
            ls = slice(p * LANES, (p + 1) * LANES)
            d = q[p]
            kk_p = kk[:, ls]
            kap = kk_p / jnp.maximum(jnp.sqrt(pair_sum(kk_p * kk_p)), 1e-12)
            d["r"], d["k"], d["v"] = f_r[:, ls], kmod[:, ls], vr[:, ls]
            d["kap_t"] = kap * e_ex[:, ls]
            d["r_t"] = d["r"] * e_in[:, ls]
            ba = kap * a[:, ls]
            d["b_h"] = ba * e_tail[:, ls]
            d["k_h"] = d["k"] * e_tail[:, ls]
            lhs4 = jnp.concatenate([stack(d["kap_t"]), stack(d["r_t"])], axis=0)
            rhs4 = jnp.concatenate([stack(ba * e_neg[:, ls]), stack(d["k"] * e_neg[:, ls])], axis=0)
            m4 = _bdot_nt(lhs4, rhs4)
            d["n"] = jnp.where(strict, m4[:n2, :n2], 0.0)
            d["a_ak"] = jnp.where(strict, m4[:n2, n2:], 0.0)
            d["a_r"] = jnp.concatenate([jnp.where(incl, m4[n2:, :n2], 0.0), jnp.where(incl, m4[n2:, n2:], 0.0)], axis=1)
            d["dinv"] = eye
            d["v_st"] = stack(d["v"])
        for p in group:
            d = q[p]
            ks_parts, rs_parts = [], []
            for s in range(bb):
                both = _bdot_nt(jnp.concatenate([d["kap_t"][seq_rows(s)], d["r_t"][seq_rows(s)]], axis=0), st_ref[s, p])
                ks_parts.append(both[:ch])
                rs_parts.append(both[ch:])
            d["ks"] = ks_parts[0] if bb == 1 else jnp.concatenate(ks_parts, axis=0)
            d["rs"] = rs_parts[0] if bb == 1 else jnp.concatenate(rs_parts, axis=0)
            d["rhs"] = stack(d["ks"]) + _bdot(d["a_ak"], d["v_st"])
        m = 1
        while m < ch:
            low = same_blk & ((r2 // (2 * m)) == (c2 // (2 * m))) & (((r2 // m) % 2) == 1) & (((c2 // m) % 2) == 0)
            tmp = {p: _bdot(q[p]["dinv"], jnp.where(low, q[p]["n"], 0.0)) for p in group}
            for p in group:
                q[p]["dinv"] = q[p]["dinv"] - _bdot(tmp[p], q[p]["dinv"])
            m *= 2
        for p in group:
            d = q[p]
            d["u_st"] = -_bdot(d["dinv"], d["rhs"])
        for p in group:
            d = q[p]
            ls = slice(p * LANES, (p + 1) * LANES)
            y = unstack(stack(d["rs"]) + _bdot(d["a_r"], jnp.concatenate([d["u_st"], d["v_st"]], axis=0)))
            u_p = unstack(d["u_st"])
            for s in range(bb):
                rows = seq_rows(s)
                upd = _bdot_tn(jnp.concatenate([u_p[rows], d["v"][rows]], axis=0),
                               jnp.concatenate([d["b_h"][rows], d["k_h"][rows]], axis=0))
                st_ref[s, p] = st_ref[s, p] * e_end[s * ch:s * ch + 1, ls] + jnp.where(blockdiag, upd, 0.0)
            mean = pair_sum(y) * (1.0 / RW_N)
            yc = y - mean
            var = pair_sum(yc * yc) * (1.0 / RW_N)
            yn = yc * lax.rsqrt(var + RW_GN_EPS) * r["lnw"][:, ls] + r["lnb"][:, ls]
            bonus = pair_sum(d["r"] * d["k"] * r["rk"][:, ls]) * d["v"]
            r["o"][:, ls] = ((yn + bonus) * gate[:, ls]).astype(r["o"].dtype)

    @pl.when(last_chunk)
    def _():
        for s in range(bb):
            for p in range(npair):
                st = st_ref[s, p]
                r["s_out"][s, 2 * p] = st[:RW_N, :RW_N]
                r["s_out"][s, 2 * p + 1] = st[RW_N:, RW_N:]


def _rwkv(proj, vf, shift, s0, o_buf, st_buf, wts, l, *, depth, m_total, r0, nseq, t, ch, bb, t_valid):
    first = vf is None
    rb = bb * ch
    nc = t // ch
    assert rb == RW_SLAB and (bb == 1 or nc == 1) and r0 % rb == 0
    small_off = proj.shape[1] - SMALL_W

    def rblk(ib, c):
        return r0 // rb + ib * nc + c

    def sec(off):
        return pl.BlockSpec((rb, RW_W), lambda ib, c: (rblk(ib, c), off // RW_W))

    def vec(width, dl=0):
        return pl.BlockSpec((None, 1, width), lambda ib, c: (l - dl, 0, 0))

    def mat(rows, dl=0):
        return pl.BlockSpec((None, rows, RW_W), lambda ib, c: (l - dl, 0, 0))

    group_rows = pl.BlockSpec((rb, RW_W), lambda ib, c: (ib * nc + c, 0))
    st_spec = pl.BlockSpec((None, bb, RW_HEADS, RW_N, RW_N), lambda ib, c: (l, ib, 0, 0, 0))
    sh_spec = pl.BlockSpec((bb, 1, SHIFT_W), lambda ib, c: (ib, 0, 0))
    names = ["pr", "pk", "pv", "ps", "mu", "sh", "w0", "w2", "a0", "a2", "g2", "kkw", "kaw", "rk", "lnw", "lnb"]
    args = [proj, proj, proj, proj, wts["mu"], shift, wts["w0"], wts["w2"], wts["a0"], wts["a2"], wts["g2"],
            wts["kk"], wts["ka"], wts["rk"], wts["lnw"], wts["lnb"]]
    in_specs = [sec(C_RR), sec(C_RK), sec(C_RV), sec(small_off), vec(SHIFT_W), sh_spec,
                vec(RW_W), mat(LANES), vec(RW_W), mat(LANES), mat(2 * LANES)] + [vec(RW_W)] * 5
    if not first:
        names += ["vf", "v0", "v2"]
        args += [vf, wts["v0"], wts["v2"]]
        in_specs += [group_rows, vec(RW_W, 1), mat(LANES, 1)]
    if s0 is not None:
        names.append("s0")
        args.append(s0)
        in_specs.append(st_spec)
    al_arrays, al_specs, alias = _alias_inputs([o_buf, st_buf])
    alias = {len(args) + k: v for k, v in alias.items()}
    names += ["alias%d" % i for i in range(len(al_arrays))] + ["o", "s_out", "sh_out"]
    out_specs = [pl.BlockSpec((rb, RW_W), lambda ib, c: (rblk(ib, c), 0)), st_spec, sh_spec]
    out_shape = [jax.ShapeDtypeStruct((m_total, RW_W), BF16),
                 jax.ShapeDtypeStruct((depth, nseq, RW_HEADS, RW_N, RW_N), F32),
                 jax.ShapeDtypeStruct((nseq, 1, SHIFT_W), F32)]
    if first:
        names.append("vf_out")
        out_specs.append(group_rows)
        out_shape.append(jax.ShapeDtypeStruct((nseq * t, RW_W), F32))
    names += ["st", "carry"]
    res = pl.pallas_call(
        functools.partial(_rwkv_body, names=names, bb=bb, ch=ch, t_valid=t_valid, first=first,
                          zero_init=s0 is None),
        grid=(nseq // bb, nc),
        in_specs=in_specs + al_specs,
        out_specs=out_specs,
        out_shape=out_shape,
        scratch_shapes=[pltpu.VMEM((bb, RW_HEADS // 2, LANES, LANES), F32), pltpu.VMEM((bb, SHIFT_W), F32)],
        input_output_aliases=alias,
        compiler_params=_cp(("parallel", "arbitrary")),
        name="rwkv7",
    )(*args, *al_arrays)
    return res[0], res[1], res[2], (res[3] if first else None)


def _xattn_body(*refs, names, bb, tq):
    r = dict(zip(names, refs))

    def kv(ref, s, h):
        return ref[s, :, h * X_DH:(h + 1) * X_DH]

    units = [(s, h) for s in range(bb) for h in range(X_HEADS)]
    pr = {}
    for s, h in units:
        sc = _bdot_nt(r["q"][s * tq:(s + 1) * tq, h * X_DH:(h + 1) * X_DH], kv(r["k"], s, h)) * (X_DH ** -0.5)
        e = jnp.exp(sc - jnp.max(sc, axis=-1, keepdims=True))
        pr[s, h] = e / jnp.sum(e, axis=-1, keepdims=True)
    for s, h in units:
        r["of"][s * tq:(s + 1) * tq, h * X_DH:(h + 1) * X_DH] = _bdot(pr[s, h], kv(r["v"], s, h))
    r["o"][...] = r["of"][...].astype(r["o"].dtype)


def _xattn(proj, mem_k, mem_v, o_buf, l, *, m_total, r0, nseq, t, tq, bb):
    rb = bb * tq
    nq = t // tq
    assert (bb == 1 or nq == 1) and r0 % rb == 0
    kv_spec = pl.BlockSpec((None, bb, MEM_LEN, X_W), lambda ib, c: (l, ib, 0, 0))
    rows = lambda ib, c: r0 // rb + ib * nq + c
    al_arrays, al_specs, alias = _alias_inputs([o_buf])
    alias = {3 + k: v for k, v in alias.items()}
    names = ["q", "k", "v"] + ["alias%d" % i for i in range(len(al_arrays))] + ["o", "of"]
    return pl.pallas_call(
        functools.partial(_xattn_body, names=names, bb=bb, tq=tq),
        grid=(nseq // bb, nq),
        in_specs=[pl.BlockSpec((rb, X_W), lambda ib, c: (rows(ib, c), C_XQ // X_W)), kv_spec, kv_spec] + al_specs,
        out_specs=pl.BlockSpec((rb, X_W), lambda ib, c: (rows(ib, c), 0)),
        out_shape=jax.ShapeDtypeStruct((m_total, X_W), BF16),
        scratch_shapes=[pltpu.VMEM((rb, X_W), F32)],
        input_output_aliases=alias,
        compiler_params=_cp(("parallel", "arbitrary")),
        name="mem_xattn",
    )(proj, mem_k, mem_v, *al_arrays)


def _xattn_cache_body(q_ref, k_hbm, v_hbm, alias_ref, o_ref, of_ref, kbuf, vbuf, sem, *, l, bb, tq):
    del alias_ref
    i = pl.program_id(0)
    n = pl.num_programs(0)

    def copies(step, slot):
        out = []
        for h in range(X_HEADS):
            for t, (src, dst) in enumerate(((k_hbm, kbuf), (v_hbm, vbuf))):
                out.append(pltpu.make_async_copy(src.at[l, pl.ds(step * bb, bb), :, h, :], dst.at[slot, h],
                                                 sem.at[slot, t, h]))
        return out

    @pl.when(i == 0)
    def _():
        for c in copies(0, 0):
            c.start()

    @pl.when(i + 1 < n)
    def _():
        for c in copies(i + 1, (i + 1) % 2):
            c.start()

    slot = i % 2
    for c in copies(i, slot):
        c.wait()

    units = [(s, h) for s in range(bb) for h in range(X_HEADS)]
    pr = {}
    for s, h in units:
        sc = _bdot_nt(q_ref[s * tq:(s + 1) * tq, h * X_DH:(h + 1) * X_DH], kbuf[slot, h, s]) * (X_DH ** -0.5)
        e = jnp.exp(sc - jnp.max(sc, axis=-1, keepdims=True))
        pr[s, h] = e / jnp.sum(e, axis=-1, keepdims=True)
    for s, h in units:
        of_ref[s * tq:(s + 1) * tq, h * X_DH:(h + 1) * X_DH] = _bdot(pr[s, h], vbuf[slot, h, s])
    o_ref[...] = of_ref[...].astype(o_ref.dtype)


def _xattn_cache(proj, cache_k, cache_v, o_buf, l, *, r0, nseq, tq, bb):
    rb = bb * tq
    assert r0 % rb == 0 and nseq % bb == 0
    rows = lambda ib: (r0 // rb + ib, 0)
    any_spec = pl.BlockSpec(memory_space=pl.ANY)
    buf = pltpu.VMEM((2, X_HEADS, bb, MEM_LEN, X_DH), F32)
    return pl.pallas_call(
        functools.partial(_xattn_cache_body, l=l, bb=bb, tq=tq),
        grid=(nseq // bb,),
        in_specs=[pl.BlockSpec((rb, X_W), lambda ib: (r0 // rb + ib, C_XQ // X_W)), any_spec, any_spec, any_spec],
        out_specs=pl.BlockSpec((rb, X_W), rows),
        out_shape=jax.ShapeDtypeStruct(o_buf.shape, o_buf.dtype),
        scratch_shapes=[pltpu.VMEM((rb, X_W), F32), buf, buf, pltpu.SemaphoreType.DMA((2, 2, X_HEADS))],
        input_output_aliases={3: 0},
        compiler_params=_cp(("arbitrary",)),
        name="cache_xattn",
    )(proj, cache_k, cache_v, o_buf)


def _pad_last(a, width):
    return jnp.pad(a, [(0, 0)] * (a.ndim - 1) + [(0, width - a.shape[-1])])


def _pad_rows(a, rows):
    return jnp.pad(a, [(0, 0)] * (a.ndim - 2) + [(0, rows - a.shape[-2]), (0, 0)])


def _small_section(ga, xw, xa, xg, v1):
    return jnp.concatenate([_pad_last(ga, S_XW - S_GA), _pad_last(xw, S_XA - S_XW), _pad_last(xa, S_XG - S_XA),
                            _pad_last(xg, S_V1 - S_XG), _pad_last(v1, SMALL_W - S_V1)], axis=-1)


def _small_weight_t(w_t, v1_t):
    sm = GLA_COLS + 3 * RW_W
    xq = GLA_COLS + RW_COLS
    t = lambda a: jnp.swapaxes(a, -1, -2)
    small = _small_section(t(w_t[:, 3072:GLA_COLS]), t(w_t[:, sm:sm + RW_DECAY_R]),
                           t(w_t[:, sm + RW_DECAY_R:sm + RW_DECAY_R + RW_A_R]),
                           t(w_t[:, sm + RW_DECAY_R + RW_A_R:xq]), t(v1_t))
    return t(small)


def _shift_layout(a):
    z16 = jnp.zeros(a.shape[:-1] + (GLA_RANK,), a.dtype)
    z32 = jnp.zeros(a.shape[:-1] + (RW_V_R,), a.dtype)
    o = 3 * RW_W
    small = _small_section(z16, a[..., o:o + RW_DECAY_R], a[..., o + RW_DECAY_R:o + RW_DECAY_R + RW_A_R],
                           a[..., o + RW_DECAY_R + RW_A_R:], z32)
    return jnp.concatenate([a[..., :o], small], axis=-1)


def _shift_unlayout(a):
    o = 3 * RW_W
    return jnp.concatenate([a[..., :o], a[..., o + S_XW:o + S_XW + RW_DECAY_R], a[..., o + S_XA:o + S_XA + RW_A_R],
                            a[..., o + S_XG:o + S_XG + RW_G_R]], axis=-1)


def kernel(x_prompt, x_sample, mem_prompt, state_gla, state_rwkv, state_rwkv_shift, cache_mem_k, cache_mem_v,
           g_norm1, w_in, gla_wa2, gla_ba2, gla_onorm, rw_mu, rw_w0, rw_w2, rw_a0, rw_a2, rw_g2, rw_kk, rw_ka,
           rw_rk, rw_lnx_w, rw_lnx_b, rw_v0, rw_v1, rw_v2, g_mem, w_mk, w_mv, w_branch, w_out, g_norm2,
           w_ff_gate, w_ff_up, w_ff_down, g_final):
    depth = w_in.shape[0]
    bp, tp, d = x_prompt.shape
    bs, ts, _ = x_sample.shape
    assert d % 1024 == 0 and ts <= SAMPLE_T_PAD and tp % RW_SLAB == 0 and bs % (RW_SLAB // SAMPLE_T_PAD) == 0
    mp, ms = bp * tp, bs * SAMPLE_T_PAD
    m = mp + ms
    f = w_ff_gate.shape[-1]

    v1_all = jnp.concatenate([jnp.zeros((1, d, RW_V_R), F32), rw_v1], axis=0)
    w_t = jnp.swapaxes(w_in, 1, 2)
    w_small_t = _small_weight_t(w_t, jnp.swapaxes(v1_all, 1, 2)).astype(BF16)
    r3 = lambda a: a.reshape(a.shape[0], 1, -1)
    rw_wts = dict(mu=r3(_shift_layout(rw_mu)), w0=r3(rw_w0), w2=_pad_rows(rw_w2, LANES), a0=r3(rw_a0),
                  a2=_pad_rows(rw_a2, LANES), g2=_pad_rows(rw_g2, 2 * LANES), kk=r3(rw_kk), ka=r3(rw_ka),
                  rk=r3(rw_rk), lnw=r3(rw_lnx_w), lnb=r3(rw_lnx_b), v0=r3(rw_v0), v2=_pad_rows(rw_v2, LANES))
    wa2p = _pad_rows(gla_wa2, LANES)
    ba2, onorm = r3(gla_ba2), r3(gla_onorm)
    g1, g2n, gm = r3(g_norm1), r3(g_norm2), r3(g_mem)

    xs_pad = jnp.pad(x_sample, ((0, 0), (0, SAMPLE_T_PAD - ts), (0, 0)))
    x = jnp.concatenate([x_prompt.reshape(mp, d), xs_pad.reshape(ms, d)], axis=0)

    tm = _pick(m, 1152, 128) if m % 128 == 0 else _pick(m, 1152, 16)
    tn_d = _pick(d, 512, LANES)
    tn_f = _pick(f, 512, LANES)

    mem_rows = mem_prompt.reshape(bp * MEM_LEN, d)
    tmm = _pick(bp * MEM_LEN, 1024, SUBLANES)
    p_mem_k = jnp.stack([_norm_mm(mem_rows, gm, w_mk, l, tm=tmm, tn=tn_d, name="mem_k") for l in range(depth)])
    p_mem_v = jnp.stack([_norm_mm(mem_rows, gm, w_mv, l, tm=tmm, tn=tn_d, name="mem_v") for l in range(depth)])
    p_mem_k = p_mem_k.reshape(depth, bp, MEM_LEN, X_W)
    p_mem_v = p_mem_v.reshape(depth, bp, MEM_LEN, X_W)

    sh0 = jnp.zeros((bp, 1, SHIFT_W), F32)
    s_shift_in = _shift_layout(state_rwkv_shift).reshape(depth, bs, 1, SHIFT_W)

    bb_s = RW_SLAB // SAMPLE_T_PAD
    tq_p = _pick(tp, 512, SUBLANES)
    common = dict(depth=depth, m_total=m)
    prompt = dict(r0=0, nseq=bp, t=tp)
    sample = dict(r0=mp, nseq=bs, t=SAMPLE_T_PAD)
    vf_p = vf_s = None
    pg_all = sg_all = pr_all = sr_all = None
    p_shift, s_shift = [], []
    for l in range(depth):
        proj, gates = _in_proj(x, g1, w_t, w_small_t, l, tm=tm)

        o_gla, pg_all = _gla(proj, wa2p, ba2, onorm, None, None, pg_all, l, **common, **prompt,
                             ch=GLA_CHUNK, bb=1, t_valid=GLA_CHUNK)
        o_gla, sg_all = _gla(proj, wa2p, ba2, onorm, state_gla, o_gla, sg_all, l, **common, **sample,
                             ch=SAMPLE_T_PAD, bb=bb_s, t_valid=ts)
        o_rw, pr_all, sh_p, vf_new_p = _rwkv(proj, vf_p, sh0, None, None, pr_all, rw_wts, l, **common, **prompt,
                                             ch=RW_SLAB, bb=1, t_valid=RW_SLAB)
        o_rw, sr_all, sh_s, vf_new_s = _rwkv(proj, vf_s, s_shift_in[l], state_rwkv, o_rw, sr_all, rw_wts, l,
                                             **common, **sample, ch=SAMPLE_T_PAD, bb=bb_s, t_valid=ts)
        if l == 0:
            vf_p, vf_s = vf_new_p, vf_new_s
        o_x = _xattn(proj, p_mem_k, p_mem_v, None, l, m_total=m, **prompt, tq=tq_p, bb=1)
        o_x = _xattn_cache(proj, cache_mem_k, cache_mem_v, o_x, l, r0=mp, nseq=bs, tq=SAMPLE_T_PAD, bb=bb_s)

        merged = _merge(o_gla, o_rw, o_x, gates, w_branch, l, tm=tm, tn=tn_d)
        x = _mm_res(merged, w_out, l, x, tm=tm, tn=tn_d, tk=d, name="out_proj")
        act = _ffn_up(x, g2n, w_ff_gate, w_ff_up, l, tm=tm, tn=tn_f)
        x = _mm_res(act, w_ff_down, l, x, tm=tm, tn=tn_d, tk=_pick(f, 2048, LANES), name="ffn_down")
        p_shift.append(_shift_unlayout(sh_p[:, 0]))
        s_shift.append(_shift_unlayout(sh_s[:, 0]))

    y = _final_norm(x, g_final.reshape(1, d))
    y_prompt = y[:mp].reshape(bp, tp, d)
    y_sample = y[mp:].reshape(bs, SAMPLE_T_PAD, d)[:, :ts]
    return (y_prompt, y_sample, pg_all, pr_all, jnp.stack(p_shift),
            p_mem_k.reshape(depth, bp, MEM_LEN, X_HEADS, X_DH), p_mem_v.reshape(depth, bp, MEM_LEN, X_HEADS, X_DH),
            sg_all, sr_all, jnp.stack(s_shift))
```

```python
import functools

import jax
import jax.numpy as jnp
from jax import lax
from jax.experimental import pallas as pl
from jax.experimental.pallas import tpu as pltpu

F32 = jnp.float32
BF16 = jnp.bfloat16

GLA_HEADS, GLA_DK, GLA_DV = 4, 128, 256
GLA_KW, GLA_VW, GLA_RANK = GLA_HEADS * GLA_DK, GLA_HEADS * GLA_DV, 16
GLA_GATE_NORM = 16.0
RW_HEADS, RW_N = 16, 64
RW_W = RW_HEADS * RW_N
RW_DECAY_R, RW_A_R, RW_V_R, RW_G_R = 64, 64, 32, 160
RW_GN_EPS = 64e-5
MEM_LEN, X_HEADS, X_DH = 256, 4, 256
X_W = X_HEADS * X_DH
N_BRANCH = 3
NORM_EPS = 1e-6
GLA_COLS = 2 * GLA_KW + 2 * GLA_VW + GLA_RANK
RW_COLS = 3 * RW_W + RW_DECAY_R + RW_A_R + RW_G_R

LANES = 128
SUBLANES = 8
VMEM_LIMIT_BYTES = 56 * 1024 * 1024

C_GQ, C_GK, C_GV, C_GG = 0, 512, 1024, 2048
C_RR, C_RK, C_RV, C_XQ, C_SMALL = 3072, 4096, 5120, 6144, 7168
SMALL_W = 1024
IN_TILE = 1024
S_GA, S_XW, S_XA, S_XG, S_V1 = 0, 128, 256, 384, 640
SHIFT_W = 3 * RW_W + SMALL_W

SAMPLE_T_PAD = 8
RW_SLAB = 64
RW_PAIR_GROUP = 8
GLA_CHUNK = 64


def _cp(sem):
    return pltpu.CompilerParams(dimension_semantics=sem, vmem_limit_bytes=VMEM_LIMIT_BYTES)


def _pick(n, target, mult):
    best = None
    for d in range(mult, min(n, target) + 1, mult):
        if n % d == 0:
            best = d
    assert best is not None, (n, target, mult)
    return best


def _bdot(a, b):
    return jnp.dot(a.astype(BF16), b.astype(BF16), preferred_element_type=F32)


def _bdot_nt(a, b):
    return lax.dot_general(a.astype(BF16), b.astype(BF16), (((1,), (1,)), ((), ())), preferred_element_type=F32)


def _bdot_tn(a, b):
    return lax.dot_general(a.astype(BF16), b.astype(BF16), (((0,), (0,)), ((), ())), preferred_element_type=F32)


def _softplus(y):
    return jnp.maximum(y, 0.0) + jnp.log(1.0 + jnp.exp(-jnp.abs(y)))


def _sigmoid(y):
    return 1.0 / (1.0 + jnp.exp(-y))


def _cumsum_rows(x, block):
    pos = lax.broadcasted_iota(jnp.int32, x.shape, 0) % block
    d = 1
    while d < block:
        x = x + jnp.where(pos >= d, pltpu.roll(x, d, axis=0), 0.0)
        d *= 2
    return x


def _last_row_of_each(x, bb, ch):
    parts = [jnp.broadcast_to(x[s * ch + ch - 1:s * ch + ch, :], (ch, x.shape[1])) for s in range(bb)]
    return parts[0] if bb == 1 else jnp.concatenate(parts, axis=0)


def _alias_inputs(bufs):
    arrays, specs, alias = [], [], {}
    for out_idx, buf in enumerate(bufs):
        if buf is not None:
            alias[len(arrays)] = out_idx
            arrays.append(buf)
            specs.append(pl.BlockSpec(memory_space=pl.ANY))
    return arrays, specs, alias


def _rms_rows(x_ref, g_ref, h_ref, rows, eps):
    n = x_ref.shape[0] // rows

    def body(r, c):
        sl = pl.ds(pl.multiple_of(r * rows, rows), rows)
        x = x_ref[sl, :]
        ms = jnp.mean(x * x, axis=-1, keepdims=True)
        h_ref[sl, :] = (x * lax.rsqrt(ms + eps) * g_ref[...]).astype(h_ref.dtype)
        return c

    lax.fori_loop(0, n, body, 0)


def _norm_mm_body(x_ref, g_ref, w_ref, o_ref, h_ref, *, rows):
    @pl.when(pl.program_id(1) == 0)
    def _():
        _rms_rows(x_ref, g_ref, h_ref, rows, NORM_EPS)

    o_ref[...] = jnp.dot(h_ref[...], w_ref[...].astype(BF16), preferred_element_type=F32).astype(o_ref.dtype)


def _norm_mm(x, g3, w3, l, *, tm, tn, name):
    m, d = x.shape
    n = w3.shape[-1]
    return pl.pallas_call(
        functools.partial(_norm_mm_body, rows=_pick(tm, 128, SUBLANES)),
        grid=(m // tm, n // tn),
        in_specs=[pl.BlockSpec((tm, d), lambda i, j: (i, 0)),
                  pl.BlockSpec((None, 1, d), lambda i, j: (l, 0, 0)),
                  pl.BlockSpec((None, d, tn), lambda i, j: (l, 0, j))],
        out_specs=pl.BlockSpec((tm, tn), lambda i, j: (i, j)),
        out_shape=jax.ShapeDtypeStruct((m, n), F32),
        scratch_shapes=[pltpu.VMEM((tm, d), BF16)],
        compiler_params=_cp(("parallel", "arbitrary")),
        name=name,
    )(x, g3, w3)


def _in_proj_body(x_ref, g_ref, wm_ref, ws_ref, o_ref, og_ref, h_ref, *, rows, n_main, n_small):
    j = pl.program_id(1)

    @pl.when(j == 0)
    def _():
        _rms_rows(x_ref, g_ref, h_ref, rows, NORM_EPS)

    @pl.when(j < n_main)
    def _():
        o_ref[...] = _bdot_nt(h_ref[...], wm_ref[0])

    @pl.when((j >= n_main) & (j < n_main + n_small))
    def _():
        o_ref[...] = _bdot_nt(h_ref[...], ws_ref[...])

    @pl.when(j >= n_main + n_small)
    def _():
        og_ref[...] = _sigmoid(_bdot_nt(h_ref[...], wm_ref[0])).astype(og_ref.dtype)


def _in_proj(x, g3, w_t, w_small_t, l, *, tm):
    m, d = x.shape
    tn = IN_TILE
    n_main, n_small = C_SMALL // tn, SMALL_W // tn
    xq0 = GLA_COLS + RW_COLS
    n_gla, n_rw = C_RR // tn, (C_XQ - C_RR) // tn
    n_gate = (w_t.shape[1] - xq0 - X_W) // tn

    def w_rows(i, j):
        jm = jnp.where(j < n_main, j, jnp.maximum(j - n_small, n_main - 1))
        start = jnp.where(jm < n_gla, jm * tn,
                          jnp.where(jm < n_gla + n_rw, GLA_COLS + (jm - n_gla) * tn, xq0 + (jm - n_gla - n_rw) * tn))
        return (l, pl.multiple_of(start, 2 * SUBLANES), 0)

    return pl.pallas_call(
        functools.partial(_in_proj_body, rows=_pick(tm, 128, SUBLANES), n_main=n_main, n_small=n_small),
        grid=(m // tm, n_main + n_small + n_gate),
        in_specs=[pl.BlockSpec((tm, d), lambda i, j: (i, 0), pipeline_mode=pl.Buffered(1)),
                  pl.BlockSpec((None, 1, d), lambda i, j: (l, 0, 0)),
                  pl.BlockSpec((pl.Element(1), pl.Element(tn), pl.Element(d)), w_rows),
                  pl.BlockSpec((None, tn, d), lambda i, j: (l, jnp.clip(j - n_main, 0, n_small - 1), 0),
                               pipeline_mode=pl.Buffered(1))],
        out_specs=[pl.BlockSpec((tm, tn), lambda i, j: (i, jnp.minimum(j, n_main + n_small - 1))),
                   pl.BlockSpec((tm, tn), lambda i, j: (i, jnp.maximum(j - n_main - n_small, 0)))],
        out_shape=[jax.ShapeDtypeStruct((m, (n_main + n_small) * tn), F32),
                   jax.ShapeDtypeStruct((m, n_gate * tn), BF16)],
        scratch_shapes=[pltpu.VMEM((tm, d), BF16)],
        compiler_params=_cp(("parallel", "arbitrary")),
        name="in_proj",
    )(x, g3, w_t, w_small_t)


def _merge_body(og_ref, or_ref, ox_ref, gg_ref, gr_ref, gx_ref, w_ref, o_ref):
    acc = gg_ref[...].astype(F32) * _bdot(og_ref[...], w_ref[0])
    acc += gr_ref[...].astype(F32) * _bdot(or_ref[...], w_ref[1])
    acc += gx_ref[...].astype(F32) * _bdot(ox_ref[...], w_ref[2])
    o_ref[...] = acc.astype(o_ref.dtype)


def _merge(o_gla, o_rw, o_x, gates, w_branch, l, *, tm, tn):
    m, bw = o_gla.shape
    d = w_branch.shape[-1]
    gate_blk = [b * d // tn for b in range(N_BRANCH)]
    o_spec = pl.BlockSpec((tm, bw), lambda i, j: (i, 0))
    return pl.pallas_call(
        _merge_body,
        grid=(m // tm, d // tn),
        in_specs=[o_spec, o_spec, o_spec]
        + [pl.BlockSpec((tm, tn), functools.partial(lambda i, j, off: (i, off + j), off=gate_blk[b]))
           for b in range(N_BRANCH)]
        + [pl.BlockSpec((None, N_BRANCH, bw, tn), lambda i, j: (l, 0, 0, j))],
        out_specs=pl.BlockSpec((tm, tn), lambda i, j: (i, j)),
        out_shape=jax.ShapeDtypeStruct((m, d), BF16),
        compiler_params=_cp(("parallel", "arbitrary")),
        name="branch_merge",
    )(o_gla, o_rw, o_x, gates, gates, gates, w_branch)


def _mm_res_body(a_ref, w_ref, r_ref, o_ref):
    o_ref[...] = r_ref[...] + _bdot(a_ref[...], w_ref[...])


def _mm_res(a, w3, l, res, *, tm, tn, name):
    m, kd = a.shape
    n = w3.shape[-1]
    return pl.pallas_call(
        _mm_res_body,
        grid=(m // tm, n // tn),
        in_specs=[pl.BlockSpec((tm, kd), lambda i, j: (i, 0)),
                  pl.BlockSpec((None, kd, tn), lambda i, j: (l, 0, j)),
                  pl.BlockSpec((tm, tn), lambda i, j: (i, j))],
        out_specs=pl.BlockSpec((tm, tn), lambda i, j: (i, j)),
        out_shape=jax.ShapeDtypeStruct((m, n), F32),
        compiler_params=_cp(("parallel", "arbitrary")),
        name=name,
    )(a, w3, res)


def _ffn_up_body(x_ref, g_ref, wg_ref, wu_ref, o_ref, h_ref, *, rows):
    @pl.when(pl.program_id(1) == 0)
    def _():
        _rms_rows(x_ref, g_ref, h_ref, rows, NORM_EPS)

    h = h_ref[...]
    a = jnp.dot(h, wg_ref[...].astype(BF16), preferred_element_type=F32)
    u = jnp.dot(h, wu_ref[...].astype(BF16), preferred_element_type=F32)
    o_ref[...] = (a * _sigmoid(a) * u).astype(o_ref.dtype)


def _ffn_up(x, g3, wg, wu, l, *, tm, tn):
    m, d = x.shape
    f = wg.shape[-1]
    w_spec = pl.BlockSpec((None, d, tn), lambda i, j: (l, 0, j))
    return pl.pallas_call(
        functools.partial(_ffn_up_body, rows=_pick(tm, 128, SUBLANES)),
        grid=(m // tm, f // tn),
        in_specs=[pl.BlockSpec((tm, d), lambda i, j: (i, 0)),
                  pl.BlockSpec((None, 1, d), lambda i, j: (l, 0, 0)),
                  w_spec, w_spec],
        out_specs=pl.BlockSpec((tm, tn), lambda i, j: (i, j)),
        out_shape=jax.ShapeDtypeStruct((m, f), BF16),
        scratch_shapes=[pltpu.VMEM((tm, d), BF16)],
        compiler_params=_cp(("parallel", "arbitrary")),
        name="ffn_up",
    )(x, g3, wg, wu)


def _final_norm_body(x_ref, g_ref, o_ref):
    x = x_ref[...]
    ms = jnp.mean(x * x, axis=-1, keepdims=True)
    o_ref[...] = x * lax.rsqrt(ms + NORM_EPS) * g_ref[...]


def _final_norm(x, g2, r0, rows):
    d = x.shape[1]
    tm = _pick(rows, 256, SUBLANES)
    assert r0 % tm == 0
    return pl.pallas_call(
        _final_norm_body,
        grid=(rows // tm,),
        in_specs=[pl.BlockSpec((tm, d), lambda i: (r0 // tm + i, 0)), pl.BlockSpec((1, d), lambda i: (0, 0))],
        out_specs=pl.BlockSpec((tm, d), lambda i: (i, 0)),
        out_shape=jax.ShapeDtypeStruct((rows, d), F32),
        compiler_params=_cp(("parallel",)),
        name="final_norm",
    )(x, g2)


def _gla_body(*refs, names, bb, ch, t_valid, zero_init):
    r = dict(zip(names, refs))
    nr = bb * ch
    s_ref = r["s_out"]

    @pl.when(pl.program_id(1) == 0)
    def _():
        if zero_init:
            s_ref[...] = jnp.zeros(s_ref.shape, F32)
        else:
            s_ref[...] = r["s0"][...]

    row = lax.broadcasted_iota(jnp.int32, (ch, ch), 0)
    col = lax.broadcasted_iota(jnp.int32, (ch, ch), 1)
    causal = col <= row
    z = _bdot(r["ga"][...], r["wa2"][...]) + r["ba2"][...]
    la = -_softplus(-z) * (1.0 / GLA_GATE_NORM)
    kk = r["k"][...]
    if t_valid < ch:
        valid = (lax.broadcasted_iota(jnp.int32, la.shape, 0) % ch) < t_valid
        la = jnp.where(valid, la, 0.0)
        kk = jnp.where(valid, kk, 0.0)
    bc = _cumsum_rows(la, ch)
    b_end = _last_row_of_each(bc, bb, ch)
    qe = r["q"][...] * (GLA_DK ** -0.5) * jnp.exp(bc)
    ke = kk * jnp.exp(-bc)
    kl = kk * jnp.exp(b_end - bc)
    dec_rows = jnp.exp(b_end)

    units = [(s, h) for s in range(bb) for h in range(GLA_HEADS)]

    def sl(s, h):
        return (slice(s * ch, (s + 1) * ch), slice(h * GLA_DK, (h + 1) * GLA_DK),
                slice(h * GLA_DV, (h + 1) * GLA_DV))

    att, o_int, upd = {}, {}, {}
    for s, h in units:
        rows, ks, vs = sl(s, h)
        att[s, h] = jnp.where(causal, _bdot_nt(qe[rows, ks], ke[rows, ks]), 0.0)
        o_int[s, h] = _bdot(qe[rows, ks], s_ref[s, h])
    for s, h in units:
        rows, ks, vs = sl(s, h)
        vh = r["v"][rows, vs]
        o = o_int[s, h] + _bdot(att[s, h], vh)
        upd[s, h] = _bdot_tn(kl[rows, ks], vh)
        ms = jnp.mean(o * o, axis=-1, keepdims=True)
        gh = r["g"][rows, vs]
        r["of"][rows, vs] = o * lax.rsqrt(ms + NORM_EPS) * r["on"][...] * (gh * _sigmoid(gh))
    for s, h in units:
        rows, ks, vs = sl(s, h)
        dec_col = jnp.broadcast_to(dec_rows[s * ch:s * ch + 1, ks], (GLA_DK, GLA_DK)).T
        dec = jnp.concatenate([dec_col] * (GLA_DV // GLA_DK), axis=1)
        s_ref[s, h] = s_ref[s, h] * dec + upd[s, h]
    r["o"][...] = r["of"][...].astype(r["o"].dtype)
    del nr


def _gla(proj, wa2p, ba2, onorm, s0, o_buf, st_buf, l, *, depth, m_total, r0, nseq, t, ch, bb, t_valid):
    rb = bb * ch
    nc = t // ch
    assert (bb == 1 or nc == 1) and r0 % rb == 0

    def rblk(ib, c):
        return r0 // rb + ib * nc + c

    def sec(width, off):
        return pl.BlockSpec((rb, width), lambda ib, c: (rblk(ib, c), off // width))

    st_spec = pl.BlockSpec((None, bb, GLA_HEADS, GLA_DK, GLA_DV), lambda ib, c: (l, ib, 0, 0, 0))
    names = ["q", "k", "v", "g", "ga", "wa2", "ba2", "on"]
    args = [proj, proj, proj, proj, proj, wa2p, ba2, onorm]
    in_specs = [sec(GLA_KW, C_GQ), sec(GLA_KW, C_GK), sec(GLA_VW, C_GV), sec(GLA_VW, C_GG),
                sec(LANES, proj.shape[1] - SMALL_W + S_GA),
                pl.BlockSpec((None, LANES, GLA_KW), lambda ib, c: (l, 0, 0)),
                pl.BlockSpec((None, 1, GLA_KW), lambda ib, c: (l, 0, 0)),
                pl.BlockSpec((None, 1, GLA_DV), lambda ib, c: (l, 0, 0))]
    if s0 is not None:
        names.append("s0")
        args.append(s0)
        in_specs.append(st_spec)
    al_arrays, al_specs, alias = _alias_inputs([o_buf, st_buf])
    alias = {len(args) + k: v for k, v in alias.items()}
    names += ["alias%d" % i for i in range(len(al_arrays))] + ["o", "s_out", "of"]
    return pl.pallas_call(
        functools.partial(_gla_body, names=names, bb=bb, ch=ch, t_valid=t_valid, zero_init=s0 is None),
        grid=(nseq // bb, nc),
        in_specs=in_specs + al_specs,
        out_specs=[pl.BlockSpec((rb, GLA_VW), lambda ib, c: (rblk(ib, c), 0)), st_spec],
        out_shape=[jax.ShapeDtypeStruct((m_total, GLA_VW), BF16),
                   jax.ShapeDtypeStruct((depth, nseq, GLA_HEADS, GLA_DK, GLA_DV), F32)],
        scratch_shapes=[pltpu.VMEM((rb, GLA_VW), F32)],
        input_output_aliases=alias,
        compiler_params=_cp(("parallel", "arbitrary")),
        name="gla",
    )(*args, *al_arrays)


def _rwkv_body(*refs, names, bb, ch, t_valid, first, zero_init):
    r = dict(zip(names, refs))
    st_ref, carry_ref = r["st"], r["carry"]
    nr = bb * ch
    n2 = 2 * nr
    npair = RW_HEADS // 2
    c_id = pl.program_id(1)
    last_chunk = c_id == pl.num_programs(1) - 1
    lane = lax.broadcasted_iota(jnp.int32, (nr, LANES), 1)
    head_a = lane < RW_N
    r2 = lax.broadcasted_iota(jnp.int32, (n2, n2), 0)
    c2 = lax.broadcasted_iota(jnp.int32, (n2, n2), 1)
    same_blk = (r2 // ch) == (c2 // ch)
    strict = same_blk & (c2 < r2)
    incl = same_blk & (c2 <= r2)
    eye = (r2 == c2).astype(F32)
    blockdiag = (lax.broadcasted_iota(jnp.int32, (LANES, LANES), 0) < RW_N) == \
                (lax.broadcasted_iota(jnp.int32, (LANES, LANES), 1) < RW_N)
    zeros64 = jnp.zeros((RW_N, RW_N), F32)

    @pl.when(c_id == 0)
    def _():
        carry_ref[...] = r["sh"][:, 0, :]
        for s in range(bb):
            for p in range(npair):
                if zero_init:
                    st_ref[s, p] = jnp.zeros((LANES, LANES), F32)
                else:
                    top = jnp.concatenate([r["s0"][s, 2 * p], zeros64], axis=1)
                    bot = jnp.concatenate([zeros64, r["s0"][s, 2 * p + 1]], axis=1)
                    st_ref[s, p] = jnp.concatenate([top, bot], axis=0)

    rowpos = lax.broadcasted_iota(jnp.int32, (nr, RW_W), 0) % ch
    sections = (("pr", 0), ("pk", RW_W), ("pv", 2 * RW_W), ("ps", 3 * RW_W))

    def shifted(name, off):
        p = r[name][...]
        prev = pltpu.roll(p, 1, axis=0)
        parts = [jnp.broadcast_to(carry_ref[s:s + 1, off:off + RW_W], (ch, RW_W)) for s in range(bb)]
        first_rows = parts[0] if bb == 1 else jnp.concatenate(parts, axis=0)
        prev = jnp.where(rowpos == 0, first_rows, prev)
        return p, p + r["mu"][:, off:off + RW_W] * (prev - p)

    raw, feat = {}, {}
    for name, off in sections:
        raw[name], feat[name] = shifted(name, off)
    for name, off in sections:
        for s in range(bb):
            last = s * ch + ch - 1
            carry_ref[s:s + 1, off:off + RW_W] = raw[name][last:last + 1, :]

    @pl.when(last_chunk)
    def _():
        for name, off in sections:
            for s in range(bb):
                tv = s * ch + t_valid - 1
                r["sh_out"][s, :, off:off + RW_W] = raw[name][tv:tv + 1, :]

    f_r, f_k, f_v, f_s = feat["pr"], feat["pk"], feat["pv"], feat["ps"]
    xw = f_s[:, S_XW:S_XW + LANES]
    xa = f_s[:, S_XA:S_XA + LANES]
    xg = f_s[:, S_XG:S_XG + 2 * LANES]
    w_log = -_softplus(-(r["w0"][...] + _bdot(jnp.tanh(xw), r["w2"][...]))) - 0.5
    lw = -jnp.exp(w_log)
    a = _sigmoid(r["a0"][...] + _bdot(xa, r["a2"][...]))
    gate = _bdot(_sigmoid(xg), r["g2"][...])
    if first:
        vr = f_v
        r["vf_out"][...] = vr
    else:
        nu = _sigmoid(r["v0"][...] + _bdot(raw["ps"][:, S_V1:S_V1 + LANES], r["v2"][...]))
        vr = f_v + (r["vf"][...] - f_v) * nu
    kk = f_k * r["kkw"][...]
    kmod = f_k * (1.0 + (a - 1.0) * r["kaw"][...])
    if t_valid < ch:
        valid = rowpos < t_valid
        lw = jnp.where(valid, lw, 0.0)
        a = jnp.where(valid, a, 0.0)
        kmod = jnp.where(valid, kmod, 0.0)
    g_in = _cumsum_rows(lw, ch)
    g_ex = g_in - lw
    g_end = _last_row_of_each(g_in, bb, ch)
    e_in, e_ex, e_neg, e_tail = jnp.exp(g_in), jnp.exp(g_ex), jnp.exp(-g_in), jnp.exp(g_end - g_in)
    e_end = jnp.exp(g_end)

    def stack(x):
        return jnp.concatenate([jnp.where(head_a, x, 0.0), jnp.where(head_a, 0.0, x)], axis=0)

    def unstack(x):
        return x[:nr] + x[nr:]

    def pair_sum(x):
        sa = jnp.sum(jnp.where(head_a, x, 0.0), axis=-1, keepdims=True)
        sb = jnp.sum(jnp.where(head_a, 0.0, x), axis=-1, keepdims=True)
        return jnp.where(head_a, sa, sb)

    def seq_rows(s):
        return slice(s * ch, (s + 1) * ch)

    for p0 in range(0, npair, RW_PAIR_GROUP):
        group = range(p0, p0 + RW_PAIR_GROUP)
        q = {p: {} for p in group}
        for p in group:
            ls = slice(p * LANES, (p + 1) * LANES)
            d = q[p]
            kk_p = kk[:, ls]
            kap = kk_p / jnp.maximum(jnp.sqrt(pair_sum(kk_p * kk_p)), 1e-12)
            d["r"], d["k"], d["v"] = f_r[:, ls], kmod[:, ls], vr[:, ls]
            d["kap_t"] = kap * e_ex[:, ls]
            d["r_t"] = d["r"] * e_in[:, ls]
            ba = kap * a[:, ls]
            d["b_h"] = ba * e_tail[:, ls]
            d["k_h"] = d["k"] * e_tail[:, ls]
            lhs4 = jnp.concatenate([stack(d["kap_t"]), stack(d["r_t"])], axis=0)
            rhs4 = jnp.concatenate([stack(ba * e_neg[:, ls]), stack(d["k"] * e_neg[:, ls])], axis=0)
            m4 = _bdot_nt(lhs4, rhs4)
            d["n"] = jnp.where(strict, m4[:n2, :n2], 0.0)
            d["a_ak"] = jnp.where(strict, m4[:n2, n2:], 0.0)
            d["a_r"] = jnp.concatenate([jnp.where(incl, m4[n2:, :n2], 0.0), jnp.where(incl, m4[n2:, n2:], 0.0)], axis=1)
            d["dinv"] = eye
            d["v_st"] = stack(d["v"])
        for p in group:
            d = q[p]
            ks_parts, rs_parts = [], []
            for s in range(bb):
                both = _bdot_nt(jnp.concatenate([d["kap_t"][seq_rows(s)], d["r_t"][seq_rows(s)]], axis=0), st_ref[s, p])
                ks_parts.append(both[:ch])
                rs_parts.append(both[ch:])
            d["ks"] = ks_parts[0] if bb == 1 else jnp.concatenate(ks_parts, axis=0)
            d["rs"] = rs_parts[0] if bb == 1 else jnp.concatenate(rs_parts, axis=0)
            d["rhs"] = stack(d["ks"]) + _bdot(d["a_ak"], d["v_st"])
        m = 1
        while m < ch:
            low = same_blk & ((r2 // (2 * m)) == (c2 // (2 * m))) & (((r2 // m) % 2) == 1) & (((c2 // m) % 2) == 0)
            tmp = {p: _bdot(q[p]["dinv"], jnp.where(low, q[p]["n"], 0.0)) for p in group}
            for p in group:
                q[p]["dinv"] = q[p]["dinv"] - _bdot(tmp[p], q[p]["dinv"])
            m *= 2
        for p in group:
            d = q[p]
            d["u_st"] = -_bdot(d["dinv"], d["rhs"])
        for p in group:
            d = q[p]
            ls = slice(p * LANES, (p + 1) * LANES)
            y = unstack(stack(d["rs"]) + _bdot(d["a_r"], jnp.concatenate([d["u_st"], d["v_st"]], axis=0)))
            u_p = unstack(d["u_st"])
            for s in range(bb):
                rows = seq_rows(s)
                upd = _bdot_tn(jnp.concatenate([u_p[rows], d["v"][rows]], axis=0),
                               jnp.concatenate([d["b_h"][rows], d["k_h"][rows]], axis=0))
                st_ref[s, p] = st_ref[s, p] * e_end[s * ch:s * ch + 1, ls] + jnp.where(blockdiag, upd, 0.0)
            mean = pair_sum(y) * (1.0 / RW_N)
            yc = y - mean
            var = pair_sum(yc * yc) * (1.0 / RW_N)
            yn = yc * lax.rsqrt(var + RW_GN_EPS) * r["lnw"][:, ls] + r["lnb"][:, ls]
            bonus = pair_sum(d["r"] * d["k"] * r["rk"][:, ls]) * d["v"]
            r["o"][:, ls] = ((yn + bonus) * gate[:, ls]).astype(r["o"].dtype)

    @pl.when(last_chunk)
    def _():
        for s in range(bb):
            for p in range(npair):
                st = st_ref[s, p]
                r["s_out"][s, 2 * p] = st[:RW_N, :RW_N]
                r["s_out"][s, 2 * p + 1] = st[RW_N:, RW_N:]


def _rwkv(proj, vf, shift, s0, o_buf, st_buf, wts, l, *, depth, m_total, r0, nseq, t, ch, bb, t_valid):
    first = vf is None
    rb = bb * ch
    nc = t // ch
    assert rb == RW_SLAB and (bb == 1 or nc == 1) and r0 % rb == 0
    small_off = proj.shape[1] - SMALL_W

    def rblk(ib, c):
        return r0 // rb + ib * nc + c

    def sec(off):
        return pl.BlockSpec((rb, RW_W), lambda ib, c: (rblk(ib, c), off // RW_W))

    def vec(width, dl=0):
        return pl.BlockSpec((None, 1, width), lambda ib, c: (l - dl, 0, 0))

    def mat(rows, dl=0):
        return pl.BlockSpec((None, rows, RW_W), lambda ib, c: (l - dl, 0, 0))

    group_rows = pl.BlockSpec((rb, RW_W), lambda ib, c: (ib * nc + c, 0))
    st_spec = pl.BlockSpec((None, bb, RW_HEADS, RW_N, RW_N), lambda ib, c: (l, ib, 0, 0, 0))
    sh_spec = pl.BlockSpec((bb, 1, SHIFT_W), lambda ib, c: (ib, 0, 0))
    names = ["pr", "pk", "pv", "ps", "mu", "sh", "w0", "w2", "a0", "a2", "g2", "kkw", "kaw", "rk", "lnw", "lnb"]
    args = [proj, proj, proj, proj, wts["mu"], shift, wts["w0"], wts["w2"], wts["a0"], wts["a2"], wts["g2"],
            wts["kk"], wts["ka"], wts["rk"], wts["lnw"], wts["lnb"]]
    in_specs = [sec(C_RR), sec(C_RK), sec(C_RV), sec(small_off), vec(SHIFT_W), sh_spec,
                vec(RW_W), mat(LANES), vec(RW_W), mat(LANES), mat(2 * LANES)] + [vec(RW_W)] * 5
    if not first:
        names += ["vf", "v0", "v2"]
        args += [vf, wts["v0"], wts["v2"]]
        in_specs += [group_rows, vec(RW_W, 1), mat(LANES, 1)]
    if s0 is not None:
        names.append("s0")
        args.append(s0)
        in_specs.append(st_spec)
    al_arrays, al_specs, alias = _alias_inputs([o_buf, st_buf])
    alias = {len(args) + k: v for k, v in alias.items()}
    names += ["alias%d" % i for i in range(len(al_arrays))] + ["o", "s_out", "sh_out"]
    out_specs = [pl.BlockSpec((rb, RW_W), lambda ib, c: (rblk(ib, c), 0)), st_spec, sh_spec]
    out_shape = [jax.ShapeDtypeStruct((m_total, RW_W), BF16),
                 jax.ShapeDtypeStruct((depth, nseq, RW_HEADS, RW_N, RW_N), F32),
                 jax.ShapeDtypeStruct((nseq, 1, SHIFT_W), F32)]
    if first:
        names.append("vf_out")
        out_specs.append(group_rows)
        out_shape.append(jax.ShapeDtypeStruct((nseq * t, RW_W), F32))
    names += ["st", "carry"]
    res = pl.pallas_call(
        functools.partial(_rwkv_body, names=names, bb=bb, ch=ch, t_valid=t_valid, first=first,
                          zero_init=s0 is None),
        grid=(nseq // bb, nc),
        in_specs=in_specs + al_specs,
        out_specs=out_specs,
        out_shape=out_shape,
        scratch_shapes=[pltpu.VMEM((bb, RW_HEADS // 2, LANES, LANES), F32), pltpu.VMEM((bb, SHIFT_W), F32)],
        input_output_aliases=alias,
        compiler_params=_cp(("parallel", "arbitrary")),
        name="rwkv7",
    )(*args, *al_arrays)
    return res[0], res[1], res[2], (res[3] if first else None)


def _xattn_body(*refs, names, bb, tq):
    r = dict(zip(names, refs))

    def kv(ref, s, h):
        return ref[s, :, h * X_DH:(h + 1) * X_DH]

    units = [(s, h) for s in range(bb) for h in range(X_HEADS)]
    pr = {}
    for s, h in units:
        sc = _bdot_nt(r["q"][s * tq:(s + 1) * tq, h * X_DH:(h + 1) * X_DH], kv(r["k"], s, h)) * (X_DH ** -0.5)
        e = jnp.exp(sc - jnp.max(sc, axis=-1, keepdims=True))
        pr[s, h] = e / jnp.sum(e, axis=-1, keepdims=True)
    for s, h in units:
        r["of"][s * tq:(s + 1) * tq, h * X_DH:(h + 1) * X_DH] = _bdot(pr[s, h], kv(r["v"], s, h))
    r["o"][...] = r["of"][...].astype(r["o"].dtype)


def _xattn(proj, mem_k, mem_v, o_buf, l, *, m_total, r0, nseq, t, tq, bb):
    rb = bb * tq
    nq = t // tq
    assert (bb == 1 or nq == 1) and r0 % rb == 0
    kv_spec = pl.BlockSpec((None, bb, MEM_LEN, X_W), lambda ib, c: (l, ib, 0, 0))
    rows = lambda ib, c: r0 // rb + ib * nq + c
    al_arrays, al_specs, alias = _alias_inputs([o_buf])
    alias = {3 + k: v for k, v in alias.items()}
    names = ["q", "k", "v"] + ["alias%d" % i for i in range(len(al_arrays))] + ["o", "of"]
    return pl.pallas_call(
        functools.partial(_xattn_body, names=names, bb=bb, tq=tq),
        grid=(nseq // bb, nq),
        in_specs=[pl.BlockSpec((rb, X_W), lambda ib, c: (rows(ib, c), C_XQ // X_W)), kv_spec, kv_spec] + al_specs,
        out_specs=pl.BlockSpec((rb, X_W), lambda ib, c: (rows(ib, c), 0)),
        out_shape=jax.ShapeDtypeStruct((m_total, X_W), BF16),
        scratch_shapes=[pltpu.VMEM((rb, X_W), F32)],
        input_output_aliases=alias,
        compiler_params=_cp(("parallel", "arbitrary")),
        name="mem_xattn",
    )(proj, mem_k, mem_v, *al_arrays)


def _xattn_cache_body(q_ref, k_hbm, v_hbm, alias_ref, o_ref, of_ref, kbuf, vbuf, sem, *, l, bb, tq):
    del alias_ref
    i = pl.program_id(0)
    n = pl.num_programs(0)

    def copies(step, slot):
        out = []
        for h in range(X_HEADS):
            for t, (src, dst) in enumerate(((k_hbm, kbuf), (v_hbm, vbuf))):
                out.append(pltpu.make_async_copy(src.at[l, pl.ds(step * bb, bb), :, h, :], dst.at[slot, h],
                                                 sem.at[slot, t, h]))
        return out

    @pl.when(i == 0)
    def _():
        for c in copies(0, 0):
            c.start()

    @pl.when(i + 1 < n)
    def _():
        for c in copies(i + 1, (i + 1) % 2):
            c.start()

    slot = i % 2
    for c in copies(i, slot):
        c.wait()

    units = [(s, h) for s in range(bb) for h in range(X_HEADS)]
    pr = {}
    for s, h in units:
        sc = _bdot_nt(q_ref[s * tq:(s + 1) * tq, h * X_DH:(h + 1) * X_DH], kbuf[slot, h, s]) * (X_DH ** -0.5)
        e = jnp.exp(sc - jnp.max(sc, axis=-1, keepdims=True))
        pr[s, h] = e / jnp.sum(e, axis=-1, keepdims=True)
    for s, h in units:
        of_ref[s * tq:(s + 1) * tq, h * X_DH:(h + 1) * X_DH] = _bdot(pr[s, h], vbuf[slot, h, s])
    o_ref[...] = of_ref[...].astype(o_ref.dtype)


def _xattn_cache(proj, cache_k, cache_v, o_buf, l, *, r0, nseq, tq, bb):
    rb = bb * tq
    assert r0 % rb == 0 and nseq % bb == 0
    rows = lambda ib: (r0 // rb + ib, 0)
    any_spec = pl.BlockSpec(memory_space=pl.ANY)
    buf = pltpu.VMEM((2, X_HEADS, bb, MEM_LEN, X_DH), F32)
    return pl.pallas_call(
        functools.partial(_xattn_cache_body, l=l, bb=bb, tq=tq),
        grid=(nseq // bb,),
        in_specs=[pl.BlockSpec((rb, X_W), lambda ib: (r0 // rb + ib, C_XQ // X_W)), any_spec, any_spec, any_spec],
        out_specs=pl.BlockSpec((rb, X_W), rows),
        out_shape=jax.ShapeDtypeStruct(o_buf.shape, o_buf.dtype),
        scratch_shapes=[pltpu.VMEM((rb, X_W), F32), buf, buf, pltpu.SemaphoreType.DMA((2, 2, X_HEADS))],
        input_output_aliases={3: 0},
        compiler_params=_cp(("arbitrary",)),
        name="cache_xattn",
    )(proj, cache_k, cache_v, o_buf)


def _pad_last(a, width):
    return jnp.pad(a, [(0, 0)] * (a.ndim - 1) + [(0, width - a.shape[-1])])


def _pad_rows(a, rows):
    return jnp.pad(a, [(0, 0)] * (a.ndim - 2) + [(0, rows - a.shape[-2]), (0, 0)])


def _small_section(ga, xw, xa, xg, v1):
    return jnp.concatenate([_pad_last(ga, S_XW - S_GA), _pad_last(xw, S_XA - S_XW), _pad_last(xa, S_XG - S_XA),
                            _pad_last(xg, S_V1 - S_XG), _pad_last(v1, SMALL_W - S_V1)], axis=-1)


def _small_weight_t(w_t, v1_t):
    sm = GLA_COLS + 3 * RW_W
    xq = GLA_COLS + RW_COLS
    pieces = ((S_GA, w_t[:, 3072:GLA_COLS]), (S_XW, w_t[:, sm:sm + RW_DECAY_R]),
              (S_XA, w_t[:, sm + RW_DECAY_R:sm + RW_DECAY_R + RW_A_R]),
              (S_XG, w_t[:, sm + RW_DECAY_R + RW_A_R:xq]), (S_V1, v1_t))
    rows = lax.optimization_barrier(tuple(p for _, p in pieces))
    out = jnp.zeros((w_t.shape[0], SMALL_W, w_t.shape[2]), BF16)
    for (off, _), piece in zip(pieces, rows):
        out = lax.dynamic_update_slice(out, piece.astype(BF16), (0, off, 0))
    return out


def _shift_layout(a):
    z16 = jnp.zeros(a.shape[:-1] + (GLA_RANK,), a.dtype)
    z32 = jnp.zeros(a.shape[:-1] + (RW_V_R,), a.dtype)
    o = 3 * RW_W
    small = _small_section(z16, a[..., o:o + RW_DECAY_R], a[..., o + RW_DECAY_R:o + RW_DECAY_R + RW_A_R],
                           a[..., o + RW_DECAY_R + RW_A_R:], z32)
    return jnp.concatenate([a[..., :o], small], axis=-1)


def _shift_unlayout(a):
    o = 3 * RW_W
    return jnp.concatenate([a[..., :o], a[..., o + S_XW:o + S_XW + RW_DECAY_R], a[..., o + S_XA:o + S_XA + RW_A_R],
                            a[..., o + S_XG:o + S_XG + RW_G_R]], axis=-1)


def kernel(x_prompt, x_sample, mem_prompt, state_gla, state_rwkv, state_rwkv_shift, cache_mem_k, cache_mem_v,
           g_norm1, w_in, gla_wa2, gla_ba2, gla_onorm, rw_mu, rw_w0, rw_w2, rw_a0, rw_a2, rw_g2, rw_kk, rw_ka,
           rw_rk, rw_lnx_w, rw_lnx_b, rw_v0, rw_v1, rw_v2, g_mem, w_mk, w_mv, w_branch, w_out, g_norm2,
           w_ff_gate, w_ff_up, w_ff_down, g_final):
    depth = w_in.shape[0]
    bp, tp, d = x_prompt.shape
    bs, ts, _ = x_sample.shape
    assert d % 1024 == 0 and ts <= SAMPLE_T_PAD and tp % RW_SLAB == 0 and bs % (RW_SLAB // SAMPLE_T_PAD) == 0
    mp, ms = bp * tp, bs * SAMPLE_T_PAD
    m = mp + ms
    f = w_ff_gate.shape[-1]

    v1_all = jnp.concatenate([jnp.zeros((1, d, RW_V_R), F32), rw_v1], axis=0)
    w_t = jnp.swapaxes(w_in, 1, 2)
    w_small_t = _small_weight_t(w_t, jnp.swapaxes(v1_all, 1, 2)).astype(BF16)
    r3 = lambda a: a.reshape(a.shape[0], 1, -1)
    rw_wts = dict(mu=r3(_shift_layout(rw_mu)), w0=r3(rw_w0), w2=_pad_rows(rw_w2, LANES), a0=r3(rw_a0),
                  a2=_pad_rows(rw_a2, LANES), g2=_pad_rows(rw_g2, 2 * LANES), kk=r3(rw_kk), ka=r3(rw_ka),
                  rk=r3(rw_rk), lnw=r3(rw_lnx_w), lnb=r3(rw_lnx_b), v0=r3(rw_v0), v2=_pad_rows(rw_v2, LANES))
    wa2p = _pad_rows(gla_wa2, LANES)
    ba2, onorm = r3(gla_ba2), r3(gla_onorm)
    g1, g2n, gm = r3(g_norm1), r3(g_norm2), r3(g_mem)

    xs_pad = jnp.pad(x_sample, ((0, 0), (0, SAMPLE_T_PAD - ts), (0, 0)))
    x = jnp.concatenate([x_prompt.reshape(mp, d), xs_pad.reshape(ms, d)], axis=0)

    tm = _pick(m, 1152, 128) if m % 128 == 0 else _pick(m, 1152, 16)
    tn_d = _pick(d, 512, LANES)
    tn_f = _pick(f, 512, LANES)

    mem_rows = mem_prompt.reshape(bp * MEM_LEN, d)
    tmm = _pick(bp * MEM_LEN, 1024, SUBLANES)
    p_mem_k = jnp.stack([_norm_mm(mem_rows, gm, w_mk, l, tm=tmm, tn=tn_d, name="mem_k") for l in range(depth)])
    p_mem_v = jnp.stack([_norm_mm(mem_rows, gm, w_mv, l, tm=tmm, tn=tn_d, name="mem_v") for l in range(depth)])
    p_mem_k = p_mem_k.reshape(depth, bp, MEM_LEN, X_W)
    p_mem_v = p_mem_v.reshape(depth, bp, MEM_LEN, X_W)

    sh0 = jnp.zeros((bp, 1, SHIFT_W), F32)
    s_shift_in = _shift_layout(state_rwkv_shift).reshape(depth, bs, 1, SHIFT_W)

    bb_s = RW_SLAB // SAMPLE_T_PAD
    tq_p = _pick(tp, 512, SUBLANES)
    common = dict(depth=depth, m_total=m)
    prompt = dict(r0=0, nseq=bp, t=tp)
    sample = dict(r0=mp, nseq=bs, t=SAMPLE_T_PAD)
    vf_p = vf_s = None
    pg_all = sg_all = pr_all = sr_all = None
    p_shift, s_shift = [], []
    for l in range(depth):
        proj, gates = _in_proj(x, g1, w_t, w_small_t, l, tm=tm)

        o_gla, pg_all = _gla(proj, wa2p, ba2, onorm, None, None, pg_all, l, **common, **prompt,
                             ch=GLA_CHUNK, bb=1, t_valid=GLA_CHUNK)
        o_gla, sg_all = _gla(proj, wa2p, ba2, onorm, state_gla, o_gla, sg_all, l, **common, **sample,
                             ch=SAMPLE_T_PAD, bb=bb_s, t_valid=ts)
        o_rw, pr_all, sh_p, vf_new_p = _rwkv(proj, vf_p, sh0, None, None, pr_all, rw_wts, l, **common, **prompt,
                                             ch=RW_SLAB, bb=1, t_valid=RW_SLAB)
        o_rw, sr_all, sh_s, vf_new_s = _rwkv(proj, vf_s, s_shift_in[l], state_rwkv, o_rw, sr_all, rw_wts, l,
                                             **common, **sample, ch=SAMPLE_T_PAD, bb=bb_s, t_valid=ts)
        if l == 0:
            vf_p, vf_s = vf_new_p, vf_new_s
        o_x = _xattn(proj, p_mem_k, p_mem_v, None, l, m_total=m, **prompt, tq=tq_p, bb=1)
        o_x = _xattn_cache(proj, cache_mem_k, cache_mem_v, o_x, l, r0=mp, nseq=bs, tq=SAMPLE_T_PAD, bb=bb_s)

        merged = _merge(o_gla, o_rw, o_x, gates, w_branch, l, tm=tm, tn=tn_d)
        x = _mm_res(merged, w_out, l, x, tm=tm, tn=tn_d, name="out_proj")
        act = _ffn_up(x, g2n, w_ff_gate, w_ff_up, l, tm=tm, tn=tn_f)
        x = _mm_res(act, w_ff_down, l, x, tm=tm, tn=_pick(d, 256, LANES), name="ffn_down")
        p_shift.append(_shift_unlayout(sh_p[:, 0]))
        s_shift.append(_shift_unlayout(sh_s[:, 0]))

    y_prompt = _final_norm(x, g_final.reshape(1, d), 0, mp).reshape(bp, tp, d)
    y_sample = _final_norm(x, g_final.reshape(1, d), mp, ms).reshape(bs, SAMPLE_T_PAD, d)[:, :ts]
    return (y_prompt, y_sample, pg_all, pr_all, jnp.stack(p_shift),
            p_mem_k.reshape(depth, bp, MEM_LEN, X_HEADS, X_DH), p_mem_v.reshape(depth, bp, MEM_LEN, X_HEADS, X_DH),
            sg_all, sr_all, jnp.stack(s_shift))
```

```python
import functools

import jax
import jax.numpy as jnp
from jax import lax
from jax.experimental import pallas as pl
from jax.experimental.pallas import tpu as pltpu

F32 = jnp.float32
BF16 = jnp.bfloat16

GLA_HEADS, GLA_DK, GLA_DV = 4, 128, 256
GLA_KW, GLA_VW, GLA_RANK = GLA_HEADS * GLA_DK, GLA_HEADS * GLA_DV, 16
GLA_GATE_NORM = 16.0
RW_HEADS, RW_N = 16, 64
RW_W = RW_HEADS * RW_N
RW_DECAY_R, RW_A_R, RW_V_R, RW_G_R = 64, 64, 32, 160
RW_GN_EPS = 64e-5
MEM_LEN, X_HEADS, X_DH = 256, 4, 256
X_W = X_HEADS * X_DH
N_BRANCH = 3
NORM_EPS = 1e-6
GLA_COLS = 2 * GLA_KW + 2 * GLA_VW + GLA_RANK
RW_COLS = 3 * RW_W + RW_DECAY_R + RW_A_R + RW_G_R

LANES = 128
SUBLANES = 8
VMEM_LIMIT_BYTES = 56 * 1024 * 1024

C_GQ, C_GK, C_GV, C_GG = 0, 512, 1024, 2048
C_RR, C_RK, C_RV, C_XQ, C_SMALL = 3072, 4096, 5120, 6144, 7168
SMALL_W = 1024
IN_TILE = 1024
S_GA, S_XW, S_XA, S_XG, S_V1 = 0, 128, 256, 384, 640
SHIFT_W = 3 * RW_W + SMALL_W

RW_SLAB = 64
XC_SEQS = 8
RW_PAIR_GROUP = 8
GLA_CHUNK = 64


def _cp(sem):
    return pltpu.CompilerParams(dimension_semantics=sem, vmem_limit_bytes=VMEM_LIMIT_BYTES)


def _pick(n, target, mult):
    best = None
    for d in range(mult, min(n, target) + 1, mult):
        if n % d == 0:
            best = d
    assert best is not None, (n, target, mult)
    return best


def _bdot(a, b):
    return jnp.dot(a.astype(BF16), b.astype(BF16), preferred_element_type=F32)


def _bdot_nt(a, b):
    return lax.dot_general(a.astype(BF16), b.astype(BF16), (((1,), (1,)), ((), ())), preferred_element_type=F32)


def _bdot_tn(a, b):
    return lax.dot_general(a.astype(BF16), b.astype(BF16), (((0,), (0,)), ((), ())), preferred_element_type=F32)


def _softplus(y):
    return jnp.maximum(y, 0.0) + jnp.log(1.0 + jnp.exp(-jnp.abs(y)))


def _sigmoid(y):
    return 1.0 / (1.0 + jnp.exp(-y))


def _cumsum_rows(x, block):
    pos = lax.broadcasted_iota(jnp.int32, x.shape, 0) % block
    d = 1
    while d < block:
        x = x + jnp.where(pos >= d, pltpu.roll(x, d, axis=0), 0.0)
        d *= 2
    return x


def _suffix_sum_rows(x, block):
    n = x.shape[0]
    pos = lax.broadcasted_iota(jnp.int32, x.shape, 0) % block
    y = x
    d = 1
    while d < block:
        y = y + jnp.where(pos + d < block, pltpu.roll(y, n - d, axis=0), 0.0)
        d *= 2
    return y - x


def _alias_inputs(bufs):
    arrays, specs, alias = [], [], {}
    for out_idx, buf in enumerate(bufs):
        if buf is not None:
            alias[len(arrays)] = out_idx
            arrays.append(buf)
            specs.append(pl.BlockSpec(memory_space=pl.ANY))
    return arrays, specs, alias


def _rms_rows(x_ref, g_ref, h_ref, rows, eps):
    n = x_ref.shape[0] // rows

    def body(r, c):
        sl = pl.ds(pl.multiple_of(r * rows, rows), rows)
        x = x_ref[sl, :]
        ms = jnp.mean(x * x, axis=-1, keepdims=True)
        h_ref[sl, :] = (x * lax.rsqrt(ms + eps) * g_ref[...]).astype(h_ref.dtype)
        return c

    lax.fori_loop(0, n, body, 0)


def _norm_mm_body(x_ref, g_ref, w_ref, o_ref, h_ref, *, rows):
    @pl.when(pl.program_id(1) == 0)
    def _():
        _rms_rows(x_ref, g_ref, h_ref, rows, NORM_EPS)

    o_ref[...] = jnp.dot(h_ref[...], w_ref[...].astype(BF16), preferred_element_type=F32).astype(o_ref.dtype)


def _norm_mm(x, g3, w3, l, *, tm, tn, name):
    m, d = x.shape
    n = w3.shape[-1]
    return pl.pallas_call(
        functools.partial(_norm_mm_body, rows=_pick(tm, 128, SUBLANES)),
        grid=(m // tm, n // tn),
        in_specs=[pl.BlockSpec((tm, d), lambda i, j: (i, 0)),
                  pl.BlockSpec((None, 1, d), lambda i, j: (l, 0, 0)),
                  pl.BlockSpec((None, d, tn), lambda i, j: (l, 0, j))],
        out_specs=pl.BlockSpec((tm, tn), lambda i, j: (i, j)),
        out_shape=jax.ShapeDtypeStruct((m, n), F32),
        scratch_shapes=[pltpu.VMEM((tm, d), BF16)],
        compiler_params=_cp(("parallel", "arbitrary")),
        name=name,
    )(x, g3, w3)


def _in_proj_body(x_ref, g_ref, wm_ref, ws_ref, o_ref, og_ref, h_ref, *, rows, n_main, n_small):
    j = pl.program_id(1)

    @pl.when(j == 0)
    def _():
        _rms_rows(x_ref, g_ref, h_ref, rows, NORM_EPS)

    @pl.when(j < n_main)
    def _():
        o_ref[...] = _bdot_nt(h_ref[...], wm_ref[0])

    @pl.when((j >= n_main) & (j < n_main + n_small))
    def _():
        o_ref[...] = _bdot_nt(h_ref[...], ws_ref[...])

    @pl.when(j >= n_main + n_small)
    def _():
        og_ref[...] = _sigmoid(_bdot_nt(h_ref[...], wm_ref[0])).astype(og_ref.dtype)


def _in_proj(x, g3, w_t, w_small_t, l, *, tm):
    m, d = x.shape
    tn = IN_TILE
    n_main, n_small = C_SMALL // tn, SMALL_W // tn
    xq0 = GLA_COLS + RW_COLS
    n_gla, n_rw = C_RR // tn, (C_XQ - C_RR) // tn
    n_gate = (w_t.shape[1] - xq0 - X_W) // tn

    def w_rows(i, j):
        jm = jnp.where(j < n_main, j, jnp.maximum(j - n_small, n_main - 1))
        start = jnp.where(jm < n_gla, jm * tn,
                          jnp.where(jm < n_gla + n_rw, GLA_COLS + (jm - n_gla) * tn, xq0 + (jm - n_gla - n_rw) * tn))
        return (l, pl.multiple_of(start, 2 * SUBLANES), 0)

    return pl.pallas_call(
        functools.partial(_in_proj_body, rows=_pick(tm, 128, SUBLANES), n_main=n_main, n_small=n_small),
        grid=(m // tm, n_main + n_small + n_gate),
        in_specs=[pl.BlockSpec((tm, d), lambda i, j: (i, 0), pipeline_mode=pl.Buffered(1)),
                  pl.BlockSpec((None, 1, d), lambda i, j: (l, 0, 0)),
                  pl.BlockSpec((pl.Element(1), pl.Element(tn), pl.Element(d)), w_rows),
                  pl.BlockSpec((None, tn, d), lambda i, j: (l, jnp.clip(j - n_main, 0, n_small - 1), 0),
                               pipeline_mode=pl.Buffered(1))],
        out_specs=[pl.BlockSpec((tm, tn), lambda i, j: (i, jnp.minimum(j, n_main + n_small - 1))),
                   pl.BlockSpec((tm, tn), lambda i, j: (i, jnp.maximum(j - n_main - n_small, 0)))],
        out_shape=[jax.ShapeDtypeStruct((m, (n_main + n_small) * tn), F32),
                   jax.ShapeDtypeStruct((m, n_gate * tn), BF16)],
        scratch_shapes=[pltpu.VMEM((tm, d), BF16)],
        compiler_params=_cp(("parallel", "arbitrary")),
        name="in_proj",
    )(x, g3, w_t, w_small_t)


def _merge_body(og_ref, or_ref, ox_ref, gg_ref, gr_ref, gx_ref, w_ref, o_ref):
    acc = gg_ref[...].astype(F32) * _bdot(og_ref[...], w_ref[0])
    acc += gr_ref[...].astype(F32) * _bdot(or_ref[...], w_ref[1])
    acc += gx_ref[...].astype(F32) * _bdot(ox_ref[...], w_ref[2])
    o_ref[...] = acc.astype(o_ref.dtype)


def _merge(o_gla, o_rw, o_x, gates, w_branch, l, *, tm, tn):
    m, bw = o_gla.shape
    d = w_branch.shape[-1]
    gate_blk = [b * d // tn for b in range(N_BRANCH)]
    o_spec = pl.BlockSpec((tm, bw), lambda i, j: (i, 0))
    return pl.pallas_call(
        _merge_body,
        grid=(m // tm, d // tn),
        in_specs=[o_spec, o_spec, o_spec]
        + [pl.BlockSpec((tm, tn), functools.partial(lambda i, j, off: (i, off + j), off=gate_blk[b]))
           for b in range(N_BRANCH)]
        + [pl.BlockSpec((None, N_BRANCH, bw, tn), lambda i, j: (l, 0, 0, j))],
        out_specs=pl.BlockSpec((tm, tn), lambda i, j: (i, j)),
        out_shape=jax.ShapeDtypeStruct((m, d), BF16),
        compiler_params=_cp(("parallel", "arbitrary")),
        name="branch_merge",
    )(o_gla, o_rw, o_x, gates, gates, gates, w_branch)


def _mm_res_body(a_ref, w_ref, r_ref, o_ref):
    o_ref[...] = r_ref[...] + _bdot(a_ref[...], w_ref[...])


def _mm_res(a, w3, l, res, *, tm, tn, name):
    m, kd = a.shape
    n = w3.shape[-1]
    return pl.pallas_call(
        _mm_res_body,
        grid=(m // tm, n // tn),
        in_specs=[pl.BlockSpec((tm, kd), lambda i, j: (i, 0)),
                  pl.BlockSpec((None, kd, tn), lambda i, j: (l, 0, j)),
                  pl.BlockSpec((tm, tn), lambda i, j: (i, j))],
        out_specs=pl.BlockSpec((tm, tn), lambda i, j: (i, j)),
        out_shape=jax.ShapeDtypeStruct((m, n), F32),
        compiler_params=_cp(("parallel", "arbitrary")),
        name=name,
    )(a, w3, res)


def _ffn_up_body(x_ref, g_ref, wg_ref, wu_ref, o_ref, h_ref, *, rows):
    @pl.when(pl.program_id(1) == 0)
    def _():
        _rms_rows(x_ref, g_ref, h_ref, rows, NORM_EPS)

    h = h_ref[...]
    a = jnp.dot(h, wg_ref[...].astype(BF16), preferred_element_type=F32)
    u = jnp.dot(h, wu_ref[...].astype(BF16), preferred_element_type=F32)
    o_ref[...] = (a * _sigmoid(a) * u).astype(o_ref.dtype)


def _ffn_up(x, g3, wg, wu, l, *, tm, tn):
    m, d = x.shape
    f = wg.shape[-1]
    w_spec = pl.BlockSpec((None, d, tn), lambda i, j: (l, 0, j))
    return pl.pallas_call(
        functools.partial(_ffn_up_body, rows=_pick(tm, 128, SUBLANES)),
        grid=(m // tm, f // tn),
        in_specs=[pl.BlockSpec((tm, d), lambda i, j: (i, 0)),
                  pl.BlockSpec((None, 1, d), lambda i, j: (l, 0, 0)),
                  w_spec, w_spec],
        out_specs=pl.BlockSpec((tm, tn), lambda i, j: (i, j)),
        out_shape=jax.ShapeDtypeStruct((m, f), BF16),
        scratch_shapes=[pltpu.VMEM((tm, d), BF16)],
        compiler_params=_cp(("parallel", "arbitrary")),
        name="ffn_up",
    )(x, g3, wg, wu)


def _final_norm_body(x_ref, g_ref, o_ref):
    x = x_ref[...]
    ms = jnp.mean(x * x, axis=-1, keepdims=True)
    o_ref[...] = x * lax.rsqrt(ms + NORM_EPS) * g_ref[...]


def _final_norm(x, g2, r0, rows):
    d = x.shape[1]
    tm = _pick(rows, 256, SUBLANES)
    assert r0 % tm == 0
    return pl.pallas_call(
        _final_norm_body,
        grid=(rows // tm,),
        in_specs=[pl.BlockSpec((tm, d), lambda i: (r0 // tm + i, 0)), pl.BlockSpec((1, d), lambda i: (0, 0))],
        out_specs=pl.BlockSpec((tm, d), lambda i: (i, 0)),
        out_shape=jax.ShapeDtypeStruct((rows, d), F32),
        compiler_params=_cp(("parallel",)),
        name="final_norm",
    )(x, g2)


def _gla_body(*refs, names, bb, ch, zero_init):
    r = dict(zip(names, refs))
    s_ref = r["s_out"]

    @pl.when(pl.program_id(1) == 0)
    def _():
        if zero_init:
            s_ref[...] = jnp.zeros(s_ref.shape, F32)
        else:
            s_ref[...] = r["s0"][...]

    grp = max(1, SUBLANES // ch)
    ur = grp * ch
    row = lax.broadcasted_iota(jnp.int32, (ur, ur), 0)
    col = lax.broadcasted_iota(jnp.int32, (ur, ur), 1)
    causal = (col <= row) & ((row // ch) == (col // ch))
    useq = lax.broadcasted_iota(jnp.int32, (ur, 1), 0) // ch
    z = _bdot(r["ga"][...], r["wa2"][...]) + r["ba2"][...]
    la = -_softplus(-z) * (1.0 / GLA_GATE_NORM)
    kk = r["k"][...]
    bc = _cumsum_rows(la, ch)
    tail = _suffix_sum_rows(la, ch)
    qe = r["q"][...] * (GLA_DK ** -0.5) * jnp.exp(bc)
    ke = kk * jnp.exp(-bc)
    kl = kk * jnp.exp(tail)
    dec_rows = jnp.exp(bc + tail)

    units = [(u, h) for u in range(bb // grp) for h in range(GLA_HEADS)]

    def sl(u, h):
        return (slice(u * ur, (u + 1) * ur), slice(h * GLA_DK, (h + 1) * GLA_DK),
                slice(h * GLA_DV, (h + 1) * GLA_DV))

    def own(q, x):
        return x if grp == 1 else jnp.where(useq == q, x, 0.0)

    att, o_int, upd = {}, {}, {}
    for u, h in units:
        rows, ks, vs = sl(u, h)
        att[u, h] = jnp.where(causal, _bdot_nt(qe[rows, ks], ke[rows, ks]), 0.0)
        o_int[u, h] = sum(own(q, _bdot(qe[rows, ks], s_ref[u * grp + q, h])) for q in range(grp))
    for u, h in units:
        rows, ks, vs = sl(u, h)
        vh = r["v"][rows, vs]
        o = o_int[u, h] + _bdot(att[u, h], vh)
        for q in range(grp):
            upd[u * grp + q, h] = _bdot_tn(own(q, kl[rows, ks]), vh)
        ms = jnp.mean(o * o, axis=-1, keepdims=True)
        gh = r["g"][rows, vs]
        r["of"][rows, vs] = o * lax.rsqrt(ms + NORM_EPS) * r["on"][...] * (gh * _sigmoid(gh))
    for s in range(bb):
        for h in range(GLA_HEADS):
            ks = slice(h * GLA_DK, (h + 1) * GLA_DK)
            dec_col = jnp.broadcast_to(dec_rows[s * ch:s * ch + 1, ks], (GLA_DK, GLA_DK)).T
            dec = jnp.concatenate([dec_col] * (GLA_DV // GLA_DK), axis=1)
            s_ref[s, h] = s_ref[s, h] * dec + upd[s, h]
    r["o"][...] = r["of"][...].astype(r["o"].dtype)


def _gla(proj, wa2p, ba2, onorm, s0, o_buf, st_buf, l, *, depth, m_total, r0, nseq, t, ch, bb):
    rb = bb * ch
    nc = t // ch
    assert (bb == 1 or nc == 1) and r0 % rb == 0

    def rblk(ib, c):
        return r0 // rb + ib * nc + c

    def sec(width, off):
        return pl.BlockSpec((rb, width), lambda ib, c: (rblk(ib, c), off // width))

    st_spec = pl.BlockSpec((None, bb, GLA_HEADS, GLA_DK, GLA_DV), lambda ib, c: (l, ib, 0, 0, 0))
    names = ["q", "k", "v", "g", "ga", "wa2", "ba2", "on"]
    args = [proj, proj, proj, proj, proj, wa2p, ba2, onorm]
    in_specs = [sec(GLA_KW, C_GQ), sec(GLA_KW, C_GK), sec(GLA_VW, C_GV), sec(GLA_VW, C_GG),
                sec(LANES, proj.shape[1] - SMALL_W + S_GA),
                pl.BlockSpec((None, LANES, GLA_KW), lambda ib, c: (l, 0, 0)),
                pl.BlockSpec((None, 1, GLA_KW), lambda ib, c: (l, 0, 0)),
                pl.BlockSpec((None, 1, GLA_DV), lambda ib, c: (l, 0, 0))]
    if s0 is not None:
        names.append("s0")
        args.append(s0)
        in_specs.append(st_spec)
    al_arrays, al_specs, alias = _alias_inputs([o_buf, st_buf])
    alias = {len(args) + k: v for k, v in alias.items()}
    names += ["alias%d" % i for i in range(len(al_arrays))] + ["o", "s_out", "of"]
    return pl.pallas_call(
        functools.partial(_gla_body, names=names, bb=bb, ch=ch, zero_init=s0 is None),
        grid=(nseq // bb, nc),
        in_specs=in_specs + al_specs,
        out_specs=[pl.BlockSpec((rb, GLA_VW), lambda ib, c: (rblk(ib, c), 0)), st_spec],
        out_shape=[jax.ShapeDtypeStruct((m_total, GLA_VW), BF16),
                   jax.ShapeDtypeStruct((depth, nseq, GLA_HEADS, GLA_DK, GLA_DV), F32)],
        scratch_shapes=[pltpu.VMEM((rb, GLA_VW), F32)],
        input_output_aliases=alias,
        compiler_params=_cp(("parallel", "arbitrary")),
        name="gla",
    )(*args, *al_arrays)


def _rwkv_body(*refs, names, bb, ch, first, zero_init):
    r = dict(zip(names, refs))
    st_ref, carry_ref = r["st"], r["carry"]
    chained = bb == 1
    nr = bb * ch
    n2 = 2 * nr
    npair = RW_HEADS // 2
    c_id = pl.program_id(1)
    last_chunk = c_id == pl.num_programs(1) - 1
    lane = lax.broadcasted_iota(jnp.int32, (nr, LANES), 1)
    head_a = lane < RW_N
    r2 = lax.broadcasted_iota(jnp.int32, (n2, n2), 0)
    c2 = lax.broadcasted_iota(jnp.int32, (n2, n2), 1)
    same_blk = (r2 // ch) == (c2 // ch)
    strict = same_blk & (c2 < r2)
    incl = same_blk & (c2 <= r2)
    eye = (r2 == c2).astype(F32)
    blockdiag = (lax.broadcasted_iota(jnp.int32, (LANES, LANES), 0) < RW_N) == \
                (lax.broadcasted_iota(jnp.int32, (LANES, LANES), 1) < RW_N)
    zeros64 = jnp.zeros((RW_N, RW_N), F32)

    @pl.when(c_id == 0)
    def _():
        if chained:
            carry_ref[...] = r["sh"][...]
        for s in range(bb):
            for p in range(npair):
                if zero_init:
                    st_ref[s, p] = jnp.zeros((LANES, LANES), F32)
                else:
                    top = jnp.concatenate([r["s0"][s, 2 * p], zeros64], axis=1)
                    bot = jnp.concatenate([zeros64, r["s0"][s, 2 * p + 1]], axis=1)
                    st_ref[s, p] = jnp.concatenate([top, bot], axis=0)

    rowpos = lax.broadcasted_iota(jnp.int32, (nr, RW_W), 0) % ch
    sections = (("pr", 0), ("pk", RW_W), ("pv", 2 * RW_W), ("ps", 3 * RW_W))

    def shifted(name, off):
        p = r[name][...]
        prev = pltpu.roll(p, 1, axis=0)
        if chained:
            first_rows = jnp.broadcast_to(carry_ref[:, off:off + RW_W], (ch, RW_W))
        else:
            first_rows = r["sh"][:, off:off + RW_W]
        prev = jnp.where(rowpos == 0, first_rows, prev)
        return p, p + r["mu"][:, off:off + RW_W] * (prev - p)

    raw, feat = {}, {}
    for name, off in sections:
        raw[name], feat[name] = shifted(name, off)
    if chained:
        for name, off in sections:
            carry_ref[:, off:off + RW_W] = raw[name][ch - 1:ch, :]

    @pl.when(last_chunk)
    def _():
        for name, off in sections:
            for s in range(bb):
                last = s * ch + ch - 1
                r["sh_out"][s, :, off:off + RW_W] = raw[name][last:last + 1, :]

    f_r, f_k, f_v, f_s = feat["pr"], feat["pk"], feat["pv"], feat["ps"]
    xw = f_s[:, S_XW:S_XW + LANES]
    xa = f_s[:, S_XA:S_XA + LANES]
    xg = f_s[:, S_XG:S_XG + 2 * LANES]
    w_log = -_softplus(-(r["w0"][...] + _bdot(jnp.tanh(xw), r["w2"][...]))) - 0.5
    lw = -jnp.exp(w_log)
    a = _sigmoid(r["a0"][...] + _bdot(xa, r["a2"][...]))
    gate = _bdot(_sigmoid(xg), r["g2"][...])
    if first:
        vr = f_v
        r["vf_out"][...] = vr
    else:
        nu = _sigmoid(r["v0"][...] + _bdot(raw["ps"][:, S_V1:S_V1 + LANES], r["v2"][...]))
        vr = f_v + (r["vf"][...] - f_v) * nu
    kk = f_k * r["kkw"][...]
    kmod = f_k * (1.0 + (a - 1.0) * r["kaw"][...])
    g_in = _cumsum_rows(lw, ch)
    g_ex = g_in - lw
    g_tail = _suffix_sum_rows(lw, ch)
    e_in, e_ex, e_neg, e_tail = jnp.exp(g_in), jnp.exp(g_ex), jnp.exp(-g_in), jnp.exp(g_tail)
    e_end = jnp.exp(g_in + g_tail)

    def stack(x):
        return jnp.concatenate([jnp.where(head_a, x, 0.0), jnp.where(head_a, 0.0, x)], axis=0)

    def unstack(x):
        return x[:nr] + x[nr:]

    def pair_sum(x):
        sa = jnp.sum(jnp.where(head_a, x, 0.0), axis=-1, keepdims=True)
        sb = jnp.sum(jnp.where(head_a, 0.0, x), axis=-1, keepdims=True)
        return jnp.where(head_a, sa, sb)

    def seq_rows(s):
        return slice(s * ch, (s + 1) * ch)

    for p0 in range(0, npair, RW_PAIR_GROUP):
        group = range(p0, p0 + RW_PAIR_GROUP)
        q = {p: {} for p in group}
        for p in group:
            ls = slice(p * LANES, (p + 1) * LANES)
            d = q[p]
            kk_p = kk[:, ls]
            kap = kk_p / jnp.maximum(jnp.sqrt(pair_sum(kk_p * kk_p)), 1e-12)
            d["r"], d["k"], d["v"] = f_r[:, ls], kmod[:, ls], vr[:, ls]
            d["kap_t"] = kap * e_ex[:, ls]
            d["r_t"] = d["r"] * e_in[:, ls]
            ba = kap * a[:, ls]
            d["b_h"] = ba * e_tail[:, ls]
            d["k_h"] = d["k"] * e_tail[:, ls]
            lhs4 = jnp.concatenate([stack(d["kap_t"]), stack(d["r_t"])], axis=0)
            rhs4 = jnp.concatenate([stack(ba * e_neg[:, ls]), stack(d["k"] * e_neg[:, ls])], axis=0)
            m4 = _bdot_nt(lhs4, rhs4)
            d["n"] = jnp.where(strict, m4[:n2, :n2], 0.0)
            d["a_ak"] = jnp.where(strict, m4[:n2, n2:], 0.0)
            d["a_r"] = jnp.concatenate([jnp.where(incl, m4[n2:, :n2], 0.0), jnp.where(incl, m4[n2:, n2:], 0.0)], axis=1)
            d["dinv"] = eye
            d["v_st"] = stack(d["v"])
        for p in group:
            d = q[p]
            ks_parts, rs_parts = [], []
            for s in range(bb):
                both = _bdot_nt(jnp.concatenate([d["kap_t"][seq_rows(s)], d["r_t"][seq_rows(s)]], axis=0), st_ref[s, p])
                ks_parts.append(both[:ch])
                rs_parts.append(both[ch:])
            d["ks"] = ks_parts[0] if bb == 1 else jnp.concatenate(ks_parts, axis=0)
            d["rs"] = rs_parts[0] if bb == 1 else jnp.concatenate(rs_parts, axis=0)
            d["rhs"] = stack(d["ks"]) + _bdot(d["a_ak"], d["v_st"])
        m = 1
        while m < ch:
            low = same_blk & ((r2 // (2 * m)) == (c2 // (2 * m))) & (((r2 // m) % 2) == 1) & (((c2 // m) % 2) == 0)
            tmp = {p: _bdot(q[p]["dinv"], jnp.where(low, q[p]["n"], 0.0)) for p in group}
            for p in group:
                q[p]["dinv"] = q[p]["dinv"] - _bdot(tmp[p], q[p]["dinv"])
            m *= 2
        for p in group:
            d = q[p]
            d["u_st"] = -_bdot(d["dinv"], d["rhs"])
        for p in group:
            d = q[p]
            ls = slice(p * LANES, (p + 1) * LANES)
            y = unstack(stack(d["rs"]) + _bdot(d["a_r"], jnp.concatenate([d["u_st"], d["v_st"]], axis=0)))
            u_p = unstack(d["u_st"])
            for s in range(bb):
                rows = seq_rows(s)
                upd = _bdot_tn(jnp.concatenate([u_p[rows], d["v"][rows]], axis=0),
                               jnp.concatenate([d["b_h"][rows], d["k_h"][rows]], axis=0))
                st_ref[s, p] = st_ref[s, p] * e_end[s * ch:s * ch + 1, ls] + jnp.where(blockdiag, upd, 0.0)
            mean = pair_sum(y) * (1.0 / RW_N)
            yc = y - mean
            var = pair_sum(yc * yc) * (1.0 / RW_N)
            yn = yc * lax.rsqrt(var + RW_GN_EPS) * r["lnw"][:, ls] + r["lnb"][:, ls]
            bonus = pair_sum(d["r"] * d["k"] * r["rk"][:, ls]) * d["v"]
            r["o"][:, ls] = ((yn + bonus) * gate[:, ls]).astype(r["o"].dtype)

    @pl.when(last_chunk)
    def _():
        for s in range(bb):
            for p in range(npair):
                st = st_ref[s, p]
                r["s_out"][s, 2 * p] = st[:RW_N, :RW_N]
                r["s_out"][s, 2 * p + 1] = st[RW_N:, RW_N:]


def _rwkv(proj, vf, shift, s0, o_buf, st_buf, wts, l, *, depth, m_total, r0, nseq, t, ch, bb):
    first = vf is None
    rb = bb * ch
    nc = t // ch
    assert rb == RW_SLAB and (bb == 1 or nc == 1) and r0 % rb == 0
    small_off = proj.shape[1] - SMALL_W

    def rblk(ib, c):
        return r0 // rb + ib * nc + c

    def sec(off):
        return pl.BlockSpec((rb, RW_W), lambda ib, c: (rblk(ib, c), off // RW_W))

    def vec(width, dl=0):
        return pl.BlockSpec((None, 1, width), lambda ib, c: (l - dl, 0, 0))

    def mat(rows, dl=0):
        return pl.BlockSpec((None, rows, RW_W), lambda ib, c: (l - dl, 0, 0))

    group_rows = pl.BlockSpec((rb, RW_W), lambda ib, c: (ib * nc + c, 0))
    st_spec = pl.BlockSpec((None, bb, RW_HEADS, RW_N, RW_N), lambda ib, c: (l, ib, 0, 0, 0))
    sh_spec = pl.BlockSpec((bb, 1, SHIFT_W), lambda ib, c: (ib, 0, 0))
    names = ["pr", "pk", "pv", "ps", "mu", "sh", "w0", "w2", "a0", "a2", "g2", "kkw", "kaw", "rk", "lnw", "lnb"]
    args = [proj, proj, proj, proj, wts["mu"], shift, wts["w0"], wts["w2"], wts["a0"], wts["a2"], wts["g2"],
            wts["kk"], wts["ka"], wts["rk"], wts["lnw"], wts["lnb"]]
    sh_in = (pl.BlockSpec((None, 1, SHIFT_W), lambda ib, c: (ib, 0, 0)) if bb == 1
             else pl.BlockSpec((rb, SHIFT_W), lambda ib, c: (ib, 0)))
    in_specs = [sec(C_RR), sec(C_RK), sec(C_RV), sec(small_off), vec(SHIFT_W), sh_in,
                vec(RW_W), mat(LANES), vec(RW_W), mat(LANES), mat(2 * LANES)] + [vec(RW_W)] * 5
    if not first:
        names += ["vf", "v0", "v2"]
        args += [vf, wts["v0"], wts["v2"]]
        in_specs += [group_rows, vec(RW_W, 1), mat(LANES, 1)]
    if s0 is not None:
        names.append("s0")
        args.append(s0)
        in_specs.append(st_spec)
    al_arrays, al_specs, alias = _alias_inputs([o_buf, st_buf])
    alias = {len(args) + k: v for k, v in alias.items()}
    names += ["alias%d" % i for i in range(len(al_arrays))] + ["o", "s_out", "sh_out"]
    out_specs = [pl.BlockSpec((rb, RW_W), lambda ib, c: (rblk(ib, c), 0)), st_spec, sh_spec]
    out_shape = [jax.ShapeDtypeStruct((m_total, RW_W), BF16),
                 jax.ShapeDtypeStruct((depth, nseq, RW_HEADS, RW_N, RW_N), F32),
                 jax.ShapeDtypeStruct((nseq, 1, SHIFT_W), F32)]
    if first:
        names.append("vf_out")
        out_specs.append(group_rows)
        out_shape.append(jax.ShapeDtypeStruct((nseq * t, RW_W), F32))
    names += ["st", "carry"]
    res = pl.pallas_call(
        functools.partial(_rwkv_body, names=names, bb=bb, ch=ch, first=first,
                          zero_init=s0 is None),
        grid=(nseq // bb, nc),
        in_specs=in_specs + al_specs,
        out_specs=out_specs,
        out_shape=out_shape,
        scratch_shapes=[pltpu.VMEM((bb, RW_HEADS // 2, LANES, LANES), F32), pltpu.VMEM((1, SHIFT_W), F32)],
        input_output_aliases=alias,
        compiler_params=_cp(("parallel", "arbitrary")),
        name="rwkv7",
    )(*args, *al_arrays)
    return res[0], res[1], res[2], (res[3] if first else None)


def _xattn_body(*refs, names, bb, tq):
    r = dict(zip(names, refs))

    def kv(ref, s, h):
        return ref[s, :, h * X_DH:(h + 1) * X_DH]

    units = [(s, h) for s in range(bb) for h in range(X_HEADS)]
    pr = {}
    for s, h in units:
        sc = _bdot_nt(r["q"][s * tq:(s + 1) * tq, h * X_DH:(h + 1) * X_DH], kv(r["k"], s, h)) * (X_DH ** -0.5)
        e = jnp.exp(sc - jnp.max(sc, axis=-1, keepdims=True))
        pr[s, h] = e / jnp.sum(e, axis=-1, keepdims=True)
    for s, h in units:
        r["of"][s * tq:(s + 1) * tq, h * X_DH:(h + 1) * X_DH] = _bdot(pr[s, h], kv(r["v"], s, h))
    r["o"][...] = r["of"][...].astype(r["o"].dtype)


def _xattn(proj, mem_k, mem_v, o_buf, l, *, m_total, r0, nseq, t, tq, bb):
    rb = bb * tq
    nq = t // tq
    assert (bb == 1 or nq == 1) and r0 % rb == 0
    kv_spec = pl.BlockSpec((None, bb, MEM_LEN, X_W), lambda ib, c: (l, ib, 0, 0))
    rows = lambda ib, c: r0 // rb + ib * nq + c
    al_arrays, al_specs, alias = _alias_inputs([o_buf])
    alias = {3 + k: v for k, v in alias.items()}
    names = ["q", "k", "v"] + ["alias%d" % i for i in range(len(al_arrays))] + ["o", "of"]
    return pl.pallas_call(
        functools.partial(_xattn_body, names=names, bb=bb, tq=tq),
        grid=(nseq // bb, nq),
        in_specs=[pl.BlockSpec((rb, X_W), lambda ib, c: (rows(ib, c), C_XQ // X_W)), kv_spec, kv_spec] + al_specs,
        out_specs=pl.BlockSpec((rb, X_W), lambda ib, c: (rows(ib, c), 0)),
        out_shape=jax.ShapeDtypeStruct((m_total, X_W), BF16),
        scratch_shapes=[pltpu.VMEM((rb, X_W), F32)],
        input_output_aliases=alias,
        compiler_params=_cp(("parallel", "arbitrary")),
        name="mem_xattn",
    )(proj, mem_k, mem_v, *al_arrays)


def _xattn_cache_body(q_ref, k_hbm, v_hbm, alias_ref, o_ref, of_ref, kbuf, vbuf, sem, *, l, bb, tq):
    del alias_ref
    i = pl.program_id(0)
    n = pl.num_programs(0)

    def copies(step, slot):
        out = []
        for h in range(X_HEADS):
            for t, (src, dst) in enumerate(((k_hbm, kbuf), (v_hbm, vbuf))):
                out.append(pltpu.make_async_copy(src.at[l, pl.ds(step * bb, bb), :, h, :], dst.at[slot, h],
                                                 sem.at[slot, t, h]))
        return out

    @pl.when(i == 0)
    def _():
        for c in copies(0, 0):
            c.start()

    @pl.when(i + 1 < n)
    def _():
        for c in copies(i + 1, (i + 1) % 2):
            c.start()

    slot = i % 2
    for c in copies(i, slot):
        c.wait()

    grp = max(1, SUBLANES // tq)
    ur = grp * tq
    useq = lax.broadcasted_iota(jnp.int32, (ur, 1), 0) // tq
    units = [(u, h) for u in range(bb // grp) for h in range(X_HEADS)]
    pr = {}
    for u, h in units:
        qu = q_ref[u * ur:(u + 1) * ur, h * X_DH:(h + 1) * X_DH]
        for g in range(grp):
            sc = _bdot_nt(qu, kbuf[slot, h, u * grp + g]) * (X_DH ** -0.5)
            e = jnp.exp(sc - jnp.max(sc, axis=-1, keepdims=True))
            pr[u, h, g] = e / jnp.sum(e, axis=-1, keepdims=True)
    for u, h in units:
        o = _bdot(pr[u, h, 0], vbuf[slot, h, u * grp])
        for g in range(1, grp):
            o = jnp.where(useq == g, _bdot(pr[u, h, g], vbuf[slot, h, u * grp + g]), o)
        of_ref[u * ur:(u + 1) * ur, h * X_DH:(h + 1) * X_DH] = o
    o_ref[...] = of_ref[...].astype(o_ref.dtype)


def _xattn_cache(proj, cache_k, cache_v, o_buf, l, *, r0, nseq, tq, bb):
    rb = bb * tq
    assert r0 % rb == 0 and nseq % bb == 0
    rows = lambda ib: (r0 // rb + ib, 0)
    any_spec = pl.BlockSpec(memory_space=pl.ANY)
    buf = pltpu.VMEM((2, X_HEADS, bb, MEM_LEN, X_DH), F32)
    return pl.pallas_call(
        functools.partial(_xattn_cache_body, l=l, bb=bb, tq=tq),
        grid=(nseq // bb,),
        in_specs=[pl.BlockSpec((rb, X_W), lambda ib: (r0 // rb + ib, C_XQ // X_W)), any_spec, any_spec, any_spec],
        out_specs=pl.BlockSpec((rb, X_W), rows),
        out_shape=jax.ShapeDtypeStruct(o_buf.shape, o_buf.dtype),
        scratch_shapes=[pltpu.VMEM((rb, X_W), F32), buf, buf, pltpu.SemaphoreType.DMA((2, 2, X_HEADS))],
        input_output_aliases={3: 0},
        compiler_params=_cp(("arbitrary",)),
        name="cache_xattn",
    )(proj, cache_k, cache_v, o_buf)


def _pad_last(a, width):
    return jnp.pad(a, [(0, 0)] * (a.ndim - 1) + [(0, width - a.shape[-1])])


def _pad_rows(a, rows):
    return jnp.pad(a, [(0, 0)] * (a.ndim - 2) + [(0, rows - a.shape[-2]), (0, 0)])


def _small_section(ga, xw, xa, xg, v1):
    return jnp.concatenate([_pad_last(ga, S_XW - S_GA), _pad_last(xw, S_XA - S_XW), _pad_last(xa, S_XG - S_XA),
                            _pad_last(xg, S_V1 - S_XG), _pad_last(v1, SMALL_W - S_V1)], axis=-1)


def _small_weight_t(w_t, v1_t):
    sm = GLA_COLS + 3 * RW_W
    xq = GLA_COLS + RW_COLS
    pieces = ((S_GA, w_t[:, 3072:GLA_COLS]), (S_XW, w_t[:, sm:sm + RW_DECAY_R]),
              (S_XA, w_t[:, sm + RW_DECAY_R:sm + RW_DECAY_R + RW_A_R]),
              (S_XG, w_t[:, sm + RW_DECAY_R + RW_A_R:xq]), (S_V1, v1_t))
    rows = lax.optimization_barrier(tuple(p for _, p in pieces))
    out = jnp.zeros((w_t.shape[0], SMALL_W, w_t.shape[2]), BF16)
    for (off, _), piece in zip(pieces, rows):
        out = lax.dynamic_update_slice(out, piece.astype(BF16), (0, off, 0))
    return out


def _shift_layout(a):
    z16 = jnp.zeros(a.shape[:-1] + (GLA_RANK,), a.dtype)
    z32 = jnp.zeros(a.shape[:-1] + (RW_V_R,), a.dtype)
    o = 3 * RW_W
    small = _small_section(z16, a[..., o:o + RW_DECAY_R], a[..., o + RW_DECAY_R:o + RW_DECAY_R + RW_A_R],
                           a[..., o + RW_DECAY_R + RW_A_R:], z32)
    return jnp.concatenate([a[..., :o], small], axis=-1)


def _shift_unlayout(a):
    o = 3 * RW_W
    return jnp.concatenate([a[..., :o], a[..., o + S_XW:o + S_XW + RW_DECAY_R], a[..., o + S_XA:o + S_XA + RW_A_R],
                            a[..., o + S_XG:o + S_XG + RW_G_R]], axis=-1)


def kernel(x_prompt, x_sample, mem_prompt, state_gla, state_rwkv, state_rwkv_shift, cache_mem_k, cache_mem_v,
           g_norm1, w_in, gla_wa2, gla_ba2, gla_onorm, rw_mu, rw_w0, rw_w2, rw_a0, rw_a2, rw_g2, rw_kk, rw_ka,
           rw_rk, rw_lnx_w, rw_lnx_b, rw_v0, rw_v1, rw_v2, g_mem, w_mk, w_mv, w_branch, w_out, g_norm2,
           w_ff_gate, w_ff_up, w_ff_down, g_final):
    depth = w_in.shape[0]
    bp, tp, d = x_prompt.shape
    bs, ts, _ = x_sample.shape
    assert d % 1024 == 0 and ts in (1, 2, 4, 8) and tp % RW_SLAB == 0 and bs % (RW_SLAB // ts) == 0
    mp, ms = bp * tp, bs * ts
    m = mp + ms
    f = w_ff_gate.shape[-1]

    v1_all = jnp.concatenate([jnp.zeros((1, d, RW_V_R), F32), rw_v1], axis=0)
    w_t = jnp.swapaxes(w_in, 1, 2)
    w_small_t = _small_weight_t(w_t, jnp.swapaxes(v1_all, 1, 2)).astype(BF16)
    r3 = lambda a: a.reshape(a.shape[0], 1, -1)
    rw_wts = dict(mu=r3(_shift_layout(rw_mu)), w0=r3(rw_w0), w2=_pad_rows(rw_w2, LANES), a0=r3(rw_a0),
                  a2=_pad_rows(rw_a2, LANES), g2=_pad_rows(rw_g2, 2 * LANES), kk=r3(rw_kk), ka=r3(rw_ka),
                  rk=r3(rw_rk), lnw=r3(rw_lnx_w), lnb=r3(rw_lnx_b), v0=r3(rw_v0), v2=_pad_rows(rw_v2, LANES))
    wa2p = _pad_rows(gla_wa2, LANES)
    ba2, onorm = r3(gla_ba2), r3(gla_onorm)
    g1, g2n, gm = r3(g_norm1), r3(g_norm2), r3(g_mem)

    x = jnp.concatenate([x_prompt.reshape(mp, d), x_sample.reshape(ms, d)], axis=0)

    tm = _pick(m, 1152, 4 * SUBLANES)
    tn_d = _pick(d, 512, LANES)
    tn_f = _pick(f, 512, LANES)

    mem_rows = mem_prompt.reshape(bp * MEM_LEN, d)
    tmm = _pick(bp * MEM_LEN, 1024, SUBLANES)
    p_mem_k = jnp.stack([_norm_mm(mem_rows, gm, w_mk, l, tm=tmm, tn=tn_d, name="mem_k") for l in range(depth)])
    p_mem_v = jnp.stack([_norm_mm(mem_rows, gm, w_mv, l, tm=tmm, tn=tn_d, name="mem_v") for l in range(depth)])
    p_mem_k = p_mem_k.reshape(depth, bp, MEM_LEN, X_W)
    p_mem_v = p_mem_v.reshape(depth, bp, MEM_LEN, X_W)

    sh0 = jnp.zeros((bp, 1, SHIFT_W), F32)
    s_shift_in = jnp.repeat(_shift_layout(state_rwkv_shift), ts, axis=1)

    bb_s = RW_SLAB // ts
    tq_p = _pick(tp, 512, SUBLANES)
    common = dict(depth=depth, m_total=m)
    prompt = dict(r0=0, nseq=bp, t=tp)
    sample = dict(r0=mp, nseq=bs, t=ts)
    vf_p = vf_s = None
    pg_all = sg_all = pr_all = sr_all = None
    p_shift, s_shift = [], []
    for l in range(depth):
        proj, gates = _in_proj(x, g1, w_t, w_small_t, l, tm=tm)

        o_gla, pg_all = _gla(proj, wa2p, ba2, onorm, None, None, pg_all, l, **common, **prompt,
                             ch=GLA_CHUNK, bb=1)
        o_gla, sg_all = _gla(proj, wa2p, ba2, onorm, state_gla, o_gla, sg_all, l, **common, **sample,
                             ch=ts, bb=bb_s)
        o_rw, pr_all, sh_p, vf_new_p = _rwkv(proj, vf_p, sh0, None, None, pr_all, rw_wts, l, **common, **prompt,
                                             ch=RW_SLAB, bb=1)
        o_rw, sr_all, sh_s, vf_new_s = _rwkv(proj, vf_s, s_shift_in[l], state_rwkv, o_rw, sr_all, rw_wts, l,
                                             **common, **sample, ch=ts, bb=bb_s)
        if l == 0:
            vf_p, vf_s = vf_new_p, vf_new_s
        o_x = _xattn(proj, p_mem_k, p_mem_v, None, l, m_total=m, **prompt, tq=tq_p, bb=1)
        o_x = _xattn_cache(proj, cache_mem_k, cache_mem_v, o_x, l, r0=mp, nseq=bs, tq=ts, bb=XC_SEQS)

        merged = _merge(o_gla, o_rw, o_x, gates, w_branch, l, tm=tm, tn=tn_d)
        x = _mm_res(merged, w_out, l, x, tm=tm, tn=tn_d, name="out_proj")
        act = _ffn_up(x, g2n, w_ff_gate, w_ff_up, l, tm=tm, tn=tn_f)
        x = _mm_res(act, w_ff_down, l, x, tm=tm, tn=_pick(d, 256, LANES), name="ffn_down")
        p_shift.append(_shift_unlayout(sh_p[:, 0]))
        s_shift.append(_shift_unlayout(sh_s[:, 0]))

    y_prompt = _final_norm(x, g_final.reshape(1, d), 0, mp).reshape(bp, tp, d)
    y_sample = _final_norm(x, g_final.reshape(1, d), mp, ms).reshape(bs, ts, d)
    return (y_prompt, y_sample, pg_all, pr_all, jnp.stack(p_shift),
            p_mem_k.reshape(depth, bp, MEM_LEN, X_HEADS, X_DH), p_mem_v.reshape(depth, bp, MEM_LEN, X_HEADS, X_DH),
            sg_all, sr_all, jnp.stack(s_shift))
```

```python
import functools

import jax
import jax.numpy as jnp
from jax import lax
from jax.experimental import pallas as pl
from jax.experimental.pallas import tpu as pltpu

F32 = jnp.float32
BF16 = jnp.bfloat16

GLA_HEADS, GLA_DK, GLA_DV = 4, 128, 256
GLA_KW, GLA_VW, GLA_RANK = GLA_HEADS * GLA_DK, GLA_HEADS * GLA_DV, 16
GLA_GATE_NORM = 16.0
RW_HEADS, RW_N = 16, 64
RW_W = RW_HEADS * RW_N
RW_DECAY_R, RW_A_R, RW_V_R, RW_G_R = 64, 64, 32, 160
RW_GN_EPS = 64e-5
MEM_LEN, X_HEADS, X_DH = 256, 4, 256
X_W = X_HEADS * X_DH
N_BRANCH = 3
NORM_EPS = 1e-6
GLA_COLS = 2 * GLA_KW + 2 * GLA_VW + GLA_RANK
RW_COLS = 3 * RW_W + RW_DECAY_R + RW_A_R + RW_G_R

LANES = 128
SUBLANES = 8
VMEM_LIMIT_BYTES = 56 * 1024 * 1024

C_GQ, C_GK, C_GV, C_GG = 0, 512, 1024, 2048
C_RR, C_RK, C_RV, C_XQ, C_SMALL = 3072, 4096, 5120, 6144, 7168
SMALL_W = 1024
IN_TILE = 1024
S_GA, S_XW, S_XA, S_XG, S_V1 = 0, 128, 256, 384, 640
SHIFT_W = 3 * RW_W + SMALL_W

RW_SLAB = 64
XC_SEQS = 8
RW_PAIR_GROUP = 8
GLA_CHUNK = 64


def _cp(sem):
    return pltpu.CompilerParams(dimension_semantics=sem, vmem_limit_bytes=VMEM_LIMIT_BYTES)


def _pick(n, target, mult):
    best = None
    for d in range(mult, min(n, target) + 1, mult):
        if n % d == 0:
            best = d
    assert best is not None, (n, target, mult)
    return best


def _bdot(a, b):
    return jnp.dot(a.astype(BF16), b.astype(BF16), preferred_element_type=F32)


def _bdot_nt(a, b):
    return lax.dot_general(a.astype(BF16), b.astype(BF16), (((1,), (1,)), ((), ())), preferred_element_type=F32)


def _bdot_tn(a, b):
    return lax.dot_general(a.astype(BF16), b.astype(BF16), (((0,), (0,)), ((), ())), preferred_element_type=F32)


def _split3(x):
    hi = x.astype(BF16)
    r1 = x - hi.astype(F32)
    mid = r1.astype(BF16)
    lo = (r1 - mid.astype(F32)).astype(BF16)
    return hi, mid, lo


def _softplus(y):
    return jnp.maximum(y, 0.0) + jnp.log(1.0 + jnp.exp(-jnp.abs(y)))


def _sigmoid(y):
    return 1.0 / (1.0 + jnp.exp(-y))


def _cumsum_rows(x, block):
    pos = lax.broadcasted_iota(jnp.int32, x.shape, 0) % block
    d = 1
    while d < block:
        x = x + jnp.where(pos >= d, pltpu.roll(x, d, axis=0), 0.0)
        d *= 2
    return x


def _suffix_sum_rows(x, block):
    n = x.shape[0]
    pos = lax.broadcasted_iota(jnp.int32, x.shape, 0) % block
    y = x
    d = 1
    while d < block:
        y = y + jnp.where(pos + d < block, pltpu.roll(y, n - d, axis=0), 0.0)
        d *= 2
    return y - x


def _alias_inputs(bufs):
    arrays, specs, alias = [], [], {}
    for out_idx, buf in enumerate(bufs):
        if buf is not None:
            alias[len(arrays)] = out_idx
            arrays.append(buf)
            specs.append(pl.BlockSpec(memory_space=pl.ANY))
    return arrays, specs, alias


def _rms_rows(x_ref, g_ref, h_ref, rows, eps):
    n = x_ref.shape[0] // rows

    def body(r, c):
        sl = pl.ds(pl.multiple_of(r * rows, rows), rows)
        x = x_ref[sl, :]
        ms = jnp.mean(x * x, axis=-1, keepdims=True)
        h_ref[sl, :] = (x * lax.rsqrt(ms + eps) * g_ref[...]).astype(h_ref.dtype)
        return c

    lax.fori_loop(0, n, body, 0)


def _norm_mm_body(x_ref, g_ref, w_ref, o_ref, h_ref, *, rows):
    @pl.when(pl.program_id(1) == 0)
    def _():
        _rms_rows(x_ref, g_ref, h_ref, rows, NORM_EPS)

    o_ref[...] = jnp.dot(h_ref[...], w_ref[...].astype(BF16), preferred_element_type=F32).astype(o_ref.dtype)


def _norm_mm(x, g3, w3, l, *, tm, tn, name):
    m, d = x.shape
    n = w3.shape[-1]
    return pl.pallas_call(
        functools.partial(_norm_mm_body, rows=_pick(tm, 128, SUBLANES)),
        grid=(m // tm, n // tn),
        in_specs=[pl.BlockSpec((tm, d), lambda i, j: (i, 0)),
                  pl.BlockSpec((None, 1, d), lambda i, j: (l, 0, 0)),
                  pl.BlockSpec((None, d, tn), lambda i, j: (l, 0, j))],
        out_specs=pl.BlockSpec((tm, tn), lambda i, j: (i, j)),
        out_shape=jax.ShapeDtypeStruct((m, n), F32),
        scratch_shapes=[pltpu.VMEM((tm, d), BF16)],
        compiler_params=_cp(("parallel", "arbitrary")),
        name=name,
    )(x, g3, w3)


def _in_proj_body(x_ref, g_ref, wm_ref, ws_ref, o_ref, og_ref, h_ref, *, rows, n_main, n_small):
    j = pl.program_id(1)

    @pl.when(j == 0)
    def _():
        _rms_rows(x_ref, g_ref, h_ref, rows, NORM_EPS)

    @pl.when(j < n_main)
    def _():
        o_ref[...] = _bdot_nt(h_ref[...], wm_ref[0])

    @pl.when((j >= n_main) & (j < n_main + n_small))
    def _():
        o_ref[...] = _bdot_nt(h_ref[...], ws_ref[...])

    @pl.when(j >= n_main + n_small)
    def _():
        og_ref[...] = _sigmoid(_bdot_nt(h_ref[...], wm_ref[0])).astype(og_ref.dtype)


def _in_proj(x, g3, w_t, w_small_t, l, *, tm):
    m, d = x.shape
    tn = IN_TILE
    n_main, n_small = C_SMALL // tn, SMALL_W // tn
    xq0 = GLA_COLS + RW_COLS
    n_gla, n_rw = C_RR // tn, (C_XQ - C_RR) // tn
    n_gate = (w_t.shape[1] - xq0 - X_W) // tn

    def w_rows(i, j):
        jm = jnp.where(j < n_main, j, jnp.maximum(j - n_small, n_main - 1))
        start = jnp.where(jm < n_gla, jm * tn,
                          jnp.where(jm < n_gla + n_rw, GLA_COLS + (jm - n_gla) * tn, xq0 + (jm - n_gla - n_rw) * tn))
        return (l, pl.multiple_of(start, 2 * SUBLANES), 0)

    return pl.pallas_call(
        functools.partial(_in_proj_body, rows=_pick(tm, 128, SUBLANES), n_main=n_main, n_small=n_small),
        grid=(m // tm, n_main + n_small + n_gate),
        in_specs=[pl.BlockSpec((tm, d), lambda i, j: (i, 0), pipeline_mode=pl.Buffered(1)),
                  pl.BlockSpec((None, 1, d), lambda i, j: (l, 0, 0)),
                  pl.BlockSpec((pl.Element(1), pl.Element(tn), pl.Element(d)), w_rows),
                  pl.BlockSpec((None, tn, d), lambda i, j: (l, jnp.clip(j - n_main, 0, n_small - 1), 0),
                               pipeline_mode=pl.Buffered(1))],
        out_specs=[pl.BlockSpec((tm, tn), lambda i, j: (i, jnp.minimum(j, n_main + n_small - 1))),
                   pl.BlockSpec((tm, tn), lambda i, j: (i, jnp.maximum(j - n_main - n_small, 0)))],
        out_shape=[jax.ShapeDtypeStruct((m, (n_main + n_small) * tn), F32),
                   jax.ShapeDtypeStruct((m, n_gate * tn), BF16)],
        scratch_shapes=[pltpu.VMEM((tm, d), BF16)],
        compiler_params=_cp(("parallel", "arbitrary")),
        name="in_proj",
    )(x, g3, w_t, w_small_t)


def _merge_body(og_ref, or_ref, ox_ref, gg_ref, gr_ref, gx_ref, w_ref, o_ref):
    acc = gg_ref[...].astype(F32) * _bdot(og_ref[...], w_ref[0])
    acc += gr_ref[...].astype(F32) * _bdot(or_ref[...], w_ref[1])
    acc += gx_ref[...].astype(F32) * _bdot(ox_ref[...], w_ref[2])
    o_ref[...] = acc.astype(o_ref.dtype)


def _merge(o_gla, o_rw, o_x, gates, w_branch, l, *, tm, tn):
    m, bw = o_gla.shape
    d = w_branch.shape[-1]
    gate_blk = [b * d // tn for b in range(N_BRANCH)]
    o_spec = pl.BlockSpec((tm, bw), lambda i, j: (i, 0))
    return pl.pallas_call(
        _merge_body,
        grid=(m // tm, d // tn),
        in_specs=[o_spec, o_spec, o_spec]
        + [pl.BlockSpec((tm, tn), functools.partial(lambda i, j, off: (i, off + j), off=gate_blk[b]))
           for b in range(N_BRANCH)]
        + [pl.BlockSpec((None, N_BRANCH, bw, tn), lambda i, j: (l, 0, 0, j))],
        out_specs=pl.BlockSpec((tm, tn), lambda i, j: (i, j)),
        out_shape=jax.ShapeDtypeStruct((m, d), BF16),
        compiler_params=_cp(("parallel", "arbitrary")),
        name="branch_merge",
    )(o_gla, o_rw, o_x, gates, gates, gates, w_branch)


def _mm_res_body(a_ref, w_ref, r_ref, o_ref):
    o_ref[...] = r_ref[...] + _bdot(a_ref[...], w_ref[...])


def _mm_res(a, w3, l, res, *, tm, tn, name):
    m, kd = a.shape
    n = w3.shape[-1]
    return pl.pallas_call(
        _mm_res_body,
        grid=(m // tm, n // tn),
        in_specs=[pl.BlockSpec((tm, kd), lambda i, j: (i, 0)),
                  pl.BlockSpec((None, kd, tn), lambda i, j: (l, 0, j)),
                  pl.BlockSpec((tm, tn), lambda i, j: (i, j))],
        out_specs=pl.BlockSpec((tm, tn), lambda i, j: (i, j)),
        out_shape=jax.ShapeDtypeStruct((m, n), F32),
        compiler_params=_cp(("parallel", "arbitrary")),
        name=name,
    )(a, w3, res)


def _ffn_up_body(x_ref, g_ref, wg_ref, wu_ref, o_ref, h_ref, *, rows):
    @pl.when(pl.program_id(1) == 0)
    def _():
        _rms_rows(x_ref, g_ref, h_ref, rows, NORM_EPS)

    h = h_ref[...]
    a = jnp.dot(h, wg_ref[...].astype(BF16), preferred_element_type=F32)
    u = jnp.dot(h, wu_ref[...].astype(BF16), preferred_element_type=F32)
    o_ref[...] = (a * _sigmoid(a) * u).astype(o_ref.dtype)


def _ffn_up(x, g3, wg, wu, l, *, tm, tn):
    m, d = x.shape
    f = wg.shape[-1]
    w_spec = pl.BlockSpec((None, d, tn), lambda i, j: (l, 0, j))
    return pl.pallas_call(
        functools.partial(_ffn_up_body, rows=_pick(tm, 128, SUBLANES)),
        grid=(m // tm, f // tn),
        in_specs=[pl.BlockSpec((tm, d), lambda i, j: (i, 0)),
                  pl.BlockSpec((None, 1, d), lambda i, j: (l, 0, 0)),
                  w_spec, w_spec],
        out_specs=pl.BlockSpec((tm, tn), lambda i, j: (i, j)),
        out_shape=jax.ShapeDtypeStruct((m, f), BF16),
        scratch_shapes=[pltpu.VMEM((tm, d), BF16)],
        compiler_params=_cp(("parallel", "arbitrary")),
        name="ffn_up",
    )(x, g3, wg, wu)


def _final_norm_body(x_ref, g_ref, o_ref):
    x = x_ref[...]
    ms = jnp.mean(x * x, axis=-1, keepdims=True)
    o_ref[...] = x * lax.rsqrt(ms + NORM_EPS) * g_ref[...]


def _final_norm(x, g2, r0, rows):
    d = x.shape[1]
    tm = _pick(rows, 256, SUBLANES)
    assert r0 % tm == 0
    return pl.pallas_call(
        _final_norm_body,
        grid=(rows // tm,),
        in_specs=[pl.BlockSpec((tm, d), lambda i: (r0 // tm + i, 0)), pl.BlockSpec((1, d), lambda i: (0, 0))],
        out_specs=pl.BlockSpec((tm, d), lambda i: (i, 0)),
        out_shape=jax.ShapeDtypeStruct((rows, d), F32),
        compiler_params=_cp(("parallel",)),
        name="final_norm",
    )(x, g2)


def _gla_body(*refs, names, bb, ch, zero_init):
    r = dict(zip(names, refs))
    s_ref = r["s_out"]

    @pl.when(pl.program_id(1) == 0)
    def _():
        if zero_init:
            s_ref[...] = jnp.zeros(s_ref.shape, F32)
        else:
            s_ref[...] = r["s0"][...]

    grp = max(1, SUBLANES // ch)
    ur = grp * ch
    useq = lax.broadcasted_iota(jnp.int32, (ur, 1), 0) // ch
    z = _bdot(r["ga"][...], r["wa2"][...]) + r["ba2"][...]
    la = -_softplus(-z) * (1.0 / GLA_GATE_NORM)
    kk = r["k"][...]
    bc = _cumsum_rows(la, ch)
    tail = _suffix_sum_rows(la, ch)
    q_s = r["q"][...] * (GLA_DK ** -0.5)
    qe = q_s * jnp.exp(bc)
    kl = kk * jnp.exp(tail)
    dec_rows = jnp.exp(bc + tail)

    levels = []
    m = ch // 2
    while m >= SUBLANES:
        levels.append(m)
        m //= 2
    lev_q, lev_k, lev_mask = [], [], []
    if levels:
        assert bb == 1
        row = lax.broadcasted_iota(jnp.int32, (ch, ch), 0)
        col = lax.broadcasted_iota(jnp.int32, (ch, ch), 1)
        bc3 = _split3(bc)
        for m in levels:
            sel = (col == (row // (2 * m)) * (2 * m) + m - 1).astype(BF16)
            ref = sum(jnp.dot(sel, part, preferred_element_type=F32) for part in bc3)
            lev_q.append(q_s * jnp.exp(jnp.minimum(bc - ref, 0.0)))
            lev_k.append(kk * jnp.exp(jnp.minimum(ref - bc, 0.0)))
            lev_mask.append(((row // (2 * m)) == (col // (2 * m))) & (((row // m) % 2) == 1) & (((col // m) % 2) == 0))

    sub = lax.broadcasted_iota(jnp.int32, (SUBLANES, 1), 0)
    tile_seq = sub // min(ch, SUBLANES)

    def tile_attention(rows8):
        q8, k8, b8, v8 = q_s[rows8], kk[rows8], bc[rows8], r["v"][rows8, :]
        acc = [jnp.zeros((SUBLANES, GLA_DV), F32) for _ in range(GLA_HEADS)]
        for j in range(SUBLANES):
            prod = q8 * k8[j:j + 1] * jnp.exp(jnp.minimum(b8 - b8[j:j + 1], 0.0))
            valid = (sub >= j) & (tile_seq == j // min(ch, SUBLANES))
            for h in range(GLA_HEADS):
                a = jnp.sum(prod[:, h * GLA_DK:(h + 1) * GLA_DK], axis=-1, keepdims=True)
                acc[h] = acc[h] + jnp.where(valid, a, 0.0) * v8[j:j + 1, h * GLA_DV:(h + 1) * GLA_DV]
        return acc

    units = [(u, h) for u in range(bb // grp) for h in range(GLA_HEADS)]

    def sl(u, h):
        return (slice(u * ur, (u + 1) * ur), slice(h * GLA_DK, (h + 1) * GLA_DK),
                slice(h * GLA_DV, (h + 1) * GLA_DV))

    def own(q, x):
        return x if grp == 1 else jnp.where(useq == q, x, 0.0)

    att, o_int, o_near, upd = {}, {}, {}, {}
    for u in range(bb // grp):
        tiles = [tile_attention(slice(u * ur + t * SUBLANES, u * ur + (t + 1) * SUBLANES))
                 for t in range(ur // SUBLANES)]
        for h in range(GLA_HEADS):
            o_near[u, h] = tiles[0][h] if len(tiles) == 1 else jnp.concatenate([t[h] for t in tiles], axis=0)
    for u, h in units:
        rows, ks, vs = sl(u, h)
        if levels:
            att[u, h] = sum(jnp.where(mk, _bdot_nt(ql[rows, ks], kl_[rows, ks]), 0.0)
                            for ql, kl_, mk in zip(lev_q, lev_k, lev_mask))
        o_int[u, h] = sum(own(q, _bdot(qe[rows, ks], s_ref[u * grp + q, h])) for q in range(grp))
    for u, h in units:
        rows, ks, vs = sl(u, h)
        vh = r["v"][rows, vs]
        o = o_int[u, h] + o_near[u, h]
        if levels:
            o = o + _bdot(att[u, h], vh)
        for q in range(grp):
            upd[u * grp + q, h] = _bdot_tn(own(q, kl[rows, ks]), vh)
        ms = jnp.mean(o * o, axis=-1, keepdims=True)
        gh = r["g"][rows, vs]
        r["of"][rows, vs] = o * lax.rsqrt(ms + NORM_EPS) * r["on"][...] * (gh * _sigmoid(gh))
    for s in range(bb):
        for h in range(GLA_HEADS):
            ks = slice(h * GLA_DK, (h + 1) * GLA_DK)
            dec_col = jnp.broadcast_to(dec_rows[s * ch:s * ch + 1, ks], (GLA_DK, GLA_DK)).T
            dec = jnp.concatenate([dec_col] * (GLA_DV // GLA_DK), axis=1)
            s_ref[s, h] = s_ref[s, h] * dec + upd[s, h]
    r["o"][...] = r["of"][...].astype(r["o"].dtype)


def _gla(proj, wa2p, ba2, onorm, s0, o_buf, st_buf, l, *, depth, m_total, r0, nseq, t, ch, bb):
    rb = bb * ch
    nc = t // ch
    assert (bb == 1 or nc == 1) and r0 % rb == 0

    def rblk(ib, c):
        return r0 // rb + ib * nc + c

    def sec(width, off):
        return pl.BlockSpec((rb, width), lambda ib, c: (rblk(ib, c), off // width))

    st_spec = pl.BlockSpec((None, bb, GLA_HEADS, GLA_DK, GLA_DV), lambda ib, c: (l, ib, 0, 0, 0))
    names = ["q", "k", "v", "g", "ga", "wa2", "ba2", "on"]
    args = [proj, proj, proj, proj, proj, wa2p, ba2, onorm]
    in_specs = [sec(GLA_KW, C_GQ), sec(GLA_KW, C_GK), sec(GLA_VW, C_GV), sec(GLA_VW, C_GG),
                sec(LANES, proj.shape[1] - SMALL_W + S_GA),
                pl.BlockSpec((None, LANES, GLA_KW), lambda ib, c: (l, 0, 0)),
                pl.BlockSpec((None, 1, GLA_KW), lambda ib, c: (l, 0, 0)),
                pl.BlockSpec((None, 1, GLA_DV), lambda ib, c: (l, 0, 0))]
    if s0 is not None:
        names.append("s0")
        args.append(s0)
        in_specs.append(st_spec)
    al_arrays, al_specs, alias = _alias_inputs([o_buf, st_buf])
    alias = {len(args) + k: v for k, v in alias.items()}
    names += ["alias%d" % i for i in range(len(al_arrays))] + ["o", "s_out", "of"]
    return pl.pallas_call(
        functools.partial(_gla_body, names=names, bb=bb, ch=ch, zero_init=s0 is None),
        grid=(nseq // bb, nc),
        in_specs=in_specs + al_specs,
        out_specs=[pl.BlockSpec((rb, GLA_VW), lambda ib, c: (rblk(ib, c), 0)), st_spec],
        out_shape=[jax.ShapeDtypeStruct((m_total, GLA_VW), BF16),
                   jax.ShapeDtypeStruct((depth, nseq, GLA_HEADS, GLA_DK, GLA_DV), F32)],
        scratch_shapes=[pltpu.VMEM((rb, GLA_VW), F32)],
        input_output_aliases=alias,
        compiler_params=_cp(("parallel", "arbitrary")),
        name="gla",
    )(*args, *al_arrays)


def _rwkv_body(*refs, names, bb, ch, first, zero_init):
    r = dict(zip(names, refs))
    st_ref, carry_ref = r["st"], r["carry"]
    chained = bb == 1
    nr = bb * ch
    n2 = 2 * nr
    npair = RW_HEADS // 2
    c_id = pl.program_id(1)
    last_chunk = c_id == pl.num_programs(1) - 1
    lane = lax.broadcasted_iota(jnp.int32, (nr, LANES), 1)
    head_a = lane < RW_N
    r2 = lax.broadcasted_iota(jnp.int32, (n2, n2), 0)
    c2 = lax.broadcasted_iota(jnp.int32, (n2, n2), 1)
    same_blk = (r2 // ch) == (c2 // ch)
    strict = same_blk & (c2 < r2)
    incl = same_blk & (c2 <= r2)
    eye = (r2 == c2).astype(F32)
    blockdiag = (lax.broadcasted_iota(jnp.int32, (LANES, LANES), 0) < RW_N) == \
                (lax.broadcasted_iota(jnp.int32, (LANES, LANES), 1) < RW_N)
    zeros64 = jnp.zeros((RW_N, RW_N), F32)

    @pl.when(c_id == 0)
    def _():
        if chained:
            carry_ref[...] = r["sh"][...]
        for s in range(bb):
            for p in range(npair):
                if zero_init:
                    st_ref[s, p] = jnp.zeros((LANES, LANES), F32)
                else:
                    top = jnp.concatenate([r["s0"][s, 2 * p], zeros64], axis=1)
                    bot = jnp.concatenate([zeros64, r["s0"][s, 2 * p + 1]], axis=1)
                    st_ref[s, p] = jnp.concatenate([top, bot], axis=0)

    rowpos = lax.broadcasted_iota(jnp.int32, (nr, RW_W), 0) % ch
    sections = (("pr", 0), ("pk", RW_W), ("pv", 2 * RW_W), ("ps", 3 * RW_W))

    def shifted(name, off):
        p = r[name][...]
        prev = pltpu.roll(p, 1, axis=0)
        if chained:
            first_rows = jnp.broadcast_to(carry_ref[:, off:off + RW_W], (ch, RW_W))
        else:
            first_rows = r["sh"][:, off:off + RW_W]
        prev = jnp.where(rowpos == 0, first_rows, prev)
        return p, p + r["mu"][:, off:off + RW_W] * (prev - p)

    raw, feat = {}, {}
    for name, off in sections:
        raw[name], feat[name] = shifted(name, off)
    if chained:
        for name, off in sections:
            carry_ref[:, off:off + RW_W] = raw[name][ch - 1:ch, :]

    @pl.when(last_chunk)
    def _():
        for name, off in sections:
            for s in range(bb):
                last = s * ch + ch - 1
                r["sh_out"][s, :, off:off + RW_W] = raw[name][last:last + 1, :]

    f_r, f_k, f_v, f_s = feat["pr"], feat["pk"], feat["pv"], feat["ps"]
    xw = f_s[:, S_XW:S_XW + LANES]
    xa = f_s[:, S_XA:S_XA + LANES]
    xg = f_s[:, S_XG:S_XG + 2 * LANES]
    w_log = -_softplus(-(r["w0"][...] + _bdot(jnp.tanh(xw), r["w2"][...]))) - 0.5
    lw = -jnp.exp(w_log)
    a = _sigmoid(r["a0"][...] + _bdot(xa, r["a2"][...]))
    gate = _bdot(_sigmoid(xg), r["g2"][...])
    if first:
        vr = f_v
        r["vf_out"][...] = vr
    else:
        nu = _sigmoid(r["v0"][...] + _bdot(raw["ps"][:, S_V1:S_V1 + LANES], r["v2"][...]))
        vr = f_v + (r["vf"][...] - f_v) * nu
    kk = f_k * r["kkw"][...]
    kmod = f_k * (1.0 + (a - 1.0) * r["kaw"][...])
    g_in = _cumsum_rows(lw, ch)
    g_ex = g_in - lw
    g_tail = _suffix_sum_rows(lw, ch)
    e_in, e_ex, e_neg, e_tail = jnp.exp(g_in), jnp.exp(g_ex), jnp.exp(-g_in), jnp.exp(g_tail)
    e_end = jnp.exp(g_in + g_tail)

    def stack(x):
        return jnp.concatenate([jnp.where(head_a, x, 0.0), jnp.where(head_a, 0.0, x)], axis=0)

    def unstack(x):
        return x[:nr] + x[nr:]

    def pair_sum(x):
        sa = jnp.sum(jnp.where(head_a, x, 0.0), axis=-1, keepdims=True)
        sb = jnp.sum(jnp.where(head_a, 0.0, x), axis=-1, keepdims=True)
        return jnp.where(head_a, sa, sb)

    def seq_rows(s):
        return slice(s * ch, (s + 1) * ch)

    for p0 in range(0, npair, RW_PAIR_GROUP):
        group = range(p0, p0 + RW_PAIR_GROUP)
        q = {p: {} for p in group}
        for p in group:
            ls = slice(p * LANES, (p + 1) * LANES)
            d = q[p]
            kk_p = kk[:, ls]
            kap = kk_p / jnp.maximum(jnp.sqrt(pair_sum(kk_p * kk_p)), 1e-12)
            d["r"], d["k"], d["v"] = f_r[:, ls], kmod[:, ls], vr[:, ls]
            d["kap_t"] = kap * e_ex[:, ls]
            d["r_t"] = d["r"] * e_in[:, ls]
            ba = kap * a[:, ls]
            d["b_h"] = ba * e_tail[:, ls]
            d["k_h"] = d["k"] * e_tail[:, ls]
            lhs4 = jnp.concatenate([stack(d["kap_t"]), stack(d["r_t"])], axis=0)
            rhs4 = jnp.concatenate([stack(ba * e_neg[:, ls]), stack(d["k"] * e_neg[:, ls])], axis=0)
            m4 = _bdot_nt(lhs4, rhs4)
            d["n"] = jnp.where(strict, m4[:n2, :n2], 0.0)
            d["a_ak"] = jnp.where(strict, m4[:n2, n2:], 0.0)
            d["a_r"] = jnp.concatenate([jnp.where(incl, m4[n2:, :n2], 0.0), jnp.where(incl, m4[n2:, n2:], 0.0)], axis=1)
            d["dinv"] = eye
            d["v_st"] = stack(d["v"])
        for p in group:
            d = q[p]
            ks_parts, rs_parts = [], []
            for s in range(bb):
                both = _bdot_nt(jnp.concatenate([d["kap_t"][seq_rows(s)], d["r_t"][seq_rows(s)]], axis=0), st_ref[s, p])
                ks_parts.append(both[:ch])
                rs_parts.append(both[ch:])
            d["ks"] = ks_parts[0] if bb == 1 else jnp.concatenate(ks_parts, axis=0)
            d["rs"] = rs_parts[0] if bb == 1 else jnp.concatenate(rs_parts, axis=0)
            d["rhs"] = stack(d["ks"]) + _bdot(d["a_ak"], d["v_st"])
        m = 1
        while m < ch:
            low = same_blk & ((r2 // (2 * m)) == (c2 // (2 * m))) & (((r2 // m) % 2) == 1) & (((c2 // m) % 2) == 0)
            tmp = {p: _bdot(q[p]["dinv"], jnp.where(low, q[p]["n"], 0.0)) for p in group}
            for p in group:
                q[p]["dinv"] = q[p]["dinv"] - _bdot(tmp[p], q[p]["dinv"])
            m *= 2
        for p in group:
            d = q[p]
            d["u_st"] = -_bdot(d["dinv"], d["rhs"])
        for p in group:
            d = q[p]
            ls = slice(p * LANES, (p + 1) * LANES)
            y = unstack(stack(d["rs"]) + _bdot(d["a_r"], jnp.concatenate([d["u_st"], d["v_st"]], axis=0)))
            u_p = unstack(d["u_st"])
            for s in range(bb):
                rows = seq_rows(s)
                upd = _bdot_tn(jnp.concatenate([u_p[rows], d["v"][rows]], axis=0),
                               jnp.concatenate([d["b_h"][rows], d["k_h"][rows]], axis=0))
                st_ref[s, p] = st_ref[s, p] * e_end[s * ch:s * ch + 1, ls] + jnp.where(blockdiag, upd, 0.0)
            mean = pair_sum(y) * (1.0 / RW_N)
            yc = y - mean
            var = pair_sum(yc * yc) * (1.0 / RW_N)
            yn = yc * lax.rsqrt(var + RW_GN_EPS) * r["lnw"][:, ls] + r["lnb"][:, ls]
            bonus = pair_sum(d["r"] * d["k"] * r["rk"][:, ls]) * d["v"]
            r["o"][:, ls] = ((yn + bonus) * gate[:, ls]).astype(r["o"].dtype)

    @pl.when(last_chunk)
    def _():
        for s in range(bb):
            for p in range(npair):
                st = st_ref[s, p]
                r["s_out"][s, 2 * p] = st[:RW_N, :RW_N]
                r["s_out"][s, 2 * p + 1] = st[RW_N:, RW_N:]


def _rwkv(proj, vf, shift, s0, o_buf, st_buf, wts, l, *, depth, m_total, r0, nseq, t, ch, bb):
    first = vf is None
    rb = bb * ch
    nc = t // ch
    assert rb == RW_SLAB and (bb == 1 or nc == 1) and r0 % rb == 0
    small_off = proj.shape[1] - SMALL_W

    def rblk(ib, c):
        return r0 // rb + ib * nc + c

    def sec(off):
        return pl.BlockSpec((rb, RW_W), lambda ib, c: (rblk(ib, c), off // RW_W))

    def vec(width, dl=0):
        return pl.BlockSpec((None, 1, width), lambda ib, c: (l - dl, 0, 0))

    def mat(rows, dl=0):
        return pl.BlockSpec((None, rows, RW_W), lambda ib, c: (l - dl, 0, 0))

    group_rows = pl.BlockSpec((rb, RW_W), lambda ib, c: (ib * nc + c, 0))
    st_spec = pl.BlockSpec((None, bb, RW_HEADS, RW_N, RW_N), lambda ib, c: (l, ib, 0, 0, 0))
    sh_spec = pl.BlockSpec((bb, 1, SHIFT_W), lambda ib, c: (ib, 0, 0))
    names = ["pr", "pk", "pv", "ps", "mu", "sh", "w0", "w2", "a0", "a2", "g2", "kkw", "kaw", "rk", "lnw", "lnb"]
    args = [proj, proj, proj, proj, wts["mu"], shift, wts["w0"], wts["w2"], wts["a0"], wts["a2"], wts["g2"],
            wts["kk"], wts["ka"], wts["rk"], wts["lnw"], wts["lnb"]]
    sh_in = (pl.BlockSpec((None, 1, SHIFT_W), lambda ib, c: (ib, 0, 0)) if bb == 1
             else pl.BlockSpec((rb, SHIFT_W), lambda ib, c: (ib, 0)))
    in_specs = [sec(C_RR), sec(C_RK), sec(C_RV), sec(small_off), vec(SHIFT_W), sh_in,
                vec(RW_W), mat(LANES), vec(RW_W), mat(LANES), mat(2 * LANES)] + [vec(RW_W)] * 5
    if not first:
        names += ["vf", "v0", "v2"]
        args += [vf, wts["v0"], wts["v2"]]
        in_specs += [group_rows, vec(RW_W, 1), mat(LANES, 1)]
    if s0 is not None:
        names.append("s0")
        args.append(s0)
        in_specs.append(st_spec)
    al_arrays, al_specs, alias = _alias_inputs([o_buf, st_buf])
    alias = {len(args) + k: v for k, v in alias.items()}
    names += ["alias%d" % i for i in range(len(al_arrays))] + ["o", "s_out", "sh_out"]
    out_specs = [pl.BlockSpec((rb, RW_W), lambda ib, c: (rblk(ib, c), 0)), st_spec, sh_spec]
    out_shape = [jax.ShapeDtypeStruct((m_total, RW_W), BF16),
                 jax.ShapeDtypeStruct((depth, nseq, RW_HEADS, RW_N, RW_N), F32),
                 jax.ShapeDtypeStruct((nseq, 1, SHIFT_W), F32)]
    if first:
        names.append("vf_out")
        out_specs.append(group_rows)
        out_shape.append(jax.ShapeDtypeStruct((nseq * t, RW_W), F32))
    names += ["st", "carry"]
    res = pl.pallas_call(
        functools.partial(_rwkv_body, names=names, bb=bb, ch=ch, first=first,
                          zero_init=s0 is None),
        grid=(nseq // bb, nc),
        in_specs=in_specs + al_specs,
        out_specs=out_specs,
        out_shape=out_shape,
        scratch_shapes=[pltpu.VMEM((bb, RW_HEADS // 2, LANES, LANES), F32), pltpu.VMEM((1, SHIFT_W), F32)],
        input_output_aliases=alias,
        compiler_params=_cp(("parallel", "arbitrary")),
        name="rwkv7",
    )(*args, *al_arrays)
    return res[0], res[1], res[2], (res[3] if first else None)


def _xattn_body(*refs, names, bb, tq):
    r = dict(zip(names, refs))

    def kv(ref, s, h):
        return ref[s, :, h * X_DH:(h + 1) * X_DH]

    units = [(s, h) for s in range(bb) for h in range(X_HEADS)]
    pr = {}
    for s, h in units:
        sc = _bdot_nt(r["q"][s * tq:(s + 1) * tq, h * X_DH:(h + 1) * X_DH], kv(r["k"], s, h)) * (X_DH ** -0.5)
        e = jnp.exp(sc - jnp.max(sc, axis=-1, keepdims=True))
        pr[s, h] = e / jnp.sum(e, axis=-1, keepdims=True)
    for s, h in units:
        r["of"][s * tq:(s + 1) * tq, h * X_DH:(h + 1) * X_DH] = _bdot(pr[s, h], kv(r["v"], s, h))
    r["o"][...] = r["of"][...].astype(r["o"].dtype)


def _xattn(proj, mem_k, mem_v, o_buf, l, *, m_total, r0, nseq, t, tq, bb):
    rb = bb * tq
    nq = t // tq
    assert (bb == 1 or nq == 1) and r0 % rb == 0
    kv_spec = pl.BlockSpec((None, bb, MEM_LEN, X_W), lambda ib, c: (l, ib, 0, 0))
    rows = lambda ib, c: r0 // rb + ib * nq + c
    al_arrays, al_specs, alias = _alias_inputs([o_buf])
    alias = {3 + k: v for k, v in alias.items()}
    names = ["q", "k", "v"] + ["alias%d" % i for i in range(len(al_arrays))] + ["o", "of"]
    return pl.pallas_call(
        functools.partial(_xattn_body, names=names, bb=bb, tq=tq),
        grid=(nseq // bb, nq),
        in_specs=[pl.BlockSpec((rb, X_W), lambda ib, c: (rows(ib, c), C_XQ // X_W)), kv_spec, kv_spec] + al_specs,
        out_specs=pl.BlockSpec((rb, X_W), lambda ib, c: (rows(ib, c), 0)),
        out_shape=jax.ShapeDtypeStruct((m_total, X_W), BF16),
        scratch_shapes=[pltpu.VMEM((rb, X_W), F32)],
        input_output_aliases=alias,
        compiler_params=_cp(("parallel", "arbitrary")),
        name="mem_xattn",
    )(proj, mem_k, mem_v, *al_arrays)


def _xattn_cache_body(q_ref, k_hbm, v_hbm, alias_ref, o_ref, of_ref, kbuf, vbuf, sem, *, l, bb, tq):
    del alias_ref
    i = pl.program_id(0)
    n = pl.num_programs(0)

    def copies(step, slot):
        out = []
        for h in range(X_HEADS):
            for t, (src, dst) in enumerate(((k_hbm, kbuf), (v_hbm, vbuf))):
                out.append(pltpu.make_async_copy(src.at[l, pl.ds(step * bb, bb), :, h, :], dst.at[slot, h],
                                                 sem.at[slot, t, h]))
        return out

    @pl.when(i == 0)
    def _():
        for c in copies(0, 0):
            c.start()

    @pl.when(i + 1 < n)
    def _():
        for c in copies(i + 1, (i + 1) % 2):
            c.start()

    slot = i % 2
    for c in copies(i, slot):
        c.wait()

    grp = max(1, SUBLANES // tq)
    ur = grp * tq
    useq = lax.broadcasted_iota(jnp.int32, (ur, 1), 0) // tq
    units = [(u, h) for u in range(bb // grp) for h in range(X_HEADS)]
    pr = {}
    for u, h in units:
        qu = q_ref[u * ur:(u + 1) * ur, h * X_DH:(h + 1) * X_DH]
        for g in range(grp):
            sc = _bdot_nt(qu, kbuf[slot, h, u * grp + g]) * (X_DH ** -0.5)
            e = jnp.exp(sc - jnp.max(sc, axis=-1, keepdims=True))
            pr[u, h, g] = e / jnp.sum(e, axis=-1, keepdims=True)
    for u, h in units:
        o = _bdot(pr[u, h, 0], vbuf[slot, h, u * grp])
        for g in range(1, grp):
            o = jnp.where(useq == g, _bdot(pr[u, h, g], vbuf[slot, h, u * grp + g]), o)
        of_ref[u * ur:(u + 1) * ur, h * X_DH:(h + 1) * X_DH] = o
    o_ref[...] = of_ref[...].astype(o_ref.dtype)


def _xattn_cache(proj, cache_k, cache_v, o_buf, l, *, r0, nseq, tq, bb):
    rb = bb * tq
    assert r0 % rb == 0 and nseq % bb == 0
    rows = lambda ib: (r0 // rb + ib, 0)
    any_spec = pl.BlockSpec(memory_space=pl.ANY)
    buf = pltpu.VMEM((2, X_HEADS, bb, MEM_LEN, X_DH), F32)
    return pl.pallas_call(
        functools.partial(_xattn_cache_body, l=l, bb=bb, tq=tq),
        grid=(nseq // bb,),
        in_specs=[pl.BlockSpec((rb, X_W), lambda ib: (r0 // rb + ib, C_XQ // X_W)), any_spec, any_spec, any_spec],
        out_specs=pl.BlockSpec((rb, X_W), rows),
        out_shape=jax.ShapeDtypeStruct(o_buf.shape, o_buf.dtype),
        scratch_shapes=[pltpu.VMEM((rb, X_W), F32), buf, buf, pltpu.SemaphoreType.DMA((2, 2, X_HEADS))],
        input_output_aliases={3: 0},
        compiler_params=_cp(("arbitrary",)),
        name="cache_xattn",
    )(proj, cache_k, cache_v, o_buf)


def _pad_last(a, width):
    return jnp.pad(a, [(0, 0)] * (a.ndim - 1) + [(0, width - a.shape[-1])])


def _pad_rows(a, rows):
    return jnp.pad(a, [(0, 0)] * (a.ndim - 2) + [(0, rows - a.shape[-2]), (0, 0)])


def _small_section(ga, xw, xa, xg, v1):
    return jnp.concatenate([_pad_last(ga, S_XW - S_GA), _pad_last(xw, S_XA - S_XW), _pad_last(xa, S_XG - S_XA),
                            _pad_last(xg, S_V1 - S_XG), _pad_last(v1, SMALL_W - S_V1)], axis=-1)


def _small_weight_t(w_t, v1_t):
    sm = GLA_COLS + 3 * RW_W
    xq = GLA_COLS + RW_COLS
    pieces = ((S_GA, w_t[:, 3072:GLA_COLS]), (S_XW, w_t[:, sm:sm + RW_DECAY_R]),
              (S_XA, w_t[:, sm + RW_DECAY_R:sm + RW_DECAY_R + RW_A_R]),
              (S_XG, w_t[:, sm + RW_DECAY_R + RW_A_R:xq]), (S_V1, v1_t))
    rows = lax.optimization_barrier(tuple(p for _, p in pieces))
    out = jnp.zeros((w_t.shape[0], SMALL_W, w_t.shape[2]), BF16)
    for (off, _), piece in zip(pieces, rows):
        out = lax.dynamic_update_slice(out, piece.astype(BF16), (0, off, 0))
    return out


def _shift_layout(a):
    z16 = jnp.zeros(a.shape[:-1] + (GLA_RANK,), a.dtype)
    z32 = jnp.zeros(a.shape[:-1] + (RW_V_R,), a.dtype)
    o = 3 * RW_W
    small = _small_section(z16, a[..., o:o + RW_DECAY_R], a[..., o + RW_DECAY_R:o + RW_DECAY_R + RW_A_R],
                           a[..., o + RW_DECAY_R + RW_A_R:], z32)
    return jnp.concatenate([a[..., :o], small], axis=-1)


def _shift_unlayout(a):
    o = 3 * RW_W
    return jnp.concatenate([a[..., :o], a[..., o + S_XW:o + S_XW + RW_DECAY_R], a[..., o + S_XA:o + S_XA + RW_A_R],
                            a[..., o + S_XG:o + S_XG + RW_G_R]], axis=-1)


def kernel(x_prompt, x_sample, mem_prompt, state_gla, state_rwkv, state_rwkv_shift, cache_mem_k, cache_mem_v,
           g_norm1, w_in, gla_wa2, gla_ba2, gla_onorm, rw_mu, rw_w0, rw_w2, rw_a0, rw_a2, rw_g2, rw_kk, rw_ka,
           rw_rk, rw_lnx_w, rw_lnx_b, rw_v0, rw_v1, rw_v2, g_mem, w_mk, w_mv, w_branch, w_out, g_norm2,
           w_ff_gate, w_ff_up, w_ff_down, g_final):
    depth = w_in.shape[0]
    bp, tp, d = x_prompt.shape
    bs, ts, _ = x_sample.shape
    assert d % 1024 == 0 and ts in (1, 2, 4, 8) and tp % RW_SLAB == 0 and bs % (RW_SLAB // ts) == 0
    mp, ms = bp * tp, bs * ts
    m = mp + ms
    f = w_ff_gate.shape[-1]

    v1_all = jnp.concatenate([jnp.zeros((1, d, RW_V_R), F32), rw_v1], axis=0)
    w_t = jnp.swapaxes(w_in, 1, 2)
    w_small_t = _small_weight_t(w_t, jnp.swapaxes(v1_all, 1, 2)).astype(BF16)
    r3 = lambda a: a.reshape(a.shape[0], 1, -1)
    rw_wts = dict(mu=r3(_shift_layout(rw_mu)), w0=r3(rw_w0), w2=_pad_rows(rw_w2, LANES), a0=r3(rw_a0),
                  a2=_pad_rows(rw_a2, LANES), g2=_pad_rows(rw_g2, 2 * LANES), kk=r3(rw_kk), ka=r3(rw_ka),
                  rk=r3(rw_rk), lnw=r3(rw_lnx_w), lnb=r3(rw_lnx_b), v0=r3(rw_v0), v2=_pad_rows(rw_v2, LANES))
    wa2p = _pad_rows(gla_wa2, LANES)
    ba2, onorm = r3(gla_ba2), r3(gla_onorm)
    g1, g2n, gm = r3(g_norm1), r3(g_norm2), r3(g_mem)

    x = jnp.concatenate([x_prompt.reshape(mp, d), x_sample.reshape(ms, d)], axis=0)

    tm = _pick(m, 1152, 4 * SUBLANES)
    tn_d = _pick(d, 512, LANES)
    tn_f = _pick(f, 512, LANES)

    mem_rows = mem_prompt.reshape(bp * MEM_LEN, d)
    tmm = _pick(bp * MEM_LEN, 1024, SUBLANES)
    p_mem_k = jnp.stack([_norm_mm(mem_rows, gm, w_mk, l, tm=tmm, tn=tn_d, name="mem_k") for l in range(depth)])
    p_mem_v = jnp.stack([_norm_mm(mem_rows, gm, w_mv, l, tm=tmm, tn=tn_d, name="mem_v") for l in range(depth)])
    p_mem_k = p_mem_k.reshape(depth, bp, MEM_LEN, X_W)
    p_mem_v = p_mem_v.reshape(depth, bp, MEM_LEN, X_W)

    sh0 = jnp.zeros((bp, 1, SHIFT_W), F32)
    s_shift_in = jnp.repeat(_shift_layout(state_rwkv_shift), ts, axis=1)

    bb_s = RW_SLAB // ts
    tq_p = _pick(tp, 512, SUBLANES)
    common = dict(depth=depth, m_total=m)
    prompt = dict(r0=0, nseq=bp, t=tp)
    sample = dict(r0=mp, nseq=bs, t=ts)
    vf_p = vf_s = None
    pg_all = sg_all = pr_all = sr_all = None
    p_shift, s_shift = [], []
    for l in range(depth):
        proj, gates = _in_proj(x, g1, w_t, w_small_t, l, tm=tm)

        o_gla, pg_all = _gla(proj, wa2p, ba2, onorm, None, None, pg_all, l, **common, **prompt,
                             ch=GLA_CHUNK, bb=1)
        o_gla, sg_all = _gla(proj, wa2p, ba2, onorm, state_gla, o_gla, sg_all, l, **common, **sample,
                             ch=ts, bb=bb_s)
        o_rw, pr_all, sh_p, vf_new_p = _rwkv(proj, vf_p, sh0, None, None, pr_all, rw_wts, l, **common, **prompt,
                                             ch=RW_SLAB, bb=1)
        o_rw, sr_all, sh_s, vf_new_s = _rwkv(proj, vf_s, s_shift_in[l], state_rwkv, o_rw, sr_all, rw_wts, l,
                                             **common, **sample, ch=ts, bb=bb_s)
        if l == 0:
            vf_p, vf_s = vf_new_p, vf_new_s
        o_x = _xattn(proj, p_mem_k, p_mem_v, None, l, m_total=m, **prompt, tq=tq_p, bb=1)
        o_x = _xattn_cache(proj, cache_mem_k, cache_mem_v, o_x, l, r0=mp, nseq=bs, tq=ts, bb=XC_SEQS)

        merged = _merge(o_gla, o_rw, o_x, gates, w_branch, l, tm=tm, tn=tn_d)
        x = _mm_res(merged, w_out, l, x, tm=tm, tn=tn_d, name="out_proj")
        act = _ffn_up(x, g2n, w_ff_gate, w_ff_up, l, tm=tm, tn=tn_f)
        x = _mm_res(act, w_ff_down, l, x, tm=tm, tn=_pick(d, 256, LANES), name="ffn_down")
        p_shift.append(_shift_unlayout(sh_p[:, 0]))
        s_shift.append(_shift_unlayout(sh_s[:, 0]))

    y_prompt = _final_norm(x, g_final.reshape(1, d), 0, mp).reshape(bp, tp, d)
    y_sample = _final_norm(x, g_final.reshape(1, d), mp, ms).reshape(bs, ts, d)
    return (y_prompt, y_sample, pg_all, pr_all, jnp.stack(p_shift),
            p_mem_k.reshape(depth, bp, MEM_LEN, X_HEADS, X_DH), p_mem_v.reshape(depth, bp, MEM_LEN, X_HEADS, X_DH),
            sg_all, sr_all, jnp.stack(s_shift))
```

```python
import functools

import jax
import jax.numpy as jnp
from jax import lax
from jax.experimental import pallas as pl
from jax.experimental.pallas import tpu as pltpu

F32 = jnp.float32
BF16 = jnp.bfloat16

GLA_HEADS, GLA_DK, GLA_DV = 4, 128, 256
GLA_KW, GLA_VW, GLA_RANK = GLA_HEADS * GLA_DK, GLA_HEADS * GLA_DV, 16
GLA_GATE_NORM = 16.0
RW_HEADS, RW_N = 16, 64
RW_W = RW_HEADS * RW_N
RW_DECAY_R, RW_A_R, RW_V_R, RW_G_R = 64, 64, 32, 160
RW_GN_EPS = 64e-5
MEM_LEN, X_HEADS, X_DH = 256, 4, 256
X_W = X_HEADS * X_DH
N_BRANCH = 3
NORM_EPS = 1e-6
GLA_COLS = 2 * GLA_KW + 2 * GLA_VW + GLA_RANK
RW_COLS = 3 * RW_W + RW_DECAY_R + RW_A_R + RW_G_R

LANES = 128
SUBLANES = 8
VMEM_LIMIT_BYTES = 56 * 1024 * 1024

C_GQ, C_GK, C_GV, C_GG = 0, 512, 1024, 2048
C_RR, C_RK, C_RV, C_XQ, C_SMALL = 3072, 4096, 5120, 6144, 7168
SMALL_W = 1024
IN_TILE = 1024
S_GA, S_XW, S_XA, S_XG, S_V1 = 0, 128, 256, 384, 640
SHIFT_W = 3 * RW_W + SMALL_W

RW_SLAB = 64
XC_SEQS = 8
RW_PAIR_GROUP = 8
GLA_CHUNK = 64
GLA_SAFE_LOG_DECAY = 60.0


def _cp(sem):
    return pltpu.CompilerParams(dimension_semantics=sem, vmem_limit_bytes=VMEM_LIMIT_BYTES)


def _pick(n, target, mult):
    best = None
    for d in range(mult, min(n, target) + 1, mult):
        if n % d == 0:
            best = d
    assert best is not None, (n, target, mult)
    return best


def _bdot(a, b):
    return jnp.dot(a.astype(BF16), b.astype(BF16), preferred_element_type=F32)


def _bdot_nt(a, b):
    return lax.dot_general(a.astype(BF16), b.astype(BF16), (((1,), (1,)), ((), ())), preferred_element_type=F32)


def _bdot_tn(a, b):
    return lax.dot_general(a.astype(BF16), b.astype(BF16), (((0,), (0,)), ((), ())), preferred_element_type=F32)


def _split3(x):
    hi = x.astype(BF16)
    r1 = x - hi.astype(F32)
    mid = r1.astype(BF16)
    lo = (r1 - mid.astype(F32)).astype(BF16)
    return hi, mid, lo


def _softplus(y):
    return jnp.maximum(y, 0.0) + jnp.log(1.0 + jnp.exp(-jnp.abs(y)))


def _sigmoid(y):
    return 1.0 / (1.0 + jnp.exp(-y))


def _cumsum_rows(x, block):
    pos = lax.broadcasted_iota(jnp.int32, x.shape, 0) % block
    d = 1
    while d < block:
        x = x + jnp.where(pos >= d, pltpu.roll(x, d, axis=0), 0.0)
        d *= 2
    return x


def _suffix_sum_rows(x, block):
    n = x.shape[0]
    pos = lax.broadcasted_iota(jnp.int32, x.shape, 0) % block
    y = x
    d = 1
    while d < block:
        y = y + jnp.where(pos + d < block, pltpu.roll(y, n - d, axis=0), 0.0)
        d *= 2
    return y - x


def _alias_inputs(bufs):
    arrays, specs, alias = [], [], {}
    for out_idx, buf in enumerate(bufs):
        if buf is not None:
            alias[len(arrays)] = out_idx
            arrays.append(buf)
            specs.append(pl.BlockSpec(memory_space=pl.ANY))
    return arrays, specs, alias


def _rms_rows(x_ref, g_ref, h_ref, rows, eps):
    n = x_ref.shape[0] // rows

    def body(r, c):
        sl = pl.ds(pl.multiple_of(r * rows, rows), rows)
        x = x_ref[sl, :]
        ms = jnp.mean(x * x, axis=-1, keepdims=True)
        h_ref[sl, :] = (x * lax.rsqrt(ms + eps) * g_ref[...]).astype(h_ref.dtype)
        return c

    lax.fori_loop(0, n, body, 0)


def _norm_mm_body(x_ref, g_ref, w_ref, o_ref, h_ref, *, rows):
    @pl.when(pl.program_id(1) == 0)
    def _():
        _rms_rows(x_ref, g_ref, h_ref, rows, NORM_EPS)

    o_ref[...] = jnp.dot(h_ref[...], w_ref[...].astype(BF16), preferred_element_type=F32).astype(o_ref.dtype)


def _norm_mm(x, g3, w3, l, *, tm, tn, name):
    m, d = x.shape
    n = w3.shape[-1]
    return pl.pallas_call(
        functools.partial(_norm_mm_body, rows=_pick(tm, 128, SUBLANES)),
        grid=(m // tm, n // tn),
        in_specs=[pl.BlockSpec((tm, d), lambda i, j: (i, 0)),
                  pl.BlockSpec((None, 1, d), lambda i, j: (l, 0, 0)),
                  pl.BlockSpec((None, d, tn), lambda i, j: (l, 0, j))],
        out_specs=pl.BlockSpec((tm, tn), lambda i, j: (i, j)),
        out_shape=jax.ShapeDtypeStruct((m, n), F32),
        scratch_shapes=[pltpu.VMEM((tm, d), BF16)],
        compiler_params=_cp(("parallel", "arbitrary")),
        name=name,
    )(x, g3, w3)


def _in_proj_body(x_ref, g_ref, wm_ref, ws_ref, o_ref, og_ref, h_ref, *, rows, n_main, n_small):
    j = pl.program_id(1)

    @pl.when(j == 0)
    def _():
        _rms_rows(x_ref, g_ref, h_ref, rows, NORM_EPS)

    @pl.when(j < n_main)
    def _():
        o_ref[...] = _bdot_nt(h_ref[...], wm_ref[0])

    @pl.when((j >= n_main) & (j < n_main + n_small))
    def _():
        o_ref[...] = _bdot_nt(h_ref[...], ws_ref[...])

    @pl.when(j >= n_main + n_small)
    def _():
        og_ref[...] = _sigmoid(_bdot_nt(h_ref[...], wm_ref[0])).astype(og_ref.dtype)


def _in_proj(x, g3, w_t, w_small_t, l, *, tm):
    m, d = x.shape
    tn = IN_TILE
    n_main, n_small = C_SMALL // tn, SMALL_W // tn
    xq0 = GLA_COLS + RW_COLS
    n_gla, n_rw = C_RR // tn, (C_XQ - C_RR) // tn
    n_gate = (w_t.shape[1] - xq0 - X_W) // tn

    def w_rows(i, j):
        jm = jnp.where(j < n_main, j, jnp.maximum(j - n_small, n_main - 1))
        start = jnp.where(jm < n_gla, jm * tn,
                          jnp.where(jm < n_gla + n_rw, GLA_COLS + (jm - n_gla) * tn, xq0 + (jm - n_gla - n_rw) * tn))
        return (l, pl.multiple_of(start, 2 * SUBLANES), 0)

    return pl.pallas_call(
        functools.partial(_in_proj_body, rows=_pick(tm, 128, SUBLANES), n_main=n_main, n_small=n_small),
        grid=(m // tm, n_main + n_small + n_gate),
        in_specs=[pl.BlockSpec((tm, d), lambda i, j: (i, 0), pipeline_mode=pl.Buffered(1)),
                  pl.BlockSpec((None, 1, d), lambda i, j: (l, 0, 0)),
                  pl.BlockSpec((pl.Element(1), pl.Element(tn), pl.Element(d)), w_rows),
                  pl.BlockSpec((None, tn, d), lambda i, j: (l, jnp.clip(j - n_main, 0, n_small - 1), 0),
                               pipeline_mode=pl.Buffered(1))],
        out_specs=[pl.BlockSpec((tm, tn), lambda i, j: (i, jnp.minimum(j, n_main + n_small - 1))),
                   pl.BlockSpec((tm, tn), lambda i, j: (i, jnp.maximum(j - n_main - n_small, 0)))],
        out_shape=[jax.ShapeDtypeStruct((m, (n_main + n_small) * tn), F32),
                   jax.ShapeDtypeStruct((m, n_gate * tn), BF16)],
        scratch_shapes=[pltpu.VMEM((tm, d), BF16)],
        compiler_params=_cp(("parallel", "arbitrary")),
        name="in_proj",
    )(x, g3, w_t, w_small_t)


def _merge_body(og_ref, or_ref, ox_ref, gg_ref, gr_ref, gx_ref, w_ref, o_ref):
    acc = gg_ref[...].astype(F32) * _bdot(og_ref[...], w_ref[0])
    acc += gr_ref[...].astype(F32) * _bdot(or_ref[...], w_ref[1])
    acc += gx_ref[...].astype(F32) * _bdot(ox_ref[...], w_ref[2])
    o_ref[...] = acc.astype(o_ref.dtype)


def _merge(o_gla, o_rw, o_x, gates, w_branch, l, *, tm, tn):
    m, bw = o_gla.shape
    d = w_branch.shape[-1]
    gate_blk = [b * d // tn for b in range(N_BRANCH)]
    o_spec = pl.BlockSpec((tm, bw), lambda i, j: (i, 0))
    return pl.pallas_call(
        _merge_body,
        grid=(m // tm, d // tn),
        in_specs=[o_spec, o_spec, o_spec]
        + [pl.BlockSpec((tm, tn), functools.partial(lambda i, j, off: (i, off + j), off=gate_blk[b]))
           for b in range(N_BRANCH)]
        + [pl.BlockSpec((None, N_BRANCH, bw, tn), lambda i, j: (l, 0, 0, j))],
        out_specs=pl.BlockSpec((tm, tn), lambda i, j: (i, j)),
        out_shape=jax.ShapeDtypeStruct((m, d), BF16),
        compiler_params=_cp(("parallel", "arbitrary")),
        name="branch_merge",
    )(o_gla, o_rw, o_x, gates, gates, gates, w_branch)


def _mm_res_body(a_ref, w_ref, r_ref, o_ref):
    o_ref[...] = r_ref[...] + _bdot(a_ref[...], w_ref[...])


def _mm_res(a, w3, l, res, *, tm, tn, name):
    m, kd = a.shape
    n = w3.shape[-1]
    return pl.pallas_call(
        _mm_res_body,
        grid=(m // tm, n // tn),
        in_specs=[pl.BlockSpec((tm, kd), lambda i, j: (i, 0)),
                  pl.BlockSpec((None, kd, tn), lambda i, j: (l, 0, j)),
                  pl.BlockSpec((tm, tn), lambda i, j: (i, j))],
        out_specs=pl.BlockSpec((tm, tn), lambda i, j: (i, j)),
        out_shape=jax.ShapeDtypeStruct((m, n), F32),
        compiler_params=_cp(("parallel", "arbitrary")),
        name=name,
    )(a, w3, res)


def _ffn_up_body(x_ref, g_ref, wg_ref, wu_ref, o_ref, h_ref, *, rows):
    @pl.when(pl.program_id(1) == 0)
    def _():
        _rms_rows(x_ref, g_ref, h_ref, rows, NORM_EPS)

    h = h_ref[...]
    a = jnp.dot(h, wg_ref[...].astype(BF16), preferred_element_type=F32)
    u = jnp.dot(h, wu_ref[...].astype(BF16), preferred_element_type=F32)
    o_ref[...] = (a * _sigmoid(a) * u).astype(o_ref.dtype)


def _ffn_up(x, g3, wg, wu, l, *, tm, tn):
    m, d = x.shape
    f = wg.shape[-1]
    w_spec = pl.BlockSpec((None, d, tn), lambda i, j: (l, 0, j))
    return pl.pallas_call(
        functools.partial(_ffn_up_body, rows=_pick(tm, 128, SUBLANES)),
        grid=(m // tm, f // tn),
        in_specs=[pl.BlockSpec((tm, d), lambda i, j: (i, 0)),
                  pl.BlockSpec((None, 1, d), lambda i, j: (l, 0, 0)),
                  w_spec, w_spec],
        out_specs=pl.BlockSpec((tm, tn), lambda i, j: (i, j)),
        out_shape=jax.ShapeDtypeStruct((m, f), BF16),
        scratch_shapes=[pltpu.VMEM((tm, d), BF16)],
        compiler_params=_cp(("parallel", "arbitrary")),
        name="ffn_up",
    )(x, g3, wg, wu)


def _final_norm_body(x_ref, g_ref, o_ref):
    x = x_ref[...]
    ms = jnp.mean(x * x, axis=-1, keepdims=True)
    o_ref[...] = x * lax.rsqrt(ms + NORM_EPS) * g_ref[...]


def _final_norm(x, g2, r0, rows):
    d = x.shape[1]
    tm = _pick(rows, 256, SUBLANES)
    assert r0 % tm == 0
    return pl.pallas_call(
        _final_norm_body,
        grid=(rows // tm,),
        in_specs=[pl.BlockSpec((tm, d), lambda i: (r0 // tm + i, 0)), pl.BlockSpec((1, d), lambda i: (0, 0))],
        out_specs=pl.BlockSpec((tm, d), lambda i: (i, 0)),
        out_shape=jax.ShapeDtypeStruct((rows, d), F32),
        compiler_params=_cp(("parallel",)),
        name="final_norm",
    )(x, g2)


def _gla_body(*refs, names, bb, ch, zero_init):
    r = dict(zip(names, refs))
    s_ref = r["s_out"]

    @pl.when(pl.program_id(1) == 0)
    def _():
        if zero_init:
            s_ref[...] = jnp.zeros(s_ref.shape, F32)
        else:
            s_ref[...] = r["s0"][...]

    grp = max(1, SUBLANES // ch)
    ur = grp * ch
    useq = lax.broadcasted_iota(jnp.int32, (ur, 1), 0) // ch
    z = _bdot(r["ga"][...], r["wa2"][...]) + r["ba2"][...]
    la = -_softplus(-z) * (1.0 / GLA_GATE_NORM)
    kk = r["k"][...]
    bc = _cumsum_rows(la, ch)
    tail = _suffix_sum_rows(la, ch)
    q_s = r["q"][...] * (GLA_DK ** -0.5)
    qe = q_s * jnp.exp(bc)
    kl = kk * jnp.exp(tail)
    dec_rows = jnp.exp(bc + tail)

    units = [(u, h) for u in range(bb // grp) for h in range(GLA_HEADS)]

    def sl(u, h):
        return (slice(u * ur, (u + 1) * ur), slice(h * GLA_DK, (h + 1) * GLA_DK),
                slice(h * GLA_DV, (h + 1) * GLA_DV))

    def own(q, x):
        return x if grp == 1 else jnp.where(useq == q, x, 0.0)

    urow = lax.broadcasted_iota(jnp.int32, (ur, ur), 0)
    ucol = lax.broadcasted_iota(jnp.int32, (ur, ur), 1)

    def intra_scaled():
        ke = kk * jnp.exp(-bc)
        causal = (ucol <= urow) & ((urow // ch) == (ucol // ch))
        out = []
        for u, h in units:
            rows, ks, vs = sl(u, h)
            att = jnp.where(causal, _bdot_nt(qe[rows, ks], ke[rows, ks]), 0.0)
            out.append(_bdot(att, r["v"][rows, vs]))
        return tuple(out)

    def intra_pairwise():
        levels = []
        m = ch // 2
        while m >= SUBLANES:
            levels.append(m)
            m //= 2
        lev = []
        if levels:
            assert bb == 1
            bc3 = _split3(bc)
            for m in levels:
                sel = (ucol == (urow // (2 * m)) * (2 * m) + m - 1).astype(BF16)
                ref = sum(jnp.dot(sel, part, preferred_element_type=F32) for part in bc3)
                mask = ((urow // (2 * m)) == (ucol // (2 * m))) & (((urow // m) % 2) == 1) & (((ucol // m) % 2) == 0)
                lev.append((q_s * jnp.exp(jnp.minimum(bc - ref, 0.0)), kk * jnp.exp(jnp.minimum(ref - bc, 0.0)), mask))
        sub = lax.broadcasted_iota(jnp.int32, (SUBLANES, 1), 0)
        tile_seq = sub // min(ch, SUBLANES)

        def tile_attention(rows8):
            q8, k8, b8, v8 = q_s[rows8], kk[rows8], bc[rows8], r["v"][rows8, :]
            acc = [jnp.zeros((SUBLANES, GLA_DV), F32) for _ in range(GLA_HEADS)]
            for j in range(SUBLANES):
                prod = q8 * k8[j:j + 1] * jnp.exp(jnp.minimum(b8 - b8[j:j + 1], 0.0))
                valid = (sub >= j) & (tile_seq == j // min(ch, SUBLANES))
                for h in range(GLA_HEADS):
                    a = jnp.sum(prod[:, h * GLA_DK:(h + 1) * GLA_DK], axis=-1, keepdims=True)
                    acc[h] = acc[h] + jnp.where(valid, a, 0.0) * v8[j:j + 1, h * GLA_DV:(h + 1) * GLA_DV]
            return acc

        near = {}
        for u in range(bb // grp):
            tiles = [tile_attention(slice(u * ur + t * SUBLANES, u * ur + (t + 1) * SUBLANES))
                     for t in range(ur // SUBLANES)]
            for h in range(GLA_HEADS):
                near[u, h] = tiles[0][h] if len(tiles) == 1 else jnp.concatenate([t[h] for t in tiles], axis=0)
        out = []
        for u, h in units:
            rows, ks, vs = sl(u, h)
            o = near[u, h]
            if lev:
                att = sum(jnp.where(mk, _bdot_nt(ql[rows, ks], kl_[rows, ks]), 0.0) for ql, kl_, mk in lev)
                o = o + _bdot(att, r["v"][rows, vs])
            out.append(o)
        return tuple(out)

    intra = lax.cond(jnp.min(bc) > -GLA_SAFE_LOG_DECAY, intra_scaled, intra_pairwise)

    o_int, upd = {}, {}
    for u, h in units:
        rows, ks, vs = sl(u, h)
        o_int[u, h] = sum(own(q, _bdot(qe[rows, ks], s_ref[u * grp + q, h])) for q in range(grp))
    for i, (u, h) in enumerate(units):
        rows, ks, vs = sl(u, h)
        vh = r["v"][rows, vs]
        o = o_int[u, h] + intra[i]
        for q in range(grp):
            upd[u * grp + q, h] = _bdot_tn(own(q, kl[rows, ks]), vh)
        ms = jnp.mean(o * o, axis=-1, keepdims=True)
        gh = r["g"][rows, vs]
        r["of"][rows, vs] = o * lax.rsqrt(ms + NORM_EPS) * r["on"][...] * (gh * _sigmoid(gh))
    for s in range(bb):
        for h in range(GLA_HEADS):
            ks = slice(h * GLA_DK, (h + 1) * GLA_DK)
            dec_col = jnp.broadcast_to(dec_rows[s * ch:s * ch + 1, ks], (GLA_DK, GLA_DK)).T
            dec = jnp.concatenate([dec_col] * (GLA_DV // GLA_DK), axis=1)
            s_ref[s, h] = s_ref[s, h] * dec + upd[s, h]
    r["o"][...] = r["of"][...].astype(r["o"].dtype)


def _gla(proj, wa2p, ba2, onorm, s0, o_buf, st_buf, l, *, depth, m_total, r0, nseq, t, ch, bb):
    rb = bb * ch
    nc = t // ch
    assert (bb == 1 or nc == 1) and r0 % rb == 0

    def rblk(ib, c):
        return r0 // rb + ib * nc + c

    def sec(width, off):
        return pl.BlockSpec((rb, width), lambda ib, c: (rblk(ib, c), off // width))

    st_spec = pl.BlockSpec((None, bb, GLA_HEADS, GLA_DK, GLA_DV), lambda ib, c: (l, ib, 0, 0, 0))
    names = ["q", "k", "v", "g", "ga", "wa2", "ba2", "on"]
    args = [proj, proj, proj, proj, proj, wa2p, ba2, onorm]
    in_specs = [sec(GLA_KW, C_GQ), sec(GLA_KW, C_GK), sec(GLA_VW, C_GV), sec(GLA_VW, C_GG),
                sec(LANES, proj.shape[1] - SMALL_W + S_GA),
                pl.BlockSpec((None, LANES, GLA_KW), lambda ib, c: (l, 0, 0)),
                pl.BlockSpec((None, 1, GLA_KW), lambda ib, c: (l, 0, 0)),
                pl.BlockSpec((None, 1, GLA_DV), lambda ib, c: (l, 0, 0))]
    if s0 is not None:
        names.append("s0")
        args.append(s0)
        in_specs.append(st_spec)
    al_arrays, al_specs, alias = _alias_inputs([o_buf, st_buf])
    alias = {len(args) + k: v for k, v in alias.items()}
    names += ["alias%d" % i for i in range(len(al_arrays))] + ["o", "s_out", "of"]
    return pl.pallas_call(
        functools.partial(_gla_body, names=names, bb=bb, ch=ch, zero_init=s0 is None),
        grid=(nseq // bb, nc),
        in_specs=in_specs + al_specs,
        out_specs=[pl.BlockSpec((rb, GLA_VW), lambda ib, c: (rblk(ib, c), 0)), st_spec],
        out_shape=[jax.ShapeDtypeStruct((m_total, GLA_VW), BF16),
                   jax.ShapeDtypeStruct((depth, nseq, GLA_HEADS, GLA_DK, GLA_DV), F32)],
        scratch_shapes=[pltpu.VMEM((rb, GLA_VW), F32)],
        input_output_aliases=alias,
        compiler_params=_cp(("parallel", "arbitrary")),
        name="gla",
    )(*args, *al_arrays)


def _rwkv_body(*refs, names, bb, ch, first, zero_init):
    r = dict(zip(names, refs))
    st_ref, carry_ref = r["st"], r["carry"]
    chained = bb == 1
    nr = bb * ch
    n2 = 2 * nr
    npair = RW_HEADS // 2
    c_id = pl.program_id(1)
    last_chunk = c_id == pl.num_programs(1) - 1
    lane = lax.broadcasted_iota(jnp.int32, (nr, LANES), 1)
    head_a = lane < RW_N
    r2 = lax.broadcasted_iota(jnp.int32, (n2, n2), 0)
    c2 = lax.broadcasted_iota(jnp.int32, (n2, n2), 1)
    same_blk = (r2 // ch) == (c2 // ch)
    strict = same_blk & (c2 < r2)
    incl = same_blk & (c2 <= r2)
    eye = (r2 == c2).astype(F32)
    blockdiag = (lax.broadcasted_iota(jnp.int32, (LANES, LANES), 0) < RW_N) == \
                (lax.broadcasted_iota(jnp.int32, (LANES, LANES), 1) < RW_N)
    zeros64 = jnp.zeros((RW_N, RW_N), F32)

    @pl.when(c_id == 0)
    def _():
        if chained:
            carry_ref[...] = r["sh"][...]
        for s in range(bb):
            for p in range(npair):
                if zero_init:
                    st_ref[s, p] = jnp.zeros((LANES, LANES), F32)
                else:
                    top = jnp.concatenate([r["s0"][s, 2 * p], zeros64], axis=1)
                    bot = jnp.concatenate([zeros64, r["s0"][s, 2 * p + 1]], axis=1)
                    st_ref[s, p] = jnp.concatenate([top, bot], axis=0)

    rowpos = lax.broadcasted_iota(jnp.int32, (nr, RW_W), 0) % ch
    sections = (("pr", 0), ("pk", RW_W), ("pv", 2 * RW_W), ("ps", 3 * RW_W))

    def shifted(name, off):
        p = r[name][...]
        prev = pltpu.roll(p, 1, axis=0)
        if chained:
            first_rows = jnp.broadcast_to(carry_ref[:, off:off + RW_W], (ch, RW_W))
        else:
            first_rows = r["sh"][:, off:off + RW_W]
        prev = jnp.where(rowpos == 0, first_rows, prev)
        return p, p + r["mu"][:, off:off + RW_W] * (prev - p)

    raw, feat = {}, {}
    for name, off in sections:
        raw[name], feat[name] = shifted(name, off)
    if chained:
        for name, off in sections:
            carry_ref[:, off:off + RW_W] = raw[name][ch - 1:ch, :]

    @pl.when(last_chunk)
    def _():
        for name, off in sections:
            for s in range(bb):
                last = s * ch + ch - 1
                r["sh_out"][s, :, off:off + RW_W] = raw[name][last:last + 1, :]

    f_r, f_k, f_v, f_s = feat["pr"], feat["pk"], feat["pv"], feat["ps"]
    xw = f_s[:, S_XW:S_XW + LANES]
    xa = f_s[:, S_XA:S_XA + LANES]
    xg = f_s[:, S_XG:S_XG + 2 * LANES]
    w_log = -_softplus(-(r["w0"][...] + _bdot(jnp.tanh(xw), r["w2"][...]))) - 0.5
    lw = -jnp.exp(w_log)
    a = _sigmoid(r["a0"][...] + _bdot(xa, r["a2"][...]))
    gate = _bdot(_sigmoid(xg), r["g2"][...])
    if first:
        vr = f_v
        r["vf_out"][...] = vr
    else:
        nu = _sigmoid(r["v0"][...] + _bdot(raw["ps"][:, S_V1:S_V1 + LANES], r["v2"][...]))
        vr = f_v + (r["vf"][...] - f_v) * nu
    kk = f_k * r["kkw"][...]
    kmod = f_k * (1.0 + (a - 1.0) * r["kaw"][...])
    g_in = _cumsum_rows(lw, ch)
    g_ex = g_in - lw
    g_tail = _suffix_sum_rows(lw, ch)
    e_in, e_ex, e_neg, e_tail = jnp.exp(g_in), jnp.exp(g_ex), jnp.exp(-g_in), jnp.exp(g_tail)
    e_end = jnp.exp(g_in + g_tail)

    def stack(x):
        return jnp.concatenate([jnp.where(head_a, x, 0.0), jnp.where(head_a, 0.0, x)], axis=0)

    def unstack(x):
        return x[:nr] + x[nr:]

    def pair_sum(x):
        sa = jnp.sum(jnp.where(head_a, x, 0.0), axis=-1, keepdims=True)
        sb = jnp.sum(jnp.where(head_a, 0.0, x), axis=-1, keepdims=True)
        return jnp.where(head_a, sa, sb)

    def seq_rows(s):
        return slice(s * ch, (s + 1) * ch)

    for p0 in range(0, npair, RW_PAIR_GROUP):
        group = range(p0, p0 + RW_PAIR_GROUP)
        q = {p: {} for p in group}
        for p in group:
            ls = slice(p * LANES, (p + 1) * LANES)
            d = q[p]
            kk_p = kk[:, ls]
            kap = kk_p / jnp.maximum(jnp.sqrt(pair_sum(kk_p * kk_p)), 1e-12)
            d["r"], d["k"], d["v"] = f_r[:, ls], kmod[:, ls], vr[:, ls]
            d["kap_t"] = kap * e_ex[:, ls]
            d["r_t"] = d["r"] * e_in[:, ls]
            ba = kap * a[:, ls]
            d["b_h"] = ba * e_tail[:, ls]
            d["k_h"] = d["k"] * e_tail[:, ls]
            lhs4 = jnp.concatenate([stack(d["kap_t"]), stack(d["r_t"])], axis=0)
            rhs4 = jnp.concatenate([stack(ba * e_neg[:, ls]), stack(d["k"] * e_neg[:, ls])], axis=0)
            m4 = _bdot_nt(lhs4, rhs4)
            d["n"] = jnp.where(strict, m4[:n2, :n2], 0.0)
            d["a_ak"] = jnp.where(strict, m4[:n2, n2:], 0.0)
            d["a_r"] = jnp.concatenate([jnp.where(incl, m4[n2:, :n2], 0.0), jnp.where(incl, m4[n2:, n2:], 0.0)], axis=1)
            d["dinv"] = eye
            d["v_st"] = stack(d["v"])
        for p in group:
            d = q[p]
            ks_parts, rs_parts = [], []
            for s in range(bb):
                both = _bdot_nt(jnp.concatenate([d["kap_t"][seq_rows(s)], d["r_t"][seq_rows(s)]], axis=0), st_ref[s, p])
                ks_parts.append(both[:ch])
                rs_parts.append(both[ch:])
            d["ks"] = ks_parts[0] if bb == 1 else jnp.concatenate(ks_parts, axis=0)
            d["rs"] = rs_parts[0] if bb == 1 else jnp.concatenate(rs_parts, axis=0)
            d["rhs"] = stack(d["ks"]) + _bdot(d["a_ak"], d["v_st"])
        m = 1
        while m < ch:
            low = same_blk & ((r2 // (2 * m)) == (c2 // (2 * m))) & (((r2 // m) % 2) == 1) & (((c2 // m) % 2) == 0)
            tmp = {p: _bdot(q[p]["dinv"], jnp.where(low, q[p]["n"], 0.0)) for p in group}
            for p in group:
                q[p]["dinv"] = q[p]["dinv"] - _bdot(tmp[p], q[p]["dinv"])
            m *= 2
        for p in group:
            d = q[p]
            d["u_st"] = -_bdot(d["dinv"], d["rhs"])
        for p in group:
            d = q[p]
            ls = slice(p * LANES, (p + 1) * LANES)
            y = unstack(stack(d["rs"]) + _bdot(d["a_r"], jnp.concatenate([d["u_st"], d["v_st"]], axis=0)))
            u_p = unstack(d["u_st"])
            for s in range(bb):
                rows = seq_rows(s)
                upd = _bdot_tn(jnp.concatenate([u_p[rows], d["v"][rows]], axis=0),
                               jnp.concatenate([d["b_h"][rows], d["k_h"][rows]], axis=0))
                st_ref[s, p] = st_ref[s, p] * e_end[s * ch:s * ch + 1, ls] + jnp.where(blockdiag, upd, 0.0)
            mean = pair_sum(y) * (1.0 / RW_N)
            yc = y - mean
            var = pair_sum(yc * yc) * (1.0 / RW_N)
            yn = yc * lax.rsqrt(var + RW_GN_EPS) * r["lnw"][:, ls] + r["lnb"][:, ls]
            bonus = pair_sum(d["r"] * d["k"] * r["rk"][:, ls]) * d["v"]
            r["o"][:, ls] = ((yn + bonus) * gate[:, ls]).astype(r["o"].dtype)

    @pl.when(last_chunk)
    def _():
        for s in range(bb):
            for p in range(npair):
                st = st_ref[s, p]
                r["s_out"][s, 2 * p] = st[:RW_N, :RW_N]
                r["s_out"][s, 2 * p + 1] = st[RW_N:, RW_N:]


def _rwkv(proj, vf, shift, s0, o_buf, st_buf, wts, l, *, depth, m_total, r0, nseq, t, ch, bb):
    first = vf is None
    rb = bb * ch
    nc = t // ch
    assert rb == RW_SLAB and (bb == 1 or nc == 1) and r0 % rb == 0
    small_off = proj.shape[1] - SMALL_W

    def rblk(ib, c):
        return r0 // rb + ib * nc + c

    def sec(off):
        return pl.BlockSpec((rb, RW_W), lambda ib, c: (rblk(ib, c), off // RW_W))

    def vec(width, dl=0):
        return pl.BlockSpec((None, 1, width), lambda ib, c: (l - dl, 0, 0))

    def mat(rows, dl=0):
        return pl.BlockSpec((None, rows, RW_W), lambda ib, c: (l - dl, 0, 0))

    group_rows = pl.BlockSpec((rb, RW_W), lambda ib, c: (ib * nc + c, 0))
    st_spec = pl.BlockSpec((None, bb, RW_HEADS, RW_N, RW_N), lambda ib, c: (l, ib, 0, 0, 0))
    sh_spec = pl.BlockSpec((bb, 1, SHIFT_W), lambda ib, c: (ib, 0, 0))
    names = ["pr", "pk", "pv", "ps", "mu", "sh", "w0", "w2", "a0", "a2", "g2", "kkw", "kaw", "rk", "lnw", "lnb"]
    args = [proj, proj, proj, proj, wts["mu"], shift, wts["w0"], wts["w2"], wts["a0"], wts["a2"], wts["g2"],
            wts["kk"], wts["ka"], wts["rk"], wts["lnw"], wts["lnb"]]
    sh_in = (pl.BlockSpec((None, 1, SHIFT_W), lambda ib, c: (ib, 0, 0)) if bb == 1
             else pl.BlockSpec((rb, SHIFT_W), lambda ib, c: (ib, 0)))
    in_specs = [sec(C_RR), sec(C_RK), sec(C_RV), sec(small_off), vec(SHIFT_W), sh_in,
                vec(RW_W), mat(LANES), vec(RW_W), mat(LANES), mat(2 * LANES)] + [vec(RW_W)] * 5
    if not first:
        names += ["vf", "v0", "v2"]
        args += [vf, wts["v0"], wts["v2"]]
        in_specs += [group_rows, vec(RW_W, 1), mat(LANES, 1)]
    if s0 is not None:
        names.append("s0")
        args.append(s0)
        in_specs.append(st_spec)
    al_arrays, al_specs, alias = _alias_inputs([o_buf, st_buf])
    alias = {len(args) + k: v for k, v in alias.items()}
    names += ["alias%d" % i for i in range(len(al_arrays))] + ["o", "s_out", "sh_out"]
    out_specs = [pl.BlockSpec((rb, RW_W), lambda ib, c: (rblk(ib, c), 0)), st_spec, sh_spec]
    out_shape = [jax.ShapeDtypeStruct((m_total, RW_W), BF16),
                 jax.ShapeDtypeStruct((depth, nseq, RW_HEADS, RW_N, RW_N), F32),
                 jax.ShapeDtypeStruct((nseq, 1, SHIFT_W), F32)]
    if first:
        names.append("vf_out")
        out_specs.append(group_rows)
        out_shape.append(jax.ShapeDtypeStruct((nseq * t, RW_W), F32))
    names += ["st", "carry"]
    res = pl.pallas_call(
        functools.partial(_rwkv_body, names=names, bb=bb, ch=ch, first=first,
                          zero_init=s0 is None),
        grid=(nseq // bb, nc),
        in_specs=in_specs + al_specs,
        out_specs=out_specs,
        out_shape=out_shape,
        scratch_shapes=[pltpu.VMEM((bb, RW_HEADS // 2, LANES, LANES), F32), pltpu.VMEM((1, SHIFT_W), F32)],
        input_output_aliases=alias,
        compiler_params=_cp(("parallel", "arbitrary")),
        name="rwkv7",
    )(*args, *al_arrays)
    return res[0], res[1], res[2], (res[3] if first else None)


def _xattn_body(*refs, names, bb, tq):
    r = dict(zip(names, refs))

    def kv(ref, s, h):
        return ref[s, :, h * X_DH:(h + 1) * X_DH]

    units = [(s, h) for s in range(bb) for h in range(X_HEADS)]
    pr = {}
    for s, h in units:
        sc = _bdot_nt(r["q"][s * tq:(s + 1) * tq, h * X_DH:(h + 1) * X_DH], kv(r["k"], s, h)) * (X_DH ** -0.5)
        e = jnp.exp(sc - jnp.max(sc, axis=-1, keepdims=True))
        pr[s, h] = e / jnp.sum(e, axis=-1, keepdims=True)
    for s, h in units:
        r["of"][s * tq:(s + 1) * tq, h * X_DH:(h + 1) * X_DH] = _bdot(pr[s, h], kv(r["v"], s, h))
    r["o"][...] = r["of"][...].astype(r["o"].dtype)


def _xattn(proj, mem_k, mem_v, o_buf, l, *, m_total, r0, nseq, t, tq, bb):
    rb = bb * tq
    nq = t // tq
    assert (bb == 1 or nq == 1) and r0 % rb == 0
    kv_spec = pl.BlockSpec((None, bb, MEM_LEN, X_W), lambda ib, c: (l, ib, 0, 0))
    rows = lambda ib, c: r0 // rb + ib * nq + c
    al_arrays, al_specs, alias = _alias_inputs([o_buf])
    alias = {3 + k: v for k, v in alias.items()}
    names = ["q", "k", "v"] + ["alias%d" % i for i in range(len(al_arrays))] + ["o", "of"]
    return pl.pallas_call(
        functools.partial(_xattn_body, names=names, bb=bb, tq=tq),
        grid=(nseq // bb, nq),
        in_specs=[pl.BlockSpec((rb, X_W), lambda ib, c: (rows(ib, c), C_XQ // X_W)), kv_spec, kv_spec] + al_specs,
        out_specs=pl.BlockSpec((rb, X_W), lambda ib, c: (rows(ib, c), 0)),
        out_shape=jax.ShapeDtypeStruct((m_total, X_W), BF16),
        scratch_shapes=[pltpu.VMEM((rb, X_W), F32)],
        input_output_aliases=alias,
        compiler_params=_cp(("parallel", "arbitrary")),
        name="mem_xattn",
    )(proj, mem_k, mem_v, *al_arrays)


def _xattn_cache_body(q_ref, k_hbm, v_hbm, alias_ref, o_ref, of_ref, kbuf, vbuf, sem, *, l, bb, tq):
    del alias_ref
    i = pl.program_id(0)
    n = pl.num_programs(0)

    def copies(step, slot):
        out = []
        for h in range(X_HEADS):
            for t, (src, dst) in enumerate(((k_hbm, kbuf), (v_hbm, vbuf))):
                out.append(pltpu.make_async_copy(src.at[l, pl.ds(step * bb, bb), :, h, :], dst.at[slot, h],
                                                 sem.at[slot, t, h]))
        return out

    @pl.when(i == 0)
    def _():
        for c in copies(0, 0):
            c.start()

    @pl.when(i + 1 < n)
    def _():
        for c in copies(i + 1, (i + 1) % 2):
            c.start()

    slot = i % 2
    for c in copies(i, slot):
        c.wait()

    grp = max(1, SUBLANES // tq)
    ur = grp * tq
    useq = lax.broadcasted_iota(jnp.int32, (ur, 1), 0) // tq
    units = [(u, h) for u in range(bb // grp) for h in range(X_HEADS)]
    pr = {}
    for u, h in units:
        qu = q_ref[u * ur:(u + 1) * ur, h * X_DH:(h + 1) * X_DH]
        for g in range(grp):
            sc = _bdot_nt(qu, kbuf[slot, h, u * grp + g]) * (X_DH ** -0.5)
            e = jnp.exp(sc - jnp.max(sc, axis=-1, keepdims=True))
            pr[u, h, g] = e / jnp.sum(e, axis=-1, keepdims=True)
    for u, h in units:
        o = _bdot(pr[u, h, 0], vbuf[slot, h, u * grp])
        for g in range(1, grp):
            o = jnp.where(useq == g, _bdot(pr[u, h, g], vbuf[slot, h, u * grp + g]), o)
        of_ref[u * ur:(u + 1) * ur, h * X_DH:(h + 1) * X_DH] = o
    o_ref[...] = of_ref[...].astype(o_ref.dtype)


def _xattn_cache(proj, cache_k, cache_v, o_buf, l, *, r0, nseq, tq, bb):
    rb = bb * tq
    assert r0 % rb == 0 and nseq % bb == 0
    rows = lambda ib: (r0 // rb + ib, 0)
    any_spec = pl.BlockSpec(memory_space=pl.ANY)
    buf = pltpu.VMEM((2, X_HEADS, bb, MEM_LEN, X_DH), F32)
    return pl.pallas_call(
        functools.partial(_xattn_cache_body, l=l, bb=bb, tq=tq),
        grid=(nseq // bb,),
        in_specs=[pl.BlockSpec((rb, X_W), lambda ib: (r0 // rb + ib, C_XQ // X_W)), any_spec, any_spec, any_spec],
        out_specs=pl.BlockSpec((rb, X_W), rows),
        out_shape=jax.ShapeDtypeStruct(o_buf.shape, o_buf.dtype),
        scratch_shapes=[pltpu.VMEM((rb, X_W), F32), buf, buf, pltpu.SemaphoreType.DMA((2, 2, X_HEADS))],
        input_output_aliases={3: 0},
        compiler_params=_cp(("arbitrary",)),
        name="cache_xattn",
    )(proj, cache_k, cache_v, o_buf)


def _pad_last(a, width):
    return jnp.pad(a, [(0, 0)] * (a.ndim - 1) + [(0, width - a.shape[-1])])


def _pad_rows(a, rows):
    return jnp.pad(a, [(0, 0)] * (a.ndim - 2) + [(0, rows - a.shape[-2]), (0, 0)])


def _small_section(ga, xw, xa, xg, v1):
    return jnp.concatenate([_pad_last(ga, S_XW - S_GA), _pad_last(xw, S_XA - S_XW), _pad_last(xa, S_XG - S_XA),
                            _pad_last(xg, S_V1 - S_XG), _pad_last(v1, SMALL_W - S_V1)], axis=-1)


def _small_weight_t(w_t, v1_t):
    sm = GLA_COLS + 3 * RW_W
    xq = GLA_COLS + RW_COLS
    pieces = ((S_GA, w_t[:, 3072:GLA_COLS]), (S_XW, w_t[:, sm:sm + RW_DECAY_R]),
              (S_XA, w_t[:, sm + RW_DECAY_R:sm + RW_DECAY_R + RW_A_R]),
              (S_XG, w_t[:, sm + RW_DECAY_R + RW_A_R:xq]), (S_V1, v1_t))
    rows = lax.optimization_barrier(tuple(p for _, p in pieces))
    out = jnp.zeros((w_t.shape[0], SMALL_W, w_t.shape[2]), BF16)
    for (off, _), piece in zip(pieces, rows):
        out = lax.dynamic_update_slice(out, piece.astype(BF16), (0, off, 0))
    return out


def _shift_layout(a):
    z16 = jnp.zeros(a.shape[:-1] + (GLA_RANK,), a.dtype)
    z32 = jnp.zeros(a.shape[:-1] + (RW_V_R,), a.dtype)
    o = 3 * RW_W
    small = _small_section(z16, a[..., o:o + RW_DECAY_R], a[..., o + RW_DECAY_R:o + RW_DECAY_R + RW_A_R],
                           a[..., o + RW_DECAY_R + RW_A_R:], z32)
    return jnp.concatenate([a[..., :o], small], axis=-1)


def _shift_unlayout(a):
    o = 3 * RW_W
    return jnp.concatenate([a[..., :o], a[..., o + S_XW:o + S_XW + RW_DECAY_R], a[..., o + S_XA:o + S_XA + RW_A_R],
                            a[..., o + S_XG:o + S_XG + RW_G_R]], axis=-1)


def kernel(x_prompt, x_sample, mem_prompt, state_gla, state_rwkv, state_rwkv_shift, cache_mem_k, cache_mem_v,
           g_norm1, w_in, gla_wa2, gla_ba2, gla_onorm, rw_mu, rw_w0, rw_w2, rw_a0, rw_a2, rw_g2, rw_kk, rw_ka,
           rw_rk, rw_lnx_w, rw_lnx_b, rw_v0, rw_v1, rw_v2, g_mem, w_mk, w_mv, w_branch, w_out, g_norm2,
           w_ff_gate, w_ff_up, w_ff_down, g_final):
    depth = w_in.shape[0]
    bp, tp, d = x_prompt.shape
    bs, ts, _ = x_sample.shape
    assert d % 1024 == 0 and ts in (1, 2, 4, 8) and tp % RW_SLAB == 0 and bs % (RW_SLAB // ts) == 0
    mp, ms = bp * tp, bs * ts
    m = mp + ms
    f = w_ff_gate.shape[-1]

    v1_all = jnp.concatenate([jnp.zeros((1, d, RW_V_R), F32), rw_v1], axis=0)
    w_t = jnp.swapaxes(w_in, 1, 2)
    w_small_t = _small_weight_t(w_t, jnp.swapaxes(v1_all, 1, 2)).astype(BF16)
    r3 = lambda a: a.reshape(a.shape[0], 1, -1)
    rw_wts = dict(mu=r3(_shift_layout(rw_mu)), w0=r3(rw_w0), w2=_pad_rows(rw_w2, LANES), a0=r3(rw_a0),
                  a2=_pad_rows(rw_a2, LANES), g2=_pad_rows(rw_g2, 2 * LANES), kk=r3(rw_kk), ka=r3(rw_ka),
                  rk=r3(rw_rk), lnw=r3(rw_lnx_w), lnb=r3(rw_lnx_b), v0=r3(rw_v0), v2=_pad_rows(rw_v2, LANES))
    wa2p = _pad_rows(gla_wa2, LANES)
    ba2, onorm = r3(gla_ba2), r3(gla_onorm)
    g1, g2n, gm = r3(g_norm1), r3(g_norm2), r3(g_mem)

    x = jnp.concatenate([x_prompt.reshape(mp, d), x_sample.reshape(ms, d)], axis=0)

    tm = _pick(m, 1152, 4 * SUBLANES)
    tn_d = _pick(d, 512, LANES)
    tn_f = _pick(f, 512, LANES)

    mem_rows = mem_prompt.reshape(bp * MEM_LEN, d)
    tmm = _pick(bp * MEM_LEN, 1024, SUBLANES)
    p_mem_k = jnp.stack([_norm_mm(mem_rows, gm, w_mk, l, tm=tmm, tn=tn_d, name="mem_k") for l in range(depth)])
    p_mem_v = jnp.stack([_norm_mm(mem_rows, gm, w_mv, l, tm=tmm, tn=tn_d, name="mem_v") for l in range(depth)])
    p_mem_k = p_mem_k.reshape(depth, bp, MEM_LEN, X_W)
    p_mem_v = p_mem_v.reshape(depth, bp, MEM_LEN, X_W)

    sh0 = jnp.zeros((bp, 1, SHIFT_W), F32)
    s_shift_in = jnp.repeat(_shift_layout(state_rwkv_shift), ts, axis=1)

    bb_s = RW_SLAB // ts
    tq_p = _pick(tp, 512, SUBLANES)
    common = dict(depth=depth, m_total=m)
    prompt = dict(r0=0, nseq=bp, t=tp)
    sample = dict(r0=mp, nseq=bs, t=ts)
    vf_p = vf_s = None
    pg_all = sg_all = pr_all = sr_all = None
    p_shift, s_shift = [], []
    for l in range(depth):
        proj, gates = _in_proj(x, g1, w_t, w_small_t, l, tm=tm)

        o_gla, pg_all = _gla(proj, wa2p, ba2, onorm, None, None, pg_all, l, **common, **prompt,
                             ch=GLA_CHUNK, bb=1)
        o_gla, sg_all = _gla(proj, wa2p, ba2, onorm, state_gla, o_gla, sg_all, l, **common, **sample,
                             ch=ts, bb=bb_s)
        o_rw, pr_all, sh_p, vf_new_p = _rwkv(proj, vf_p, sh0, None, None, pr_all, rw_wts, l, **common, **prompt,
                                             ch=RW_SLAB, bb=1)
        o_rw, sr_all, sh_s, vf_new_s = _rwkv(proj, vf_s, s_shift_in[l], state_rwkv, o_rw, sr_all, rw_wts, l,
                                             **common, **sample, ch=ts, bb=bb_s)
        if l == 0:
            vf_p, vf_s = vf_new_p, vf_new_s
        o_x = _xattn(proj, p_mem_k, p_mem_v, None, l, m_total=m, **prompt, tq=tq_p, bb=1)
        o_x = _xattn_cache(proj, cache_mem_k, cache_mem_v, o_x, l, r0=mp, nseq=bs, tq=ts, bb=XC_SEQS)

        merged = _merge(o_gla, o_rw, o_x, gates, w_branch, l, tm=tm, tn=tn_d)
        x = _mm_res(merged, w_out, l, x, tm=tm, tn=tn_d, name="out_proj")
        act = _ffn_up(x, g2n, w_ff_gate, w_ff_up, l, tm=tm, tn=tn_f)
        x = _mm_res(act, w_ff_down, l, x, tm=tm, tn=_pick(d, 256, LANES), name="ffn_down")
        p_shift.append(_shift_unlayout(sh_p[:, 0]))
        s_shift.append(_shift_unlayout(sh_s[:, 0]))

    y_prompt = _final_norm(x, g_final.reshape(1, d), 0, mp).reshape(bp, tp, d)
    y_sample = _final_norm(x, g_final.reshape(1, d), mp, ms).reshape(bs, ts, d)
    return (y_prompt, y_sample, pg_all, pr_all, jnp.stack(p_shift),
            p_mem_k.reshape(depth, bp, MEM_LEN, X_HEADS, X_DH), p_mem_v.reshape(depth, bp, MEM_LEN, X_HEADS, X_DH),
            sg_all, sr_all, jnp.stack(s_shift))
```

```python
import functools

import jax
import jax.numpy as jnp
from jax import lax
from jax.experimental import pallas as pl
from jax.experimental.pallas import tpu as pltpu

F32 = jnp.float32
BF16 = jnp.bfloat16

GLA_HEADS, GLA_DK, GLA_DV = 4, 128, 256
GLA_KW, GLA_VW, GLA_RANK = GLA_HEADS * GLA_DK, GLA_HEADS * GLA_DV, 16
GLA_GATE_NORM = 16.0
RW_HEADS, RW_N = 16, 64
RW_W = RW_HEADS * RW_N
RW_DECAY_R, RW_A_R, RW_V_R, RW_G_R = 64, 64, 32, 160
RW_GN_EPS = 64e-5
MEM_LEN, X_HEADS, X_DH = 256, 4, 256
X_W = X_HEADS * X_DH
N_BRANCH = 3
NORM_EPS = 1e-6
GLA_COLS = 2 * GLA_KW + 2 * GLA_VW + GLA_RANK
RW_COLS = 3 * RW_W + RW_DECAY_R + RW_A_R + RW_G_R

LANES = 128
SUBLANES = 8
VMEM_LIMIT_BYTES = 56 * 1024 * 1024

C_GQ, C_GK, C_GV, C_GG = 0, 512, 1024, 2048
C_RR, C_RK, C_RV, C_XQ, C_SMALL = 3072, 4096, 5120, 6144, 7168
SMALL_W = 1024
IN_TILE = 1024
S_GA, S_XW, S_XA, S_XG, S_V1 = 0, 128, 256, 384, 640
SHIFT_W = 3 * RW_W + SMALL_W

RW_SLAB = 64
XC_SEQS = 8
RW_PAIR_GROUP = 8
GLA_CHUNK = 64


def _cp(sem):
    return pltpu.CompilerParams(dimension_semantics=sem, vmem_limit_bytes=VMEM_LIMIT_BYTES)


def _pick(n, target, mult):
    best = None
    for d in range(mult, min(n, target) + 1, mult):
        if n % d == 0:
            best = d
    assert best is not None, (n, target, mult)
    return best


def _bdot(a, b):
    return jnp.dot(a.astype(BF16), b.astype(BF16), preferred_element_type=F32)


def _bdot_nt(a, b):
    return lax.dot_general(a.astype(BF16), b.astype(BF16), (((1,), (1,)), ((), ())), preferred_element_type=F32)


def _bdot_tn(a, b):
    return lax.dot_general(a.astype(BF16), b.astype(BF16), (((0,), (0,)), ((), ())), preferred_element_type=F32)


def _split3(x):
    hi = x.astype(BF16)
    r1 = x - hi.astype(F32)
    mid = r1.astype(BF16)
    lo = (r1 - mid.astype(F32)).astype(BF16)
    return hi, mid, lo


def _softplus(y):
    return jnp.maximum(y, 0.0) + jnp.log(1.0 + jnp.exp(-jnp.abs(y)))


def _sigmoid(y):
    return 1.0 / (1.0 + jnp.exp(-y))


def _cumsum_rows(x, block):
    pos = lax.broadcasted_iota(jnp.int32, x.shape, 0) % block
    d = 1
    while d < block:
        x = x + jnp.where(pos >= d, pltpu.roll(x, d, axis=0), 0.0)
        d *= 2
    return x


def _suffix_sum_rows(x, block):
    n = x.shape[0]
    pos = lax.broadcasted_iota(jnp.int32, x.shape, 0) % block
    y = x
    d = 1
    while d < block:
        y = y + jnp.where(pos + d < block, pltpu.roll(y, n - d, axis=0), 0.0)
        d *= 2
    return y - x


def _alias_inputs(bufs):
    arrays, specs, alias = [], [], {}
    for out_idx, buf in enumerate(bufs):
        if buf is not None:
            alias[len(arrays)] = out_idx
            arrays.append(buf)
            specs.append(pl.BlockSpec(memory_space=pl.ANY))
    return arrays, specs, alias


def _rms_rows(x_ref, g_ref, h_ref, rows, eps):
    n = x_ref.shape[0] // rows

    def body(r, c):
        sl = pl.ds(pl.multiple_of(r * rows, rows), rows)
        x = x_ref[sl, :]
        ms = jnp.mean(x * x, axis=-1, keepdims=True)
        h_ref[sl, :] = (x * lax.rsqrt(ms + eps) * g_ref[...]).astype(h_ref.dtype)
        return c

    lax.fori_loop(0, n, body, 0)


def _norm_mm_body(x_ref, g_ref, w_ref, o_ref, h_ref, *, rows):
    @pl.when(pl.program_id(1) == 0)
    def _():
        _rms_rows(x_ref, g_ref, h_ref, rows, NORM_EPS)

    o_ref[...] = jnp.dot(h_ref[...], w_ref[...].astype(BF16), preferred_element_type=F32).astype(o_ref.dtype)


def _norm_mm(x, g3, w3, l, *, tm, tn, name):
    m, d = x.shape
    n = w3.shape[-1]
    return pl.pallas_call(
        functools.partial(_norm_mm_body, rows=_pick(tm, 128, SUBLANES)),
        grid=(m // tm, n // tn),
        in_specs=[pl.BlockSpec((tm, d), lambda i, j: (i, 0)),
                  pl.BlockSpec((None, 1, d), lambda i, j: (l, 0, 0)),
                  pl.BlockSpec((None, d, tn), lambda i, j: (l, 0, j))],
        out_specs=pl.BlockSpec((tm, tn), lambda i, j: (i, j)),
        out_shape=jax.ShapeDtypeStruct((m, n), F32),
        scratch_shapes=[pltpu.VMEM((tm, d), BF16)],
        compiler_params=_cp(("parallel", "arbitrary")),
        name=name,
    )(x, g3, w3)


def _in_proj_body(x_ref, g_ref, wm_ref, ws_ref, o_ref, og_ref, h_ref, *, rows, n_main, n_small):
    j = pl.program_id(1)

    @pl.when(j == 0)
    def _():
        _rms_rows(x_ref, g_ref, h_ref, rows, NORM_EPS)

    @pl.when(j < n_main)
    def _():
        o_ref[...] = _bdot_nt(h_ref[...], wm_ref[0])

    @pl.when((j >= n_main) & (j < n_main + n_small))
    def _():
        o_ref[...] = _bdot_nt(h_ref[...], ws_ref[...])

    @pl.when(j >= n_main + n_small)
    def _():
        og_ref[...] = _sigmoid(_bdot_nt(h_ref[...], wm_ref[0])).astype(og_ref.dtype)


def _in_proj(x, g3, w_t, w_small_t, l, *, tm):
    m, d = x.shape
    tn = IN_TILE
    n_main, n_small = C_SMALL // tn, SMALL_W // tn
    xq0 = GLA_COLS + RW_COLS
    n_gla, n_rw = C_RR // tn, (C_XQ - C_RR) // tn
    n_gate = (w_t.shape[1] - xq0 - X_W) // tn

    def w_rows(i, j):
        jm = jnp.where(j < n_main, j, jnp.maximum(j - n_small, n_main - 1))
        start = jnp.where(jm < n_gla, jm * tn,
                          jnp.where(jm < n_gla + n_rw, GLA_COLS + (jm - n_gla) * tn, xq0 + (jm - n_gla - n_rw) * tn))
        return (l, pl.multiple_of(start, 2 * SUBLANES), 0)

    return pl.pallas_call(
        functools.partial(_in_proj_body, rows=_pick(tm, 128, SUBLANES), n_main=n_main, n_small=n_small),
        grid=(m // tm, n_main + n_small + n_gate),
        in_specs=[pl.BlockSpec((tm, d), lambda i, j: (i, 0), pipeline_mode=pl.Buffered(1)),
                  pl.BlockSpec((None, 1, d), lambda i, j: (l, 0, 0)),
                  pl.BlockSpec((pl.Element(1), pl.Element(tn), pl.Element(d)), w_rows),
                  pl.BlockSpec((None, tn, d), lambda i, j: (l, jnp.clip(j - n_main, 0, n_small - 1), 0),
                               pipeline_mode=pl.Buffered(1))],
        out_specs=[pl.BlockSpec((tm, tn), lambda i, j: (i, jnp.minimum(j, n_main + n_small - 1))),
                   pl.BlockSpec((tm, tn), lambda i, j: (i, jnp.maximum(j - n_main - n_small, 0)))],
        out_shape=[jax.ShapeDtypeStruct((m, (n_main + n_small) * tn), F32),
                   jax.ShapeDtypeStruct((m, n_gate * tn), BF16)],
        scratch_shapes=[pltpu.VMEM((tm, d), BF16)],
        compiler_params=_cp(("parallel", "arbitrary")),
        name="in_proj",
    )(x, g3, w_t, w_small_t)


def _merge_body(og_ref, or_ref, ox_ref, gg_ref, gr_ref, gx_ref, w_ref, o_ref):
    acc = gg_ref[...].astype(F32) * _bdot(og_ref[...], w_ref[0])
    acc += gr_ref[...].astype(F32) * _bdot(or_ref[...], w_ref[1])
    acc += gx_ref[...].astype(F32) * _bdot(ox_ref[...], w_ref[2])
    o_ref[...] = acc.astype(o_ref.dtype)


def _merge(o_gla, o_rw, o_x, gates, w_branch, l, *, tm, tn):
    m, bw = o_gla.shape
    d = w_branch.shape[-1]
    gate_blk = [b * d // tn for b in range(N_BRANCH)]
    o_spec = pl.BlockSpec((tm, bw), lambda i, j: (i, 0))
    return pl.pallas_call(
        _merge_body,
        grid=(m // tm, d // tn),
        in_specs=[o_spec, o_spec, o_spec]
        + [pl.BlockSpec((tm, tn), functools.partial(lambda i, j, off: (i, off + j), off=gate_blk[b]))
           for b in range(N_BRANCH)]
        + [pl.BlockSpec((None, N_BRANCH, bw, tn), lambda i, j: (l, 0, 0, j))],
        out_specs=pl.BlockSpec((tm, tn), lambda i, j: (i, j)),
        out_shape=jax.ShapeDtypeStruct((m, d), BF16),
        compiler_params=_cp(("parallel", "arbitrary")),
        name="branch_merge",
    )(o_gla, o_rw, o_x, gates, gates, gates, w_branch)


def _mm_res_body(a_ref, w_ref, r_ref, o_ref):
    o_ref[...] = r_ref[...] + _bdot(a_ref[...], w_ref[...])


def _mm_res(a, w3, l, res, *, tm, tn, name):
    m, kd = a.shape
    n = w3.shape[-1]
    return pl.pallas_call(
        _mm_res_body,
        grid=(m // tm, n // tn),
        in_specs=[pl.BlockSpec((tm, kd), lambda i, j: (i, 0)),
                  pl.BlockSpec((None, kd, tn), lambda i, j: (l, 0, j)),
                  pl.BlockSpec((tm, tn), lambda i, j: (i, j))],
        out_specs=pl.BlockSpec((tm, tn), lambda i, j: (i, j)),
        out_shape=jax.ShapeDtypeStruct((m, n), F32),
        compiler_params=_cp(("parallel", "arbitrary")),
        name=name,
    )(a, w3, res)


def _ffn_up_body(x_ref, g_ref, wg_ref, wu_ref, o_ref, h_ref, *, rows):
    @pl.when(pl.program_id(1) == 0)
    def _():
        _rms_rows(x_ref, g_ref, h_ref, rows, NORM_EPS)

    h = h_ref[...]
    a = jnp.dot(h, wg_ref[...].astype(BF16), preferred_element_type=F32)
    u = jnp.dot(h, wu_ref[...].astype(BF16), preferred_element_type=F32)
    o_ref[...] = (a * _sigmoid(a) * u).astype(o_ref.dtype)


def _ffn_up(x, g3, wg, wu, l, *, tm, tn):
    m, d = x.shape
    f = wg.shape[-1]
    w_spec = pl.BlockSpec((None, d, tn), lambda i, j: (l, 0, j))
    return pl.pallas_call(
        functools.partial(_ffn_up_body, rows=_pick(tm, 128, SUBLANES)),
        grid=(m // tm, f // tn),
        in_specs=[pl.BlockSpec((tm, d), lambda i, j: (i, 0)),
                  pl.BlockSpec((None, 1, d), lambda i, j: (l, 0, 0)),
                  w_spec, w_spec],
        out_specs=pl.BlockSpec((tm, tn), lambda i, j: (i, j)),
        out_shape=jax.ShapeDtypeStruct((m, f), BF16),
        scratch_shapes=[pltpu.VMEM((tm, d), BF16)],
        compiler_params=_cp(("parallel", "arbitrary")),
        name="ffn_up",
    )(x, g3, wg, wu)


def _final_norm_body(x_ref, g_ref, o_ref):
    x = x_ref[...]
    ms = jnp.mean(x * x, axis=-1, keepdims=True)
    o_ref[...] = x * lax.rsqrt(ms + NORM_EPS) * g_ref[...]


def _final_norm(x, g2, r0, rows):
    d = x.shape[1]
    tm = _pick(rows, 256, SUBLANES)
    assert r0 % tm == 0
    return pl.pallas_call(
        _final_norm_body,
        grid=(rows // tm,),
        in_specs=[pl.BlockSpec((tm, d), lambda i: (r0 // tm + i, 0)), pl.BlockSpec((1, d), lambda i: (0, 0))],
        out_specs=pl.BlockSpec((tm, d), lambda i: (i, 0)),
        out_shape=jax.ShapeDtypeStruct((rows, d), F32),
        compiler_params=_cp(("parallel",)),
        name="final_norm",
    )(x, g2)


def _gla_body(*refs, names, bb, ch, zero_init):
    r = dict(zip(names, refs))
    s_ref = r["s_out"]

    @pl.when(pl.program_id(1) == 0)
    def _():
        if zero_init:
            s_ref[...] = jnp.zeros(s_ref.shape, F32)
        else:
            s_ref[...] = r["s0"][...]

    grp = max(1, SUBLANES // ch)
    ur = grp * ch
    useq = lax.broadcasted_iota(jnp.int32, (ur, 1), 0) // ch
    z = _bdot(r["ga"][...], r["wa2"][...]) + r["ba2"][...]
    la = -_softplus(-z) * (1.0 / GLA_GATE_NORM)
    kk = r["k"][...]
    bc = _cumsum_rows(la, ch)
    tail = _suffix_sum_rows(la, ch)
    q_s = r["q"][...] * (GLA_DK ** -0.5)
    qe = q_s * jnp.exp(bc)
    kl = kk * jnp.exp(tail)
    dec_rows = jnp.exp(bc + tail)

    units = [(u, h) for u in range(bb // grp) for h in range(GLA_HEADS)]

    def sl(u, h):
        return (slice(u * ur, (u + 1) * ur), slice(h * GLA_DK, (h + 1) * GLA_DK),
                slice(h * GLA_DV, (h + 1) * GLA_DV))

    def own(q, x):
        return x if grp == 1 else jnp.where(useq == q, x, 0.0)

    urow = lax.broadcasted_iota(jnp.int32, (ur, ur), 0)
    ucol = lax.broadcasted_iota(jnp.int32, (ur, ur), 1)

    def intra_pairwise():
        levels = []
        m = ch // 2
        while m >= SUBLANES:
            levels.append(m)
            m //= 2
        lev = []
        if levels:
            assert bb == 1
            bc3 = _split3(bc)
            for m in levels:
                sel = (ucol == (urow // (2 * m)) * (2 * m) + m - 1).astype(BF16)
                ref = sum(jnp.dot(sel, part, preferred_element_type=F32) for part in bc3)
                mask = ((urow // (2 * m)) == (ucol // (2 * m))) & (((urow // m) % 2) == 1) & (((ucol // m) % 2) == 0)
                lev.append((q_s * jnp.exp(jnp.minimum(bc - ref, 0.0)), kk * jnp.exp(jnp.minimum(ref - bc, 0.0)), mask))
        sub = lax.broadcasted_iota(jnp.int32, (SUBLANES, 1), 0)
        tile_seq = sub // min(ch, SUBLANES)

        def tile_attention(rows8):
            q8, k8, b8, v8 = q_s[rows8], kk[rows8], bc[rows8], r["v"][rows8, :]
            acc = [jnp.zeros((SUBLANES, GLA_DV), F32) for _ in range(GLA_HEADS)]
            for j in range(SUBLANES):
                prod = q8 * k8[j:j + 1] * jnp.exp(jnp.minimum(b8 - b8[j:j + 1], 0.0))
                valid = (sub >= j) & (tile_seq == j // min(ch, SUBLANES))
                for h in range(GLA_HEADS):
                    a = jnp.sum(prod[:, h * GLA_DK:(h + 1) * GLA_DK], axis=-1, keepdims=True)
                    acc[h] = acc[h] + jnp.where(valid, a, 0.0) * v8[j:j + 1, h * GLA_DV:(h + 1) * GLA_DV]
            return acc

        near = {}
        for u in range(bb // grp):
            tiles = [tile_attention(slice(u * ur + t * SUBLANES, u * ur + (t + 1) * SUBLANES))
                     for t in range(ur // SUBLANES)]
            for h in range(GLA_HEADS):
                near[u, h] = tiles[0][h] if len(tiles) == 1 else jnp.concatenate([t[h] for t in tiles], axis=0)
        out = []
        for u, h in units:
            rows, ks, vs = sl(u, h)
            o = near[u, h]
            if lev:
                att = sum(jnp.where(mk, _bdot_nt(ql[rows, ks], kl_[rows, ks]), 0.0) for ql, kl_, mk in lev)
                o = o + _bdot(att, r["v"][rows, vs])
            out.append(o)
        return tuple(out)

    intra = intra_pairwise()

    o_int, upd = {}, {}
    for u, h in units:
        rows, ks, vs = sl(u, h)
        o_int[u, h] = sum(own(q, _bdot(qe[rows, ks], s_ref[u * grp + q, h])) for q in range(grp))
    for i, (u, h) in enumerate(units):
        rows, ks, vs = sl(u, h)
        vh = r["v"][rows, vs]
        o = o_int[u, h] + intra[i]
        for q in range(grp):
            upd[u * grp + q, h] = _bdot_tn(own(q, kl[rows, ks]), vh)
        ms = jnp.mean(o * o, axis=-1, keepdims=True)
        gh = r["g"][rows, vs]
        r["of"][rows, vs] = o * lax.rsqrt(ms + NORM_EPS) * r["on"][...] * (gh * _sigmoid(gh))
    for s in range(bb):
        for h in range(GLA_HEADS):
            ks = slice(h * GLA_DK, (h + 1) * GLA_DK)
            dec_col = jnp.broadcast_to(dec_rows[s * ch:s * ch + 1, ks], (GLA_DK, GLA_DK)).T
            dec = jnp.concatenate([dec_col] * (GLA_DV // GLA_DK), axis=1)
            s_ref[s, h] = s_ref[s, h] * dec + upd[s, h]
    r["o"][...] = r["of"][...].astype(r["o"].dtype)


def _gla(proj, wa2p, ba2, onorm, s0, o_buf, st_buf, l, *, depth, m_total, r0, nseq, t, ch, bb):
    rb = bb * ch
    nc = t // ch
    assert (bb == 1 or nc == 1) and r0 % rb == 0

    def rblk(ib, c):
        return r0 // rb + ib * nc + c

    def sec(width, off):
        return pl.BlockSpec((rb, width), lambda ib, c: (rblk(ib, c), off // width))

    st_spec = pl.BlockSpec((None, bb, GLA_HEADS, GLA_DK, GLA_DV), lambda ib, c: (l, ib, 0, 0, 0))
    names = ["q", "k", "v", "g", "ga", "wa2", "ba2", "on"]
    args = [proj, proj, proj, proj, proj, wa2p, ba2, onorm]
    in_specs = [sec(GLA_KW, C_GQ), sec(GLA_KW, C_GK), sec(GLA_VW, C_GV), sec(GLA_VW, C_GG),
                sec(LANES, proj.shape[1] - SMALL_W + S_GA),
                pl.BlockSpec((None, LANES, GLA_KW), lambda ib, c: (l, 0, 0)),
                pl.BlockSpec((None, 1, GLA_KW), lambda ib, c: (l, 0, 0)),
                pl.BlockSpec((None, 1, GLA_DV), lambda ib, c: (l, 0, 0))]
    if s0 is not None:
        names.append("s0")
        args.append(s0)
        in_specs.append(st_spec)
    al_arrays, al_specs, alias = _alias_inputs([o_buf, st_buf])
    alias = {len(args) + k: v for k, v in alias.items()}
    names += ["alias%d" % i for i in range(len(al_arrays))] + ["o", "s_out", "of"]
    return pl.pallas_call(
        functools.partial(_gla_body, names=names, bb=bb, ch=ch, zero_init=s0 is None),
        grid=(nseq // bb, nc),
        in_specs=in_specs + al_specs,
        out_specs=[pl.BlockSpec((rb, GLA_VW), lambda ib, c: (rblk(ib, c), 0)), st_spec],
        out_shape=[jax.ShapeDtypeStruct((m_total, GLA_VW), BF16),
                   jax.ShapeDtypeStruct((depth, nseq, GLA_HEADS, GLA_DK, GLA_DV), F32)],
        scratch_shapes=[pltpu.VMEM((rb, GLA_VW), F32)],
        input_output_aliases=alias,
        compiler_params=_cp(("parallel", "arbitrary")),
        name="gla",
    )(*args, *al_arrays)


def _rwkv_body(*refs, names, bb, ch, first, zero_init):
    r = dict(zip(names, refs))
    st_ref, carry_ref = r["st"], r["carry"]
    chained = bb == 1
    nr = bb * ch
    n2 = 2 * nr
    npair = RW_HEADS // 2
    c_id = pl.program_id(1)
    last_chunk = c_id == pl.num_programs(1) - 1
    lane = lax.broadcasted_iota(jnp.int32, (nr, LANES), 1)
    head_a = lane < RW_N
    r2 = lax.broadcasted_iota(jnp.int32, (n2, n2), 0)
    c2 = lax.broadcasted_iota(jnp.int32, (n2, n2), 1)
    same_blk = (r2 // ch) == (c2 // ch)
    strict = same_blk & (c2 < r2)
    incl = same_blk & (c2 <= r2)
    eye = (r2 == c2).astype(F32)
    blockdiag = (lax.broadcasted_iota(jnp.int32, (LANES, LANES), 0) < RW_N) == \
                (lax.broadcasted_iota(jnp.int32, (LANES, LANES), 1) < RW_N)
    zeros64 = jnp.zeros((RW_N, RW_N), F32)

    @pl.when(c_id == 0)
    def _():
        if chained:
            carry_ref[...] = r["sh"][...]
        for s in range(bb):
            for p in range(npair):
                if zero_init:
                    st_ref[s, p] = jnp.zeros((LANES, LANES), F32)
                else:
                    top = jnp.concatenate([r["s0"][s, 2 * p], zeros64], axis=1)
                    bot = jnp.concatenate([zeros64, r["s0"][s, 2 * p + 1]], axis=1)
                    st_ref[s, p] = jnp.concatenate([top, bot], axis=0)

    rowpos = lax.broadcasted_iota(jnp.int32, (nr, RW_W), 0) % ch
    sections = (("pr", 0), ("pk", RW_W), ("pv", 2 * RW_W), ("ps", 3 * RW_W))

    def shifted(name, off):
        p = r[name][...]
        prev = pltpu.roll(p, 1, axis=0)
        if chained:
            first_rows = jnp.broadcast_to(carry_ref[:, off:off + RW_W], (ch, RW_W))
        else:
            first_rows = r["sh"][:, off:off + RW_W]
        prev = jnp.where(rowpos == 0, first_rows, prev)
        return p, p + r["mu"][:, off:off + RW_W] * (prev - p)

    raw, feat = {}, {}
    for name, off in sections:
        raw[name], feat[name] = shifted(name, off)
    if chained:
        for name, off in sections:
            carry_ref[:, off:off + RW_W] = raw[name][ch - 1:ch, :]

    @pl.when(last_chunk)
    def _():
        for name, off in sections:
            for s in range(bb):
                last = s * ch + ch - 1
                r["sh_out"][s, :, off:off + RW_W] = raw[name][last:last + 1, :]

    f_r, f_k, f_v, f_s = feat["pr"], feat["pk"], feat["pv"], feat["ps"]
    xw = f_s[:, S_XW:S_XW + LANES]
    xa = f_s[:, S_XA:S_XA + LANES]
    xg = f_s[:, S_XG:S_XG + 2 * LANES]
    w_log = -_softplus(-(r["w0"][...] + _bdot(jnp.tanh(xw), r["w2"][...]))) - 0.5
    lw = -jnp.exp(w_log)
    a = _sigmoid(r["a0"][...] + _bdot(xa, r["a2"][...]))
    gate = _bdot(_sigmoid(xg), r["g2"][...])
    if first:
        vr = f_v
        r["vf_out"][...] = vr
    else:
        nu = _sigmoid(r["v0"][...] + _bdot(raw["ps"][:, S_V1:S_V1 + LANES], r["v2"][...]))
        vr = f_v + (r["vf"][...] - f_v) * nu
    kk = f_k * r["kkw"][...]
    kmod = f_k * (1.0 + (a - 1.0) * r["kaw"][...])
    g_in = _cumsum_rows(lw, ch)
    g_ex = g_in - lw
    g_tail = _suffix_sum_rows(lw, ch)
    e_in, e_ex, e_neg, e_tail = jnp.exp(g_in), jnp.exp(g_ex), jnp.exp(-g_in), jnp.exp(g_tail)
    e_end = jnp.exp(g_in + g_tail)

    def stack(x):
        return jnp.concatenate([jnp.where(head_a, x, 0.0), jnp.where(head_a, 0.0, x)], axis=0)

    def unstack(x):
        return x[:nr] + x[nr:]

    def pair_sum(x):
        sa = jnp.sum(jnp.where(head_a, x, 0.0), axis=-1, keepdims=True)
        sb = jnp.sum(jnp.where(head_a, 0.0, x), axis=-1, keepdims=True)
        return jnp.where(head_a, sa, sb)

    def seq_rows(s):
        return slice(s * ch, (s + 1) * ch)

    for p0 in range(0, npair, RW_PAIR_GROUP):
        group = range(p0, p0 + RW_PAIR_GROUP)
        q = {p: {} for p in group}
        for p in group:
            ls = slice(p * LANES, (p + 1) * LANES)
            d = q[p]
            kk_p = kk[:, ls]
            kap = kk_p / jnp.maximum(jnp.sqrt(pair_sum(kk_p * kk_p)), 1e-12)
            d["r"], d["k"], d["v"] = f_r[:, ls], kmod[:, ls], vr[:, ls]
            d["kap_t"] = kap * e_ex[:, ls]
            d["r_t"] = d["r"] * e_in[:, ls]
            ba = kap * a[:, ls]
            d["b_h"] = ba * e_tail[:, ls]
            d["k_h"] = d["k"] * e_tail[:, ls]
            lhs4 = jnp.concatenate([stack(d["kap_t"]), stack(d["r_t"])], axis=0)
            rhs4 = jnp.concatenate([stack(ba * e_neg[:, ls]), stack(d["k"] * e_neg[:, ls])], axis=0)
            m4 = _bdot_nt(lhs4, rhs4)
            d["n"] = jnp.where(strict, m4[:n2, :n2], 0.0)
            d["a_ak"] = jnp.where(strict, m4[:n2, n2:], 0.0)
            d["a_r"] = jnp.concatenate([jnp.where(incl, m4[n2:, :n2], 0.0), jnp.where(incl, m4[n2:, n2:], 0.0)], axis=1)
            d["dinv"] = eye
            d["v_st"] = stack(d["v"])
        for p in group:
            d = q[p]
            ks_parts, rs_parts = [], []
            for s in range(bb):
                both = _bdot_nt(jnp.concatenate([d["kap_t"][seq_rows(s)], d["r_t"][seq_rows(s)]], axis=0), st_ref[s, p])
                ks_parts.append(both[:ch])
                rs_parts.append(both[ch:])
            d["ks"] = ks_parts[0] if bb == 1 else jnp.concatenate(ks_parts, axis=0)
            d["rs"] = rs_parts[0] if bb == 1 else jnp.concatenate(rs_parts, axis=0)
            d["rhs"] = stack(d["ks"]) + _bdot(d["a_ak"], d["v_st"])
        m = 1
        while m < ch:
            low = same_blk & ((r2 // (2 * m)) == (c2 // (2 * m))) & (((r2 // m) % 2) == 1) & (((c2 // m) % 2) == 0)
            tmp = {p: _bdot(q[p]["dinv"], jnp.where(low, q[p]["n"], 0.0)) for p in group}
            for p in group:
                q[p]["dinv"] = q[p]["dinv"] - _bdot(tmp[p], q[p]["dinv"])
            m *= 2
        for p in group:
            d = q[p]
            d["u_st"] = -_bdot(d["dinv"], d["rhs"])
        for p in group:
            d = q[p]
            ls = slice(p * LANES, (p + 1) * LANES)
            y = unstack(stack(d["rs"]) + _bdot(d["a_r"], jnp.concatenate([d["u_st"], d["v_st"]], axis=0)))
            u_p = unstack(d["u_st"])
            for s in range(bb):
                rows = seq_rows(s)
                upd = _bdot_tn(jnp.concatenate([u_p[rows], d["v"][rows]], axis=0),
                               jnp.concatenate([d["b_h"][rows], d["k_h"][rows]], axis=0))
                st_ref[s, p] = st_ref[s, p] * e_end[s * ch:s * ch + 1, ls] + jnp.where(blockdiag, upd, 0.0)
            mean = pair_sum(y) * (1.0 / RW_N)
            yc = y - mean
            var = pair_sum(yc * yc) * (1.0 / RW_N)
            yn = yc * lax.rsqrt(var + RW_GN_EPS) * r["lnw"][:, ls] + r["lnb"][:, ls]
            bonus = pair_sum(d["r"] * d["k"] * r["rk"][:, ls]) * d["v"]
            r["o"][:, ls] = ((yn + bonus) * gate[:, ls]).astype(r["o"].dtype)

    @pl.when(last_chunk)
    def _():
        for s in range(bb):
            for p in range(npair):
                st = st_ref[s, p]
                r["s_out"][s, 2 * p] = st[:RW_N, :RW_N]
                r["s_out"][s, 2 * p + 1] = st[RW_N:, RW_N:]


def _rwkv(proj, vf, shift, s0, o_buf, st_buf, wts, l, *, depth, m_total, r0, nseq, t, ch, bb):
    first = vf is None
    rb = bb * ch
    nc = t // ch
    assert rb == RW_SLAB and (bb == 1 or nc == 1) and r0 % rb == 0
    small_off = proj.shape[1] - SMALL_W

    def rblk(ib, c):
        return r0 // rb + ib * nc + c

    def sec(off):
        return pl.BlockSpec((rb, RW_W), lambda ib, c: (rblk(ib, c), off // RW_W))

    def vec(width, dl=0):
        return pl.BlockSpec((None, 1, width), lambda ib, c: (l - dl, 0, 0))

    def mat(rows, dl=0):
        return pl.BlockSpec((None, rows, RW_W), lambda ib, c: (l - dl, 0, 0))

    group_rows = pl.BlockSpec((rb, RW_W), lambda ib, c: (ib * nc + c, 0))
    st_spec = pl.BlockSpec((None, bb, RW_HEADS, RW_N, RW_N), lambda ib, c: (l, ib, 0, 0, 0))
    sh_spec = pl.BlockSpec((bb, 1, SHIFT_W), lambda ib, c: (ib, 0, 0))
    names = ["pr", "pk", "pv", "ps", "mu", "sh", "w0", "w2", "a0", "a2", "g2", "kkw", "kaw", "rk", "lnw", "lnb"]
    args = [proj, proj, proj, proj, wts["mu"], shift, wts["w0"], wts["w2"], wts["a0"], wts["a2"], wts["g2"],
            wts["kk"], wts["ka"], wts["rk"], wts["lnw"], wts["lnb"]]
    sh_in = (pl.BlockSpec((None, 1, SHIFT_W), lambda ib, c: (ib, 0, 0)) if bb == 1
             else pl.BlockSpec((rb, SHIFT_W), lambda ib, c: (ib, 0)))
    in_specs = [sec(C_RR), sec(C_RK), sec(C_RV), sec(small_off), vec(SHIFT_W), sh_in,
                vec(RW_W), mat(LANES), vec(RW_W), mat(LANES), mat(2 * LANES)] + [vec(RW_W)] * 5
    if not first:
        names += ["vf", "v0", "v2"]
        args += [vf, wts["v0"], wts["v2"]]
        in_specs += [group_rows, vec(RW_W, 1), mat(LANES, 1)]
    if s0 is not None:
        names.append("s0")
        args.append(s0)
        in_specs.append(st_spec)
    al_arrays, al_specs, alias = _alias_inputs([o_buf, st_buf])
    alias = {len(args) + k: v for k, v in alias.items()}
    names += ["alias%d" % i for i in range(len(al_arrays))] + ["o", "s_out", "sh_out"]
    out_specs = [pl.BlockSpec((rb, RW_W), lambda ib, c: (rblk(ib, c), 0)), st_spec, sh_spec]
    out_shape = [jax.ShapeDtypeStruct((m_total, RW_W), BF16),
                 jax.ShapeDtypeStruct((depth, nseq, RW_HEADS, RW_N, RW_N), F32),
                 jax.ShapeDtypeStruct((nseq, 1, SHIFT_W), F32)]
    if first:
        names.append("vf_out")
        out_specs.append(group_rows)
        out_shape.append(jax.ShapeDtypeStruct((nseq * t, RW_W), F32))
    names += ["st", "carry"]
    res = pl.pallas_call(
        functools.partial(_rwkv_body, names=names, bb=bb, ch=ch, first=first,
                          zero_init=s0 is None),
        grid=(nseq // bb, nc),
        in_specs=in_specs + al_specs,
        out_specs=out_specs,
        out_shape=out_shape,
        scratch_shapes=[pltpu.VMEM((bb, RW_HEADS // 2, LANES, LANES), F32), pltpu.VMEM((1, SHIFT_W), F32)],
        input_output_aliases=alias,
        compiler_params=_cp(("parallel", "arbitrary")),
        name="rwkv7",
    )(*args, *al_arrays)
    return res[0], res[1], res[2], (res[3] if first else None)


def _xattn_body(*refs, names, bb, tq):
    r = dict(zip(names, refs))

    def kv(ref, s, h):
        return ref[s, :, h * X_DH:(h + 1) * X_DH]

    units = [(s, h) for s in range(bb) for h in range(X_HEADS)]
    pr = {}
    for s, h in units:
        sc = _bdot_nt(r["q"][s * tq:(s + 1) * tq, h * X_DH:(h + 1) * X_DH], kv(r["k"], s, h)) * (X_DH ** -0.5)
        e = jnp.exp(sc - jnp.max(sc, axis=-1, keepdims=True))
        pr[s, h] = e / jnp.sum(e, axis=-1, keepdims=True)
    for s, h in units:
        r["of"][s * tq:(s + 1) * tq, h * X_DH:(h + 1) * X_DH] = _bdot(pr[s, h], kv(r["v"], s, h))
    r["o"][...] = r["of"][...].astype(r["o"].dtype)


def _xattn(proj, mem_k, mem_v, o_buf, l, *, m_total, r0, nseq, t, tq, bb):
    rb = bb * tq
    nq = t // tq
    assert (bb == 1 or nq == 1) and r0 % rb == 0
    kv_spec = pl.BlockSpec((None, bb, MEM_LEN, X_W), lambda ib, c: (l, ib, 0, 0))
    rows = lambda ib, c: r0 // rb + ib * nq + c
    al_arrays, al_specs, alias = _alias_inputs([o_buf])
    alias = {3 + k: v for k, v in alias.items()}
    names = ["q", "k", "v"] + ["alias%d" % i for i in range(len(al_arrays))] + ["o", "of"]
    return pl.pallas_call(
        functools.partial(_xattn_body, names=names, bb=bb, tq=tq),
        grid=(nseq // bb, nq),
        in_specs=[pl.BlockSpec((rb, X_W), lambda ib, c: (rows(ib, c), C_XQ // X_W)), kv_spec, kv_spec] + al_specs,
        out_specs=pl.BlockSpec((rb, X_W), lambda ib, c: (rows(ib, c), 0)),
        out_shape=jax.ShapeDtypeStruct((m_total, X_W), BF16),
        scratch_shapes=[pltpu.VMEM((rb, X_W), F32)],
        input_output_aliases=alias,
        compiler_params=_cp(("parallel", "arbitrary")),
        name="mem_xattn",
    )(proj, mem_k, mem_v, *al_arrays)


def _xattn_cache_body(q_ref, k_hbm, v_hbm, alias_ref, o_ref, of_ref, kbuf, vbuf, sem, *, l, bb, tq):
    del alias_ref
    i = pl.program_id(0)
    n = pl.num_programs(0)

    def copies(step, slot):
        out = []
        for h in range(X_HEADS):
            for t, (src, dst) in enumerate(((k_hbm, kbuf), (v_hbm, vbuf))):
                out.append(pltpu.make_async_copy(src.at[l, pl.ds(step * bb, bb), :, h, :], dst.at[slot, h],
                                                 sem.at[slot, t, h]))
        return out

    @pl.when(i == 0)
    def _():
        for c in copies(0, 0):
            c.start()

    @pl.when(i + 1 < n)
    def _():
        for c in copies(i + 1, (i + 1) % 2):
            c.start()

    slot = i % 2
    for c in copies(i, slot):
        c.wait()

    grp = max(1, SUBLANES // tq)
    ur = grp * tq
    useq = lax.broadcasted_iota(jnp.int32, (ur, 1), 0) // tq
    units = [(u, h) for u in range(bb // grp) for h in range(X_HEADS)]
    pr = {}
    for u, h in units:
        qu = q_ref[u * ur:(u + 1) * ur, h * X_DH:(h + 1) * X_DH]
        for g in range(grp):
            sc = _bdot_nt(qu, kbuf[slot, h, u * grp + g]) * (X_DH ** -0.5)
            e = jnp.exp(sc - jnp.max(sc, axis=-1, keepdims=True))
            pr[u, h, g] = e / jnp.sum(e, axis=-1, keepdims=True)
    for u, h in units:
        o = _bdot(pr[u, h, 0], vbuf[slot, h, u * grp])
        for g in range(1, grp):
            o = jnp.where(useq == g, _bdot(pr[u, h, g], vbuf[slot, h, u * grp + g]), o)
        of_ref[u * ur:(u + 1) * ur, h * X_DH:(h + 1) * X_DH] = o
    o_ref[...] = of_ref[...].astype(o_ref.dtype)


def _xattn_cache(proj, cache_k, cache_v, o_buf, l, *, r0, nseq, tq, bb):
    rb = bb * tq
    assert r0 % rb == 0 and nseq % bb == 0
    rows = lambda ib: (r0 // rb + ib, 0)
    any_spec = pl.BlockSpec(memory_space=pl.ANY)
    buf = pltpu.VMEM((2, X_HEADS, bb, MEM_LEN, X_DH), F32)
    return pl.pallas_call(
        functools.partial(_xattn_cache_body, l=l, bb=bb, tq=tq),
        grid=(nseq // bb,),
        in_specs=[pl.BlockSpec((rb, X_W), lambda ib: (r0 // rb + ib, C_XQ // X_W)), any_spec, any_spec, any_spec],
        out_specs=pl.BlockSpec((rb, X_W), rows),
        out_shape=jax.ShapeDtypeStruct(o_buf.shape, o_buf.dtype),
        scratch_shapes=[pltpu.VMEM((rb, X_W), F32), buf, buf, pltpu.SemaphoreType.DMA((2, 2, X_HEADS))],
        input_output_aliases={3: 0},
        compiler_params=_cp(("arbitrary",)),
        name="cache_xattn",
    )(proj, cache_k, cache_v, o_buf)


def _pad_last(a, width):
    return jnp.pad(a, [(0, 0)] * (a.ndim - 1) + [(0, width - a.shape[-1])])


def _pad_rows(a, rows):
    return jnp.pad(a, [(0, 0)] * (a.ndim - 2) + [(0, rows - a.shape[-2]), (0, 0)])


def _small_section(ga, xw, xa, xg, v1):
    return jnp.concatenate([_pad_last(ga, S_XW - S_GA), _pad_last(xw, S_XA - S_XW), _pad_last(xa, S_XG - S_XA),
                            _pad_last(xg, S_V1 - S_XG), _pad_last(v1, SMALL_W - S_V1)], axis=-1)


def _small_weight_t(w_t, v1_t):
    sm = GLA_COLS + 3 * RW_W
    xq = GLA_COLS + RW_COLS
    pieces = ((S_GA, w_t[:, 3072:GLA_COLS]), (S_XW, w_t[:, sm:sm + RW_DECAY_R]),
              (S_XA, w_t[:, sm + RW_DECAY_R:sm + RW_DECAY_R + RW_A_R]),
              (S_XG, w_t[:, sm + RW_DECAY_R + RW_A_R:xq]), (S_V1, v1_t))
    rows = lax.optimization_barrier(tuple(p for _, p in pieces))
    out = jnp.zeros((w_t.shape[0], SMALL_W, w_t.shape[2]), BF16)
    for (off, _), piece in zip(pieces, rows):
        out = lax.dynamic_update_slice(out, piece.astype(BF16), (0, off, 0))
    return out


def _shift_layout(a):
    z16 = jnp.zeros(a.shape[:-1] + (GLA_RANK,), a.dtype)
    z32 = jnp.zeros(a.shape[:-1] + (RW_V_R,), a.dtype)
    o = 3 * RW_W
    small = _small_section(z16, a[..., o:o + RW_DECAY_R], a[..., o + RW_DECAY_R:o + RW_DECAY_R + RW_A_R],
                           a[..., o + RW_DECAY_R + RW_A_R:], z32)
    return jnp.concatenate([a[..., :o], small], axis=-1)


def _shift_unlayout(a):
    o = 3 * RW_W
    return jnp.concatenate([a[..., :o], a[..., o + S_XW:o + S_XW + RW_DECAY_R], a[..., o + S_XA:o + S_XA + RW_A_R],
                            a[..., o + S_XG:o + S_XG + RW_G_R]], axis=-1)


def kernel(x_prompt, x_sample, mem_prompt, state_gla, state_rwkv, state_rwkv_shift, cache_mem_k, cache_mem_v,
           g_norm1, w_in, gla_wa2, gla_ba2, gla_onorm, rw_mu, rw_w0, rw_w2, rw_a0, rw_a2, rw_g2, rw_kk, rw_ka,
           rw_rk, rw_lnx_w, rw_lnx_b, rw_v0, rw_v1, rw_v2, g_mem, w_mk, w_mv, w_branch, w_out, g_norm2,
           w_ff_gate, w_ff_up, w_ff_down, g_final):
    depth = w_in.shape[0]
    bp, tp, d = x_prompt.shape
    bs, ts, _ = x_sample.shape
    assert d % 1024 == 0 and ts in (1, 2, 4, 8) and tp % RW_SLAB == 0 and bs % (RW_SLAB // ts) == 0
    mp, ms = bp * tp, bs * ts
    m = mp + ms
    f = w_ff_gate.shape[-1]

    v1_all = jnp.concatenate([jnp.zeros((1, d, RW_V_R), F32), rw_v1], axis=0)
    w_t = jnp.swapaxes(w_in, 1, 2)
    w_small_t = _small_weight_t(w_t, jnp.swapaxes(v1_all, 1, 2)).astype(BF16)
    r3 = lambda a: a.reshape(a.shape[0], 1, -1)
    rw_wts = dict(mu=r3(_shift_layout(rw_mu)), w0=r3(rw_w0), w2=_pad_rows(rw_w2, LANES), a0=r3(rw_a0),
                  a2=_pad_rows(rw_a2, LANES), g2=_pad_rows(rw_g2, 2 * LANES), kk=r3(rw_kk), ka=r3(rw_ka),
                  rk=r3(rw_rk), lnw=r3(rw_lnx_w), lnb=r3(rw_lnx_b), v0=r3(rw_v0), v2=_pad_rows(rw_v2, LANES))
    wa2p = _pad_rows(gla_wa2, LANES)
    ba2, onorm = r3(gla_ba2), r3(gla_onorm)
    g1, g2n, gm = r3(g_norm1), r3(g_norm2), r3(g_mem)

    x = jnp.concatenate([x_prompt.reshape(mp, d), x_sample.reshape(ms, d)], axis=0)

    tm = _pick(m, 1152, 4 * SUBLANES)
    tm_wide = _pick(m, 2 * tm, 4 * SUBLANES)
    tn_d = _pick(d, 512, LANES)
    tn_f = _pick(f, 512, LANES)

    mem_rows = mem_prompt.reshape(bp * MEM_LEN, d)
    tmm = _pick(bp * MEM_LEN, 1024, SUBLANES)
    p_mem_k = jnp.stack([_norm_mm(mem_rows, gm, w_mk, l, tm=tmm, tn=tn_d, name="mem_k") for l in range(depth)])
    p_mem_v = jnp.stack([_norm_mm(mem_rows, gm, w_mv, l, tm=tmm, tn=tn_d, name="mem_v") for l in range(depth)])
    p_mem_k = p_mem_k.reshape(depth, bp, MEM_LEN, X_W)
    p_mem_v = p_mem_v.reshape(depth, bp, MEM_LEN, X_W)

    sh0 = jnp.zeros((bp, 1, SHIFT_W), F32)
    s_shift_in = jnp.repeat(_shift_layout(state_rwkv_shift), ts, axis=1)

    bb_s = RW_SLAB // ts
    tq_p = _pick(tp, 512, SUBLANES)
    common = dict(depth=depth, m_total=m)
    prompt = dict(r0=0, nseq=bp, t=tp)
    sample = dict(r0=mp, nseq=bs, t=ts)
    vf_p = vf_s = None
    pg_all = sg_all = pr_all = sr_all = None
    p_shift, s_shift = [], []
    for l in range(depth):
        proj, gates = _in_proj(x, g1, w_t, w_small_t, l, tm=tm)

        o_gla, pg_all = _gla(proj, wa2p, ba2, onorm, None, None, pg_all, l, **common, **prompt,
                             ch=GLA_CHUNK, bb=1)
        o_gla, sg_all = _gla(proj, wa2p, ba2, onorm, state_gla, o_gla, sg_all, l, **common, **sample,
                             ch=ts, bb=bb_s)
        o_rw, pr_all, sh_p, vf_new_p = _rwkv(proj, vf_p, sh0, None, None, pr_all, rw_wts, l, **common, **prompt,
                                             ch=RW_SLAB, bb=1)
        o_rw, sr_all, sh_s, vf_new_s = _rwkv(proj, vf_s, s_shift_in[l], state_rwkv, o_rw, sr_all, rw_wts, l,
                                             **common, **sample, ch=ts, bb=bb_s)
        if l == 0:
            vf_p, vf_s = vf_new_p, vf_new_s
        o_x = _xattn(proj, p_mem_k, p_mem_v, None, l, m_total=m, **prompt, tq=tq_p, bb=1)
        o_x = _xattn_cache(proj, cache_mem_k, cache_mem_v, o_x, l, r0=mp, nseq=bs, tq=ts, bb=XC_SEQS)

        merged = _merge(o_gla, o_rw, o_x, gates, w_branch, l, tm=tm_wide, tn=_pick(d, 256, LANES))
        x = _mm_res(merged, w_out, l, x, tm=tm_wide, tn=tn_d, name="out_proj")
        act = _ffn_up(x, g2n, w_ff_gate, w_ff_up, l, tm=tm, tn=tn_f)
        x = _mm_res(act, w_ff_down, l, x, tm=tm, tn=_pick(d, 256, LANES), name="ffn_down")
        p_shift.append(_shift_unlayout(sh_p[:, 0]))
        s_shift.append(_shift_unlayout(sh_s[:, 0]))

    y_prompt = _final_norm(x, g_final.reshape(1, d), 0, mp).reshape(bp, tp, d)
    y_sample = _final_norm(x, g_final.reshape(1, d), mp, ms).reshape(bs, ts, d)
    return (y_prompt, y_sample, pg_all, pr_all, jnp.stack(p_shift),
            p_mem_k.reshape(depth, bp, MEM_LEN, X_HEADS, X_DH), p_mem_v.reshape(depth, bp, MEM_LEN, X_HEADS, X_DH),
            sg_all, sr_all, jnp.stack(s_shift))
```

```python
import functools

import jax
import jax.numpy as jnp
from jax import lax
from jax.experimental import pallas as pl
from jax.experimental.pallas import tpu as pltpu

F32 = jnp.float32
BF16 = jnp.bfloat16

GLA_HEADS, GLA_DK, GLA_DV = 4, 128, 256
GLA_KW, GLA_VW, GLA_RANK = GLA_HEADS * GLA_DK, GLA_HEADS * GLA_DV, 16
GLA_GATE_NORM = 16.0
RW_HEADS, RW_N = 16, 64
RW_W = RW_HEADS * RW_N
RW_DECAY_R, RW_A_R, RW_V_R, RW_G_R = 64, 64, 32, 160
RW_GN_EPS = 64e-5
MEM_LEN, X_HEADS, X_DH = 256, 4, 256
X_W = X_HEADS * X_DH
N_BRANCH = 3
NORM_EPS = 1e-6
GLA_COLS = 2 * GLA_KW + 2 * GLA_VW + GLA_RANK
RW_COLS = 3 * RW_W + RW_DECAY_R + RW_A_R + RW_G_R

LANES = 128
SUBLANES = 8
VMEM_LIMIT_BYTES = 56 * 1024 * 1024

C_GQ, C_GK, C_GV, C_GG = 0, 512, 1024, 2048
C_RR, C_RK, C_RV, C_XQ, C_SMALL = 3072, 4096, 5120, 6144, 7168
SMALL_W = 1024
IN_TILE = 1024
S_GA, S_XW, S_XA, S_XG, S_V1 = 0, 128, 256, 384, 640
SHIFT_W = 3 * RW_W + SMALL_W

RW_SLAB = 64
XC_SEQS = 8
RW_PAIR_GROUP = 8
GLA_CHUNK = 64


def _cp(sem):
    return pltpu.CompilerParams(dimension_semantics=sem, vmem_limit_bytes=VMEM_LIMIT_BYTES)


def _pick(n, target, mult):
    best = None
    for d in range(mult, min(n, target) + 1, mult):
        if n % d == 0:
            best = d
    assert best is not None, (n, target, mult)
    return best


def _bdot(a, b):
    return jnp.dot(a.astype(BF16), b.astype(BF16), preferred_element_type=F32)


def _bdot_nt(a, b):
    return lax.dot_general(a.astype(BF16), b.astype(BF16), (((1,), (1,)), ((), ())), preferred_element_type=F32)


def _bdot_tn(a, b):
    return lax.dot_general(a.astype(BF16), b.astype(BF16), (((0,), (0,)), ((), ())), preferred_element_type=F32)


def _split3(x):
    hi = x.astype(BF16)
    r1 = x - hi.astype(F32)
    mid = r1.astype(BF16)
    lo = (r1 - mid.astype(F32)).astype(BF16)
    return hi, mid, lo


def _softplus(y):
    return jnp.maximum(y, 0.0) + jnp.log(1.0 + jnp.exp(-jnp.abs(y)))


def _sigmoid(y):
    return 1.0 / (1.0 + jnp.exp(-y))


def _cumsum_rows(x, block):
    pos = lax.broadcasted_iota(jnp.int32, x.shape, 0) % block
    d = 1
    while d < block:
        x = x + jnp.where(pos >= d, pltpu.roll(x, d, axis=0), 0.0)
        d *= 2
    return x


def _suffix_sum_rows(x, block):
    n = x.shape[0]
    pos = lax.broadcasted_iota(jnp.int32, x.shape, 0) % block
    y = x
    d = 1
    while d < block:
        y = y + jnp.where(pos + d < block, pltpu.roll(y, n - d, axis=0), 0.0)
        d *= 2
    return y - x


def _alias_inputs(bufs):
    arrays, specs, alias = [], [], {}
    for out_idx, buf in enumerate(bufs):
        if buf is not None:
            alias[len(arrays)] = out_idx
            arrays.append(buf)
            specs.append(pl.BlockSpec(memory_space=pl.ANY))
    return arrays, specs, alias


def _rms_rows(x_ref, g_ref, h_ref, rows, eps):
    n = x_ref.shape[0] // rows

    def body(r, c):
        sl = pl.ds(pl.multiple_of(r * rows, rows), rows)
        x = x_ref[sl, :]
        ms = jnp.mean(x * x, axis=-1, keepdims=True)
        h_ref[sl, :] = (x * lax.rsqrt(ms + eps) * g_ref[...]).astype(h_ref.dtype)
        return c

    lax.fori_loop(0, n, body, 0)


def _norm_mm_body(x_ref, g_ref, w_ref, o_ref, h_ref, *, rows):
    @pl.when(pl.program_id(1) == 0)
    def _():
        _rms_rows(x_ref, g_ref, h_ref, rows, NORM_EPS)

    o_ref[...] = jnp.dot(h_ref[...], w_ref[...].astype(BF16), preferred_element_type=F32).astype(o_ref.dtype)


def _norm_mm(x, g3, w3, l, *, tm, tn, name):
    m, d = x.shape
    n = w3.shape[-1]
    return pl.pallas_call(
        functools.partial(_norm_mm_body, rows=_pick(tm, 128, SUBLANES)),
        grid=(m // tm, n // tn),
        in_specs=[pl.BlockSpec((tm, d), lambda i, j: (i, 0)),
                  pl.BlockSpec((None, 1, d), lambda i, j: (l, 0, 0)),
                  pl.BlockSpec((None, d, tn), lambda i, j: (l, 0, j))],
        out_specs=pl.BlockSpec((tm, tn), lambda i, j: (i, j)),
        out_shape=jax.ShapeDtypeStruct((m, n), F32),
        scratch_shapes=[pltpu.VMEM((tm, d), BF16)],
        compiler_params=_cp(("parallel", "arbitrary")),
        name=name,
    )(x, g3, w3)


def _in_proj_body(x_ref, g_ref, wm_ref, ws_ref, o_ref, og_ref, h_ref, *, rows, n_main, n_small):
    j = pl.program_id(1)

    @pl.when(j == 0)
    def _():
        _rms_rows(x_ref, g_ref, h_ref, rows, NORM_EPS)

    @pl.when(j < n_main)
    def _():
        o_ref[...] = _bdot_nt(h_ref[...], wm_ref[0])

    @pl.when((j >= n_main) & (j < n_main + n_small))
    def _():
        o_ref[...] = _bdot_nt(h_ref[...], ws_ref[...])

    @pl.when(j >= n_main + n_small)
    def _():
        og_ref[...] = _sigmoid(_bdot_nt(h_ref[...], wm_ref[0])).astype(og_ref.dtype)


def _in_proj(x, g3, w_t, w_small_t, l, *, tm):
    m, d = x.shape
    tn = IN_TILE
    n_main, n_small = C_SMALL // tn, SMALL_W // tn
    xq0 = GLA_COLS + RW_COLS
    n_gla, n_rw = C_RR // tn, (C_XQ - C_RR) // tn
    n_gate = (w_t.shape[1] - xq0 - X_W) // tn

    def w_rows(i, j):
        jm = jnp.where(j < n_main, j, jnp.maximum(j - n_small, n_main - 1))
        start = jnp.where(jm < n_gla, jm * tn,
                          jnp.where(jm < n_gla + n_rw, GLA_COLS + (jm - n_gla) * tn, xq0 + (jm - n_gla - n_rw) * tn))
        return (l, pl.multiple_of(start, 2 * SUBLANES), 0)

    return pl.pallas_call(
        functools.partial(_in_proj_body, rows=_pick(tm, 128, SUBLANES), n_main=n_main, n_small=n_small),
        grid=(m // tm, n_main + n_small + n_gate),
        in_specs=[pl.BlockSpec((tm, d), lambda i, j: (i, 0), pipeline_mode=pl.Buffered(1)),
                  pl.BlockSpec((None, 1, d), lambda i, j: (l, 0, 0)),
                  pl.BlockSpec((pl.Element(1), pl.Element(tn), pl.Element(d)), w_rows),
                  pl.BlockSpec((None, tn, d), lambda i, j: (l, jnp.clip(j - n_main, 0, n_small - 1), 0),
                               pipeline_mode=pl.Buffered(1))],
        out_specs=[pl.BlockSpec((tm, tn), lambda i, j: (i, jnp.minimum(j, n_main + n_small - 1))),
                   pl.BlockSpec((tm, tn), lambda i, j: (i, jnp.maximum(j - n_main - n_small, 0)))],
        out_shape=[jax.ShapeDtypeStruct((m, (n_main + n_small) * tn), F32),
                   jax.ShapeDtypeStruct((m, n_gate * tn), BF16)],
        scratch_shapes=[pltpu.VMEM((tm, d), BF16)],
        compiler_params=_cp(("parallel", "arbitrary")),
        name="in_proj",
    )(x, g3, w_t, w_small_t)


def _merge_body(og_ref, or_ref, ox_ref, gg_ref, gr_ref, gx_ref, w_ref, o_ref):
    acc = gg_ref[...].astype(F32) * _bdot(og_ref[...], w_ref[0])
    acc += gr_ref[...].astype(F32) * _bdot(or_ref[...], w_ref[1])
    acc += gx_ref[...].astype(F32) * _bdot(ox_ref[...], w_ref[2])
    o_ref[...] = acc.astype(o_ref.dtype)


def _merge(o_gla, o_rw, o_x, gates, w_branch, l, *, tm, tn):
    m, bw = o_gla.shape
    d = w_branch.shape[-1]
    gate_blk = [b * d // tn for b in range(N_BRANCH)]
    o_spec = pl.BlockSpec((tm, bw), lambda i, j: (i, 0))
    return pl.pallas_call(
        _merge_body,
        grid=(m // tm, d // tn),
        in_specs=[o_spec, o_spec, o_spec]
        + [pl.BlockSpec((tm, tn), functools.partial(lambda i, j, off: (i, off + j), off=gate_blk[b]))
           for b in range(N_BRANCH)]
        + [pl.BlockSpec((None, N_BRANCH, bw, tn), lambda i, j: (l, 0, 0, j))],
        out_specs=pl.BlockSpec((tm, tn), lambda i, j: (i, j)),
        out_shape=jax.ShapeDtypeStruct((m, d), BF16),
        compiler_params=_cp(("parallel", "arbitrary")),
        name="branch_merge",
    )(o_gla, o_rw, o_x, gates, gates, gates, w_branch)


def _mm_res_body(a_ref, w_ref, r_ref, o_ref):
    o_ref[...] = r_ref[...] + _bdot(a_ref[...], w_ref[...])


def _mm_res(a, w3, l, res, *, tm, tn, name):
    m, kd = a.shape
    n = w3.shape[-1]
    return pl.pallas_call(
        _mm_res_body,
        grid=(m // tm, n // tn),
        in_specs=[pl.BlockSpec((tm, kd), lambda i, j: (i, 0)),
                  pl.BlockSpec((None, kd, tn), lambda i, j: (l, 0, j)),
                  pl.BlockSpec((tm, tn), lambda i, j: (i, j))],
        out_specs=pl.BlockSpec((tm, tn), lambda i, j: (i, j)),
        out_shape=jax.ShapeDtypeStruct((m, n), F32),
        compiler_params=_cp(("parallel", "arbitrary")),
        name=name,
    )(a, w3, res)


def _ffn_up_body(x_ref, g_ref, wg_ref, wu_ref, o_ref, h_ref, *, rows):
    @pl.when(pl.program_id(1) == 0)
    def _():
        _rms_rows(x_ref, g_ref, h_ref, rows, NORM_EPS)

    h = h_ref[...]
    a = jnp.dot(h, wg_ref[...].astype(BF16), preferred_element_type=F32)
    u = jnp.dot(h, wu_ref[...].astype(BF16), preferred_element_type=F32)
    o_ref[...] = (a * _sigmoid(a) * u).astype(o_ref.dtype)


def _ffn_up(x, g3, wg, wu, l, *, tm, tn):
    m, d = x.shape
    f = wg.shape[-1]
    w_spec = pl.BlockSpec((None, d, tn), lambda i, j: (l, 0, j))
    return pl.pallas_call(
        functools.partial(_ffn_up_body, rows=_pick(tm, 128, SUBLANES)),
        grid=(m // tm, f // tn),
        in_specs=[pl.BlockSpec((tm, d), lambda i, j: (i, 0)),
                  pl.BlockSpec((None, 1, d), lambda i, j: (l, 0, 0)),
                  w_spec, w_spec],
        out_specs=pl.BlockSpec((tm, tn), lambda i, j: (i, j)),
        out_shape=jax.ShapeDtypeStruct((m, f), BF16),
        scratch_shapes=[pltpu.VMEM((tm, d), BF16)],
        compiler_params=_cp(("parallel", "arbitrary")),
        name="ffn_up",
    )(x, g3, wg, wu)


def _final_norm_body(x_ref, g_ref, o_ref):
    x = x_ref[...]
    ms = jnp.mean(x * x, axis=-1, keepdims=True)
    o_ref[...] = x * lax.rsqrt(ms + NORM_EPS) * g_ref[...]


def _final_norm(x, g2, r0, rows):
    d = x.shape[1]
    tm = _pick(rows, 256, SUBLANES)
    assert r0 % tm == 0
    return pl.pallas_call(
        _final_norm_body,
        grid=(rows // tm,),
        in_specs=[pl.BlockSpec((tm, d), lambda i: (r0 // tm + i, 0)), pl.BlockSpec((1, d), lambda i: (0, 0))],
        out_specs=pl.BlockSpec((tm, d), lambda i: (i, 0)),
        out_shape=jax.ShapeDtypeStruct((rows, d), F32),
        compiler_params=_cp(("parallel",)),
        name="final_norm",
    )(x, g2)


def _gla_body(*refs, names, bb, ch, zero_init):
    r = dict(zip(names, refs))
    s_ref = r["s_out"]

    @pl.when(pl.program_id(1) == 0)
    def _():
        if zero_init:
            s_ref[...] = jnp.zeros(s_ref.shape, F32)
        else:
            s_ref[...] = r["s0"][...]

    grp = max(1, SUBLANES // ch)
    ur = grp * ch
    useq = lax.broadcasted_iota(jnp.int32, (ur, 1), 0) // ch
    z = _bdot(r["ga"][...], r["wa2"][...]) + r["ba2"][...]
    la = -_softplus(-z) * (1.0 / GLA_GATE_NORM)
    kk = r["k"][...]
    bc = _cumsum_rows(la, ch)
    tail = _suffix_sum_rows(la, ch)
    q_s = r["q"][...] * (GLA_DK ** -0.5)
    qe = q_s * jnp.exp(bc)
    kl = kk * jnp.exp(tail)
    dec_rows = jnp.exp(bc + tail)

    levels = []
    m = ch // 2
    while m >= SUBLANES:
        levels.append(m)
        m //= 2
    lev_q, lev_k, lev_mask = [], [], []
    if levels:
        assert bb == 1
        row = lax.broadcasted_iota(jnp.int32, (ch, ch), 0)
        col = lax.broadcasted_iota(jnp.int32, (ch, ch), 1)
        bc3 = _split3(bc)
        for m in levels:
            sel = (col == (row // (2 * m)) * (2 * m) + m - 1).astype(BF16)
            ref = sum(jnp.dot(sel, part, preferred_element_type=F32) for part in bc3)
            lev_q.append(q_s * jnp.exp(jnp.minimum(bc - ref, 0.0)))
            lev_k.append(kk * jnp.exp(jnp.minimum(ref - bc, 0.0)))
            lev_mask.append(((row // (2 * m)) == (col // (2 * m))) & (((row // m) % 2) == 1) & (((col // m) % 2) == 0))

    sub = lax.broadcasted_iota(jnp.int32, (SUBLANES, 1), 0)
    tile_seq = sub // min(ch, SUBLANES)

    def tile_attention(rows8):
        q8, k8, b8, v8 = q_s[rows8], kk[rows8], bc[rows8], r["v"][rows8, :]
        acc = [jnp.zeros((SUBLANES, GLA_DV), F32) for _ in range(GLA_HEADS)]
        for j in range(SUBLANES):
            prod = q8 * k8[j:j + 1] * jnp.exp(jnp.minimum(b8 - b8[j:j + 1], 0.0))
            valid = (sub >= j) & (tile_seq == j // min(ch, SUBLANES))
            for h in range(GLA_HEADS):
                a = jnp.sum(prod[:, h * GLA_DK:(h + 1) * GLA_DK], axis=-1, keepdims=True)
                acc[h] = acc[h] + jnp.where(valid, a, 0.0) * v8[j:j + 1, h * GLA_DV:(h + 1) * GLA_DV]
        return acc

    units = [(u, h) for u in range(bb // grp) for h in range(GLA_HEADS)]

    def sl(u, h):
        return (slice(u * ur, (u + 1) * ur), slice(h * GLA_DK, (h + 1) * GLA_DK),
                slice(h * GLA_DV, (h + 1) * GLA_DV))

    def own(q, x):
        return x if grp == 1 else jnp.where(useq == q, x, 0.0)

    att, o_int, o_near, upd = {}, {}, {}, {}
    for u in range(bb // grp):
        tiles = [tile_attention(slice(u * ur + t * SUBLANES, u * ur + (t + 1) * SUBLANES))
                 for t in range(ur // SUBLANES)]
        for h in range(GLA_HEADS):
            o_near[u, h] = tiles[0][h] if len(tiles) == 1 else jnp.concatenate([t[h] for t in tiles], axis=0)
    for u, h in units:
        rows, ks, vs = sl(u, h)
        if levels:
            att[u, h] = sum(jnp.where(mk, _bdot_nt(ql[rows, ks], kl_[rows, ks]), 0.0)
                            for ql, kl_, mk in zip(lev_q, lev_k, lev_mask))
        o_int[u, h] = sum(own(q, _bdot(qe[rows, ks], s_ref[u * grp + q, h])) for q in range(grp))
    for u, h in units:
        rows, ks, vs = sl(u, h)
        vh = r["v"][rows, vs]
        o = o_int[u, h] + o_near[u, h]
        if levels:
            o = o + _bdot(att[u, h], vh)
        for q in range(grp):
            upd[u * grp + q, h] = _bdot_tn(own(q, kl[rows, ks]), vh)
        ms = jnp.mean(o * o, axis=-1, keepdims=True)
        gh = r["g"][rows, vs]
        r["of"][rows, vs] = o * lax.rsqrt(ms + NORM_EPS) * r["on"][...] * (gh * _sigmoid(gh))
    for s in range(bb):
        for h in range(GLA_HEADS):
            ks = slice(h * GLA_DK, (h + 1) * GLA_DK)
            dec_col = jnp.broadcast_to(dec_rows[s * ch:s * ch + 1, ks], (GLA_DK, GLA_DK)).T
            dec = jnp.concatenate([dec_col] * (GLA_DV // GLA_DK), axis=1)
            s_ref[s, h] = s_ref[s, h] * dec + upd[s, h]
    r["o"][...] = r["of"][...].astype(r["o"].dtype)


def _gla(proj, wa2p, ba2, onorm, s0, o_buf, st_buf, l, *, depth, m_total, r0, nseq, t, ch, bb):
    rb = bb * ch
    nc = t // ch
    assert (bb == 1 or nc == 1) and r0 % rb == 0

    def rblk(ib, c):
        return r0 // rb + ib * nc + c

    def sec(width, off):
        return pl.BlockSpec((rb, width), lambda ib, c: (rblk(ib, c), off // width))

    st_spec = pl.BlockSpec((None, bb, GLA_HEADS, GLA_DK, GLA_DV), lambda ib, c: (l, ib, 0, 0, 0))
    names = ["q", "k", "v", "g", "ga", "wa2", "ba2", "on"]
    args = [proj, proj, proj, proj, proj, wa2p, ba2, onorm]
    in_specs = [sec(GLA_KW, C_GQ), sec(GLA_KW, C_GK), sec(GLA_VW, C_GV), sec(GLA_VW, C_GG),
                sec(LANES, proj.shape[1] - SMALL_W + S_GA),
                pl.BlockSpec((None, LANES, GLA_KW), lambda ib, c: (l, 0, 0)),
                pl.BlockSpec((None, 1, GLA_KW), lambda ib, c: (l, 0, 0)),
                pl.BlockSpec((None, 1, GLA_DV), lambda ib, c: (l, 0, 0))]
    if s0 is not None:
        names.append("s0")
        args.append(s0)
        in_specs.append(st_spec)
    al_arrays, al_specs, alias = _alias_inputs([o_buf, st_buf])
    alias = {len(args) + k: v for k, v in alias.items()}
    names += ["alias%d" % i for i in range(len(al_arrays))] + ["o", "s_out", "of"]
    return pl.pallas_call(
        functools.partial(_gla_body, names=names, bb=bb, ch=ch, zero_init=s0 is None),
        grid=(nseq // bb, nc),
        in_specs=in_specs + al_specs,
        out_specs=[pl.BlockSpec((rb, GLA_VW), lambda ib, c: (rblk(ib, c), 0)), st_spec],
        out_shape=[jax.ShapeDtypeStruct((m_total, GLA_VW), BF16),
                   jax.ShapeDtypeStruct((depth, nseq, GLA_HEADS, GLA_DK, GLA_DV), F32)],
        scratch_shapes=[pltpu.VMEM((rb, GLA_VW), F32)],
        input_output_aliases=alias,
        compiler_params=_cp(("parallel", "arbitrary")),
        name="gla",
    )(*args, *al_arrays)


def _rwkv_body(*refs, names, bb, ch, first, zero_init):
    r = dict(zip(names, refs))
    st_ref, carry_ref = r["st"], r["carry"]
    chained = bb == 1
    nr = bb * ch
    n2 = 2 * nr
    npair = RW_HEADS // 2
    c_id = pl.program_id(1)
    last_chunk = c_id == pl.num_programs(1) - 1
    lane = lax.broadcasted_iota(jnp.int32, (nr, LANES), 1)
    head_a = lane < RW_N
    r2 = lax.broadcasted_iota(jnp.int32, (n2, n2), 0)
    c2 = lax.broadcasted_iota(jnp.int32, (n2, n2), 1)
    same_blk = (r2 // ch) == (c2 // ch)
    strict = same_blk & (c2 < r2)
    incl = same_blk & (c2 <= r2)
    eye = (r2 == c2).astype(F32)
    blockdiag = (lax.broadcasted_iota(jnp.int32, (LANES, LANES), 0) < RW_N) == \
                (lax.broadcasted_iota(jnp.int32, (LANES, LANES), 1) < RW_N)
    zeros64 = jnp.zeros((RW_N, RW_N), F32)

    @pl.when(c_id == 0)
    def _():
        if chained:
            carry_ref[...] = r["sh"][...]
        for s in range(bb):
            for p in range(npair):
                if zero_init:
                    st_ref[s, p] = jnp.zeros((LANES, LANES), F32)
                else:
                    top = jnp.concatenate([r["s0"][s, 2 * p], zeros64], axis=1)
                    bot = jnp.concatenate([zeros64, r["s0"][s, 2 * p + 1]], axis=1)
                    st_ref[s, p] = jnp.concatenate([top, bot], axis=0)

    rowpos = lax.broadcasted_iota(jnp.int32, (nr, RW_W), 0) % ch
    sections = (("pr", 0), ("pk", RW_W), ("pv", 2 * RW_W), ("ps", 3 * RW_W))

    def shifted(name, off):
        p = r[name][...]
        prev = pltpu.roll(p, 1, axis=0)
        if chained:
            first_rows = jnp.broadcast_to(carry_ref[:, off:off + RW_W], (ch, RW_W))
        else:
            first_rows = r["sh"][:, off:off + RW_W]
        prev = jnp.where(rowpos == 0, first_rows, prev)
        return p, p + r["mu"][:, off:off + RW_W] * (prev - p)

    raw, feat = {}, {}
    for name, off in sections:
        raw[name], feat[name] = shifted(name, off)
    if chained:
        for name, off in sections:
            carry_ref[:, off:off + RW_W] = raw[name][ch - 1:ch, :]

    @pl.when(last_chunk)
    def _():
        for name, off in sections:
            for s in range(bb):
                last = s * ch + ch - 1
                r["sh_out"][s, :, off:off + RW_W] = raw[name][last:last + 1, :]

    f_r, f_k, f_v, f_s = feat["pr"], feat["pk"], feat["pv"], feat["ps"]
    xw = f_s[:, S_XW:S_XW + LANES]
    xa = f_s[:, S_XA:S_XA + LANES]
    xg = f_s[:, S_XG:S_XG + 2 * LANES]
    w_log = -_softplus(-(r["w0"][...] + _bdot(jnp.tanh(xw), r["w2"][...]))) - 0.5
    lw = -jnp.exp(w_log)
    a = _sigmoid(r["a0"][...] + _bdot(xa, r["a2"][...]))
    gate = _bdot(_sigmoid(xg), r["g2"][...])
    if first:
        vr = f_v
        r["vf_out"][...] = vr
    else:
        nu = _sigmoid(r["v0"][...] + _bdot(raw["ps"][:, S_V1:S_V1 + LANES], r["v2"][...]))
        vr = f_v + (r["vf"][...] - f_v) * nu
    kk = f_k * r["kkw"][...]
    kmod = f_k * (1.0 + (a - 1.0) * r["kaw"][...])
    g_in = _cumsum_rows(lw, ch)
    g_ex = g_in - lw
    g_tail = _suffix_sum_rows(lw, ch)
    e_in, e_ex, e_neg, e_tail = jnp.exp(g_in), jnp.exp(g_ex), jnp.exp(-g_in), jnp.exp(g_tail)
    e_end = jnp.exp(g_in + g_tail)

    def stack(x):
        return jnp.concatenate([jnp.where(head_a, x, 0.0), jnp.where(head_a, 0.0, x)], axis=0)

    def unstack(x):
        return x[:nr] + x[nr:]

    def pair_sum(x):
        sa = jnp.sum(jnp.where(head_a, x, 0.0), axis=-1, keepdims=True)
        sb = jnp.sum(jnp.where(head_a, 0.0, x), axis=-1, keepdims=True)
        return jnp.where(head_a, sa, sb)

    def seq_rows(s):
        return slice(s * ch, (s + 1) * ch)

    for p0 in range(0, npair, RW_PAIR_GROUP):
        group = range(p0, p0 + RW_PAIR_GROUP)
        q = {p: {} for p in group}
        for p in group:
            ls = slice(p * LANES, (p + 1) * LANES)
            d = q[p]
            kk_p = kk[:, ls]
            kap = kk_p / jnp.maximum(jnp.sqrt(pair_sum(kk_p * kk_p)), 1e-12)
            d["r"], d["k"], d["v"] = f_r[:, ls], kmod[:, ls], vr[:, ls]
            d["kap_t"] = kap * e_ex[:, ls]
            d["r_t"] = d["r"] * e_in[:, ls]
            ba = kap * a[:, ls]
            d["b_h"] = ba * e_tail[:, ls]
            d["k_h"] = d["k"] * e_tail[:, ls]
            lhs4 = jnp.concatenate([stack(d["kap_t"]), stack(d["r_t"])], axis=0)
            rhs4 = jnp.concatenate([stack(ba * e_neg[:, ls]), stack(d["k"] * e_neg[:, ls])], axis=0)
            m4 = _bdot_nt(lhs4, rhs4)
            d["n"] = jnp.where(strict, m4[:n2, :n2], 0.0)
            d["a_ak"] = jnp.where(strict, m4[:n2, n2:], 0.0)
            d["a_r"] = jnp.concatenate([jnp.where(incl, m4[n2:, :n2], 0.0), jnp.where(incl, m4[n2:, n2:], 0.0)], axis=1)
            d["dinv"] = eye
            d["v_st"] = stack(d["v"])
        for p in group:
            d = q[p]
            ks_parts, rs_parts = [], []
            for s in range(bb):
                both = _bdot_nt(jnp.concatenate([d["kap_t"][seq_rows(s)], d["r_t"][seq_rows(s)]], axis=0), st_ref[s, p])
                ks_parts.append(both[:ch])
                rs_parts.append(both[ch:])
            d["ks"] = ks_parts[0] if bb == 1 else jnp.concatenate(ks_parts, axis=0)
            d["rs"] = rs_parts[0] if bb == 1 else jnp.concatenate(rs_parts, axis=0)
            d["rhs"] = stack(d["ks"]) + _bdot(d["a_ak"], d["v_st"])
        m = 1
        while m < ch:
            low = same_blk & ((r2 // (2 * m)) == (c2 // (2 * m))) & (((r2 // m) % 2) == 1) & (((c2 // m) % 2) == 0)
            tmp = {p: _bdot(q[p]["dinv"], jnp.where(low, q[p]["n"], 0.0)) for p in group}
            for p in group:
                q[p]["dinv"] = q[p]["dinv"] - _bdot(tmp[p], q[p]["dinv"])
            m *= 2
        for p in group:
            d = q[p]
            d["u_st"] = -_bdot(d["dinv"], d["rhs"])
        for p in group:
            d = q[p]
            ls = slice(p * LANES, (p + 1) * LANES)
            y = unstack(stack(d["rs"]) + _bdot(d["a_r"], jnp.concatenate([d["u_st"], d["v_st"]], axis=0)))
            u_p = unstack(d["u_st"])
            for s in range(bb):
                rows = seq_rows(s)
                upd = _bdot_tn(jnp.concatenate([u_p[rows], d["v"][rows]], axis=0),
                               jnp.concatenate([d["b_h"][rows], d["k_h"][rows]], axis=0))
                st_ref[s, p] = st_ref[s, p] * e_end[s * ch:s * ch + 1, ls] + jnp.where(blockdiag, upd, 0.0)
            mean = pair_sum(y) * (1.0 / RW_N)
            yc = y - mean
            var = pair_sum(yc * yc) * (1.0 / RW_N)
            yn = yc * lax.rsqrt(var + RW_GN_EPS) * r["lnw"][:, ls] + r["lnb"][:, ls]
            bonus = pair_sum(d["r"] * d["k"] * r["rk"][:, ls]) * d["v"]
            r["o"][:, ls] = ((yn + bonus) * gate[:, ls]).astype(r["o"].dtype)

    @pl.when(last_chunk)
    def _():
        for s in range(bb):
            for p in range(npair):
                st = st_ref[s, p]
                r["s_out"][s, 2 * p] = st[:RW_N, :RW_N]
                r["s_out"][s, 2 * p + 1] = st[RW_N:, RW_N:]


def _rwkv(proj, vf, shift, s0, o_buf, st_buf, wts, l, *, depth, m_total, r0, nseq, t, ch, bb):
    first = vf is None
    rb = bb * ch
    nc = t // ch
    assert rb == RW_SLAB and (bb == 1 or nc == 1) and r0 % rb == 0
    small_off = proj.shape[1] - SMALL_W

    def rblk(ib, c):
        return r0 // rb + ib * nc + c

    def sec(off):
        return pl.BlockSpec((rb, RW_W), lambda ib, c: (rblk(ib, c), off // RW_W))

    def vec(width, dl=0):
        return pl.BlockSpec((None, 1, width), lambda ib, c: (l - dl, 0, 0))

    def mat(rows, dl=0):
        return pl.BlockSpec((None, rows, RW_W), lambda ib, c: (l - dl, 0, 0))

    group_rows = pl.BlockSpec((rb, RW_W), lambda ib, c: (ib * nc + c, 0))
    st_spec = pl.BlockSpec((None, bb, RW_HEADS, RW_N, RW_N), lambda ib, c: (l, ib, 0, 0, 0))
    sh_spec = pl.BlockSpec((bb, 1, SHIFT_W), lambda ib, c: (ib, 0, 0))
    names = ["pr", "pk", "pv", "ps", "mu", "sh", "w0", "w2", "a0", "a2", "g2", "kkw", "kaw", "rk", "lnw", "lnb"]
    args = [proj, proj, proj, proj, wts["mu"], shift, wts["w0"], wts["w2"], wts["a0"], wts["a2"], wts["g2"],
            wts["kk"], wts["ka"], wts["rk"], wts["lnw"], wts["lnb"]]
    sh_in = (pl.BlockSpec((None, 1, SHIFT_W), lambda ib, c: (ib, 0, 0)) if bb == 1
             else pl.BlockSpec((rb, SHIFT_W), lambda ib, c: (ib, 0)))
    in_specs = [sec(C_RR), sec(C_RK), sec(C_RV), sec(small_off), vec(SHIFT_W), sh_in,
                vec(RW_W), mat(LANES), vec(RW_W), mat(LANES), mat(2 * LANES)] + [vec(RW_W)] * 5
    if not first:
        names += ["vf", "v0", "v2"]
        args += [vf, wts["v0"], wts["v2"]]
        in_specs += [group_rows, vec(RW_W, 1), mat(LANES, 1)]
    if s0 is not None:
        names.append("s0")
        args.append(s0)
        in_specs.append(st_spec)
    al_arrays, al_specs, alias = _alias_inputs([o_buf, st_buf])
    alias = {len(args) + k: v for k, v in alias.items()}
    names += ["alias%d" % i for i in range(len(al_arrays))] + ["o", "s_out", "sh_out"]
    out_specs = [pl.BlockSpec((rb, RW_W), lambda ib, c: (rblk(ib, c), 0)), st_spec, sh_spec]
    out_shape = [jax.ShapeDtypeStruct((m_total, RW_W), BF16),
                 jax.ShapeDtypeStruct((depth, nseq, RW_HEADS, RW_N, RW_N), F32),
                 jax.ShapeDtypeStruct((nseq, 1, SHIFT_W), F32)]
    if first:
        names.append("vf_out")
        out_specs.append(group_rows)
        out_shape.append(jax.ShapeDtypeStruct((nseq * t, RW_W), F32))
    names += ["st", "carry"]
    res = pl.pallas_call(
        functools.partial(_rwkv_body, names=names, bb=bb, ch=ch, first=first,
                          zero_init=s0 is None),
        grid=(nseq // bb, nc),
        in_specs=in_specs + al_specs,
        out_specs=out_specs,
        out_shape=out_shape,
        scratch_shapes=[pltpu.VMEM((bb, RW_HEADS // 2, LANES, LANES), F32), pltpu.VMEM((1, SHIFT_W), F32)],
        input_output_aliases=alias,
        compiler_params=_cp(("parallel", "arbitrary")),
        name="rwkv7",
    )(*args, *al_arrays)
    return res[0], res[1], res[2], (res[3] if first else None)


def _xattn_body(*refs, names, bb, tq):
    r = dict(zip(names, refs))

    def kv(ref, s, h):
        return ref[s, :, h * X_DH:(h + 1) * X_DH]

    units = [(s, h) for s in range(bb) for h in range(X_HEADS)]
    pr = {}
    for s, h in units:
        sc = _bdot_nt(r["q"][s * tq:(s + 1) * tq, h * X_DH:(h + 1) * X_DH], kv(r["k"], s, h)) * (X_DH ** -0.5)
        e = jnp.exp(sc - jnp.max(sc, axis=-1, keepdims=True))
        pr[s, h] = e / jnp.sum(e, axis=-1, keepdims=True)
    for s, h in units:
        r["of"][s * tq:(s + 1) * tq, h * X_DH:(h + 1) * X_DH] = _bdot(pr[s, h], kv(r["v"], s, h))
    r["o"][...] = r["of"][...].astype(r["o"].dtype)


def _xattn(proj, mem_k, mem_v, o_buf, l, *, m_total, r0, nseq, t, tq, bb):
    rb = bb * tq
    nq = t // tq
    assert (bb == 1 or nq == 1) and r0 % rb == 0
    kv_spec = pl.BlockSpec((None, bb, MEM_LEN, X_W), lambda ib, c: (l, ib, 0, 0))
    rows = lambda ib, c: r0 // rb + ib * nq + c
    al_arrays, al_specs, alias = _alias_inputs([o_buf])
    alias = {3 + k: v for k, v in alias.items()}
    names = ["q", "k", "v"] + ["alias%d" % i for i in range(len(al_arrays))] + ["o", "of"]
    return pl.pallas_call(
        functools.partial(_xattn_body, names=names, bb=bb, tq=tq),
        grid=(nseq // bb, nq),
        in_specs=[pl.BlockSpec((rb, X_W), lambda ib, c: (rows(ib, c), C_XQ // X_W)), kv_spec, kv_spec] + al_specs,
        out_specs=pl.BlockSpec((rb, X_W), lambda ib, c: (rows(ib, c), 0)),
        out_shape=jax.ShapeDtypeStruct((m_total, X_W), BF16),
        scratch_shapes=[pltpu.VMEM((rb, X_W), F32)],
        input_output_aliases=alias,
        compiler_params=_cp(("parallel", "arbitrary")),
        name="mem_xattn",
    )(proj, mem_k, mem_v, *al_arrays)


def _xattn_cache_body(q_ref, k_hbm, v_hbm, alias_ref, o_ref, of_ref, kbuf, vbuf, sem, *, l, bb, tq):
    del alias_ref
    i = pl.program_id(0)
    n = pl.num_programs(0)

    def copies(step, slot):
        out = []
        for h in range(X_HEADS):
            for t, (src, dst) in enumerate(((k_hbm, kbuf), (v_hbm, vbuf))):
                out.append(pltpu.make_async_copy(src.at[l, pl.ds(step * bb, bb), :, h, :], dst.at[slot, h],
                                                 sem.at[slot, t, h]))
        return out

    @pl.when(i == 0)
    def _():
        for c in copies(0, 0):
            c.start()

    @pl.when(i + 1 < n)
    def _():
        for c in copies(i + 1, (i + 1) % 2):
            c.start()

    slot = i % 2
    for c in copies(i, slot):
        c.wait()

    grp = max(1, SUBLANES // tq)
    ur = grp * tq
    useq = lax.broadcasted_iota(jnp.int32, (ur, 1), 0) // tq
    units = [(u, h) for u in range(bb // grp) for h in range(X_HEADS)]
    pr = {}
    for u, h in units:
        qu = q_ref[u * ur:(u + 1) * ur, h * X_DH:(h + 1) * X_DH]
        for g in range(grp):
            sc = _bdot_nt(qu, kbuf[slot, h, u * grp + g]) * (X_DH ** -0.5)
            e = jnp.exp(sc - jnp.max(sc, axis=-1, keepdims=True))
            pr[u, h, g] = e / jnp.sum(e, axis=-1, keepdims=True)
    for u, h in units:
        o = _bdot(pr[u, h, 0], vbuf[slot, h, u * grp])
        for g in range(1, grp):
            o = jnp.where(useq == g, _bdot(pr[u, h, g], vbuf[slot, h, u * grp + g]), o)
        of_ref[u * ur:(u + 1) * ur, h * X_DH:(h + 1) * X_DH] = o
    o_ref[...] = of_ref[...].astype(o_ref.dtype)


def _xattn_cache(proj, cache_k, cache_v, o_buf, l, *, r0, nseq, tq, bb):
    rb = bb * tq
    assert r0 % rb == 0 and nseq % bb == 0
    rows = lambda ib: (r0 // rb + ib, 0)
    any_spec = pl.BlockSpec(memory_space=pl.ANY)
    buf = pltpu.VMEM((2, X_HEADS, bb, MEM_LEN, X_DH), F32)
    return pl.pallas_call(
        functools.partial(_xattn_cache_body, l=l, bb=bb, tq=tq),
        grid=(nseq // bb,),
        in_specs=[pl.BlockSpec((rb, X_W), lambda ib: (r0 // rb + ib, C_XQ // X_W)), any_spec, any_spec, any_spec],
        out_specs=pl.BlockSpec((rb, X_W), rows),
        out_shape=jax.ShapeDtypeStruct(o_buf.shape, o_buf.dtype),
        scratch_shapes=[pltpu.VMEM((rb, X_W), F32), buf, buf, pltpu.SemaphoreType.DMA((2, 2, X_HEADS))],
        input_output_aliases={3: 0},
        compiler_params=_cp(("arbitrary",)),
        name="cache_xattn",
    )(proj, cache_k, cache_v, o_buf)


def _pad_last(a, width):
    return jnp.pad(a, [(0, 0)] * (a.ndim - 1) + [(0, width - a.shape[-1])])


def _pad_rows(a, rows):
    return jnp.pad(a, [(0, 0)] * (a.ndim - 2) + [(0, rows - a.shape[-2]), (0, 0)])


def _small_section(ga, xw, xa, xg, v1):
    return jnp.concatenate([_pad_last(ga, S_XW - S_GA), _pad_last(xw, S_XA - S_XW), _pad_last(xa, S_XG - S_XA),
                            _pad_last(xg, S_V1 - S_XG), _pad_last(v1, SMALL_W - S_V1)], axis=-1)


def _small_weight_t(w_t, v1_t):
    sm = GLA_COLS + 3 * RW_W
    xq = GLA_COLS + RW_COLS
    pieces = ((S_GA, w_t[:, 3072:GLA_COLS]), (S_XW, w_t[:, sm:sm + RW_DECAY_R]),
              (S_XA, w_t[:, sm + RW_DECAY_R:sm + RW_DECAY_R + RW_A_R]),
              (S_XG, w_t[:, sm + RW_DECAY_R + RW_A_R:xq]), (S_V1, v1_t))
    rows = lax.optimization_barrier(tuple(p for _, p in pieces))
    out = jnp.zeros((w_t.shape[0], SMALL_W, w_t.shape[2]), BF16)
    for (off, _), piece in zip(pieces, rows):
        out = lax.dynamic_update_slice(out, piece.astype(BF16), (0, off, 0))
    return out


def _shift_layout(a):
    z16 = jnp.zeros(a.shape[:-1] + (GLA_RANK,), a.dtype)
    z32 = jnp.zeros(a.shape[:-1] + (RW_V_R,), a.dtype)
    o = 3 * RW_W
    small = _small_section(z16, a[..., o:o + RW_DECAY_R], a[..., o + RW_DECAY_R:o + RW_DECAY_R + RW_A_R],
                           a[..., o + RW_DECAY_R + RW_A_R:], z32)
    return jnp.concatenate([a[..., :o], small], axis=-1)


def _shift_unlayout(a):
    o = 3 * RW_W
    return jnp.concatenate([a[..., :o], a[..., o + S_XW:o + S_XW + RW_DECAY_R], a[..., o + S_XA:o + S_XA + RW_A_R],
                            a[..., o + S_XG:o + S_XG + RW_G_R]], axis=-1)


def kernel(x_prompt, x_sample, mem_prompt, state_gla, state_rwkv, state_rwkv_shift, cache_mem_k, cache_mem_v,
           g_norm1, w_in, gla_wa2, gla_ba2, gla_onorm, rw_mu, rw_w0, rw_w2, rw_a0, rw_a2, rw_g2, rw_kk, rw_ka,
           rw_rk, rw_lnx_w, rw_lnx_b, rw_v0, rw_v1, rw_v2, g_mem, w_mk, w_mv, w_branch, w_out, g_norm2,
           w_ff_gate, w_ff_up, w_ff_down, g_final):
    depth = w_in.shape[0]
    bp, tp, d = x_prompt.shape
    bs, ts, _ = x_sample.shape
    assert d % 1024 == 0 and ts in (1, 2, 4, 8) and tp % RW_SLAB == 0 and bs % (RW_SLAB // ts) == 0
    mp, ms = bp * tp, bs * ts
    m = mp + ms
    f = w_ff_gate.shape[-1]

    v1_all = jnp.concatenate([jnp.zeros((1, d, RW_V_R), F32), rw_v1], axis=0)
    w_t = jnp.swapaxes(w_in, 1, 2)
    w_small_t = _small_weight_t(w_t, jnp.swapaxes(v1_all, 1, 2)).astype(BF16)
    r3 = lambda a: a.reshape(a.shape[0], 1, -1)
    rw_wts = dict(mu=r3(_shift_layout(rw_mu)), w0=r3(rw_w0), w2=_pad_rows(rw_w2, LANES), a0=r3(rw_a0),
                  a2=_pad_rows(rw_a2, LANES), g2=_pad_rows(rw_g2, 2 * LANES), kk=r3(rw_kk), ka=r3(rw_ka),
                  rk=r3(rw_rk), lnw=r3(rw_lnx_w), lnb=r3(rw_lnx_b), v0=r3(rw_v0), v2=_pad_rows(rw_v2, LANES))
    wa2p = _pad_rows(gla_wa2, LANES)
    ba2, onorm = r3(gla_ba2), r3(gla_onorm)
    g1, g2n, gm = r3(g_norm1), r3(g_norm2), r3(g_mem)

    x = jnp.concatenate([x_prompt.reshape(mp, d), x_sample.reshape(ms, d)], axis=0)

    tm = _pick(m, 1152, 4 * SUBLANES)
    tm_wide = _pick(m, 2 * tm, 4 * SUBLANES)
    tn_d = _pick(d, 512, LANES)
    tn_f = _pick(f, 512, LANES)

    mem_rows = mem_prompt.reshape(bp * MEM_LEN, d)
    tmm = _pick(bp * MEM_LEN, 1024, SUBLANES)
    p_mem_k = jnp.stack([_norm_mm(mem_rows, gm, w_mk, l, tm=tmm, tn=tn_d, name="mem_k") for l in range(depth)])
    p_mem_v = jnp.stack([_norm_mm(mem_rows, gm, w_mv, l, tm=tmm, tn=tn_d, name="mem_v") for l in range(depth)])
    p_mem_k = p_mem_k.reshape(depth, bp, MEM_LEN, X_W)
    p_mem_v = p_mem_v.reshape(depth, bp, MEM_LEN, X_W)

    sh0 = jnp.zeros((bp, 1, SHIFT_W), F32)
    s_shift_in = jnp.repeat(_shift_layout(state_rwkv_shift), ts, axis=1)

    bb_s = RW_SLAB // ts
    tq_p = _pick(tp, 512, SUBLANES)
    common = dict(depth=depth, m_total=m)
    prompt = dict(r0=0, nseq=bp, t=tp)
    sample = dict(r0=mp, nseq=bs, t=ts)
    vf_p = vf_s = None
    pg_all = sg_all = pr_all = sr_all = None
    p_shift, s_shift = [], []
    for l in range(depth):
        proj, gates = _in_proj(x, g1, w_t, w_small_t, l, tm=tm)

        o_gla, pg_all = _gla(proj, wa2p, ba2, onorm, None, None, pg_all, l, **common, **prompt,
                             ch=GLA_CHUNK, bb=1)
        o_gla, sg_all = _gla(proj, wa2p, ba2, onorm, state_gla, o_gla, sg_all, l, **common, **sample,
                             ch=ts, bb=bb_s)
        o_rw, pr_all, sh_p, vf_new_p = _rwkv(proj, vf_p, sh0, None, None, pr_all, rw_wts, l, **common, **prompt,
                                             ch=RW_SLAB, bb=1)
        o_rw, sr_all, sh_s, vf_new_s = _rwkv(proj, vf_s, s_shift_in[l], state_rwkv, o_rw, sr_all, rw_wts, l,
                                             **common, **sample, ch=ts, bb=bb_s)
        if l == 0:
            vf_p, vf_s = vf_new_p, vf_new_s
        o_x = _xattn(proj, p_mem_k, p_mem_v, None, l, m_total=m, **prompt, tq=tq_p, bb=1)
        o_x = _xattn_cache(proj, cache_mem_k, cache_mem_v, o_x, l, r0=mp, nseq=bs, tq=ts, bb=XC_SEQS)

        merged = _merge(o_gla, o_rw, o_x, gates, w_branch, l, tm=tm, tn=tn_d)
        x = _mm_res(merged, w_out, l, x, tm=tm_wide, tn=tn_d, name="out_proj")
        act = _ffn_up(x, g2n, w_ff_gate, w_ff_up, l, tm=tm, tn=tn_f)
        x = _mm_res(act, w_ff_down, l, x, tm=tm, tn=_pick(d, 256, LANES), name="ffn_down")
        p_shift.append(_shift_unlayout(sh_p[:, 0]))
        s_shift.append(_shift_unlayout(sh_s[:, 0]))

    y_prompt = _final_norm(x, g_final.reshape(1, d), 0, mp).reshape(bp, tp, d)
    y_sample = _final_norm(x, g_final.reshape(1, d), mp, ms).reshape(bs, ts, d)
    return (y_prompt, y_sample, pg_all, pr_all, jnp.stack(p_shift),
            p_mem_k.reshape(depth, bp, MEM_LEN, X_HEADS, X_DH), p_mem_v.reshape(depth, bp, MEM_LEN, X_HEADS, X_DH),
            sg_all, sr_all, jnp.stack(s_shift))
```

```python
import functools

import jax
import jax.numpy as jnp
from jax import lax
from jax.experimental import pallas as pl
from jax.experimental.pallas import tpu as pltpu

F32 = jnp.float32
BF16 = jnp.bfloat16

GLA_HEADS, GLA_DK, GLA_DV = 4, 128, 256
GLA_KW, GLA_VW, GLA_RANK = GLA_HEADS * GLA_DK, GLA_HEADS * GLA_DV, 16
GLA_GATE_NORM = 16.0
RW_HEADS, RW_N = 16, 64
RW_W = RW_HEADS * RW_N
RW_DECAY_R, RW_A_R, RW_V_R, RW_G_R = 64, 64, 32, 160
RW_GN_EPS = 64e-5
MEM_LEN, X_HEADS, X_DH = 256, 4, 256
X_W = X_HEADS * X_DH
N_BRANCH = 3
NORM_EPS = 1e-6
GLA_COLS = 2 * GLA_KW + 2 * GLA_VW + GLA_RANK
RW_COLS = 3 * RW_W + RW_DECAY_R + RW_A_R + RW_G_R

LANES = 128
SUBLANES = 8
VMEM_LIMIT_BYTES = 56 * 1024 * 1024

C_GQ, C_GK, C_GV, C_GG = 0, 512, 1024, 2048
C_RR, C_RK, C_RV, C_XQ, C_SMALL = 3072, 4096, 5120, 6144, 7168
SMALL_W = 1024
IN_TILE = 1024
S_GA, S_XW, S_XA, S_XG, S_V1 = 0, 128, 256, 384, 640
SHIFT_W = 3 * RW_W + SMALL_W

RW_SLAB = 64
XC_SEQS = 8
RW_PAIR_GROUP = 8
GLA_CHUNK = 64


def _cp(sem):
    return pltpu.CompilerParams(dimension_semantics=sem, vmem_limit_bytes=VMEM_LIMIT_BYTES)


def _pick(n, target, mult):
    best = None
    for d in range(mult, min(n, target) + 1, mult):
        if n % d == 0:
            best = d
    assert best is not None, (n, target, mult)
    return best


def _bdot(a, b):
    return jnp.dot(a.astype(BF16), b.astype(BF16), preferred_element_type=F32)


def _bdot_nt(a, b):
    return lax.dot_general(a.astype(BF16), b.astype(BF16), (((1,), (1,)), ((), ())), preferred_element_type=F32)


def _bdot_tn(a, b):
    return lax.dot_general(a.astype(BF16), b.astype(BF16), (((0,), (0,)), ((), ())), preferred_element_type=F32)


def _split3(x):
    hi = x.astype(BF16)
    r1 = x - hi.astype(F32)
    mid = r1.astype(BF16)
    lo = (r1 - mid.astype(F32)).astype(BF16)
    return hi, mid, lo


def _softplus(y):
    return jnp.maximum(y, 0.0) + jnp.log(1.0 + jnp.exp(-jnp.abs(y)))


def _sigmoid(y):
    return 1.0 / (1.0 + jnp.exp(-y))


def _cumsum_rows(x, block):
    pos = lax.broadcasted_iota(jnp.int32, x.shape, 0) % block
    d = 1
    while d < block:
        x = x + jnp.where(pos >= d, pltpu.roll(x, d, axis=0), 0.0)
        d *= 2
    return x


def _suffix_sum_rows(x, block):
    n = x.shape[0]
    pos = lax.broadcasted_iota(jnp.int32, x.shape, 0) % block
    y = x
    d = 1
    while d < block:
        y = y + jnp.where(pos + d < block, pltpu.roll(y, n - d, axis=0), 0.0)
        d *= 2
    return y - x


def _alias_inputs(bufs):
    arrays, specs, alias = [], [], {}
    for out_idx, buf in enumerate(bufs):
        if buf is not None:
            alias[len(arrays)] = out_idx
            arrays.append(buf)
            specs.append(pl.BlockSpec(memory_space=pl.ANY))
    return arrays, specs, alias


def _rms_rows(x_ref, g_ref, h_ref, rows, eps):
    n = x_ref.shape[0] // rows

    def body(r, c):
        sl = pl.ds(pl.multiple_of(r * rows, rows), rows)
        x = x_ref[sl, :]
        ms = jnp.mean(x * x, axis=-1, keepdims=True)
        h_ref[sl, :] = (x * lax.rsqrt(ms + eps) * g_ref[...]).astype(h_ref.dtype)
        return c

    lax.fori_loop(0, n, body, 0)


def _mem_kv_body(x_ref, g_ref, wk_ref, wv_ref, ok_ref, ov_ref, h_ref, *, rows):
    @pl.when(pl.program_id(2) == 0)
    def _():
        _rms_rows(x_ref, g_ref, h_ref, rows, NORM_EPS)

    h = h_ref[...]
    ok_ref[...] = jnp.dot(h, wk_ref[...].astype(BF16), preferred_element_type=F32)
    ov_ref[...] = jnp.dot(h, wv_ref[...].astype(BF16), preferred_element_type=F32)


def _mem_kv(x, g3, wk, wv, *, tm, tn):
    m, d = x.shape
    depth, _, n = wk.shape
    w_spec = pl.BlockSpec((None, d, tn), lambda l, i, j: (l, 0, j))
    o_spec = pl.BlockSpec((None, tm, tn), lambda l, i, j: (l, i, j))
    return pl.pallas_call(
        functools.partial(_mem_kv_body, rows=_pick(tm, 128, SUBLANES)),
        grid=(depth, m // tm, n // tn),
        in_specs=[pl.BlockSpec((tm, d), lambda l, i, j: (i, 0)),
                  pl.BlockSpec((None, 1, d), lambda l, i, j: (l, 0, 0)),
                  w_spec, w_spec],
        out_specs=[o_spec, o_spec],
        out_shape=[jax.ShapeDtypeStruct((depth, m, n), F32)] * 2,
        scratch_shapes=[pltpu.VMEM((tm, d), BF16)],
        compiler_params=_cp(("parallel", "parallel", "arbitrary")),
        name="mem_kv",
    )(x, g3, wk, wv)


def _in_proj_body(x_ref, g_ref, wm_ref, ws_ref, o_ref, og_ref, h_ref, *, rows, n_main, n_small):
    j = pl.program_id(1)

    @pl.when(j == 0)
    def _():
        _rms_rows(x_ref, g_ref, h_ref, rows, NORM_EPS)

    @pl.when(j < n_main)
    def _():
        o_ref[...] = _bdot_nt(h_ref[...], wm_ref[0])

    @pl.when((j >= n_main) & (j < n_main + n_small))
    def _():
        o_ref[...] = _bdot_nt(h_ref[...], ws_ref[...])

    @pl.when(j >= n_main + n_small)
    def _():
        og_ref[...] = _sigmoid(_bdot_nt(h_ref[...], wm_ref[0])).astype(og_ref.dtype)


def _in_proj(x, g3, w_t, w_small_t, l, *, tm):
    m, d = x.shape
    tn = IN_TILE
    n_main, n_small = C_SMALL // tn, SMALL_W // tn
    xq0 = GLA_COLS + RW_COLS
    n_gla, n_rw = C_RR // tn, (C_XQ - C_RR) // tn
    n_gate = (w_t.shape[1] - xq0 - X_W) // tn

    def w_rows(i, j):
        jm = jnp.where(j < n_main, j, jnp.maximum(j - n_small, n_main - 1))
        start = jnp.where(jm < n_gla, jm * tn,
                          jnp.where(jm < n_gla + n_rw, GLA_COLS + (jm - n_gla) * tn, xq0 + (jm - n_gla - n_rw) * tn))
        return (l, pl.multiple_of(start, 2 * SUBLANES), 0)

    return pl.pallas_call(
        functools.partial(_in_proj_body, rows=_pick(tm, 128, SUBLANES), n_main=n_main, n_small=n_small),
        grid=(m // tm, n_main + n_small + n_gate),
        in_specs=[pl.BlockSpec((tm, d), lambda i, j: (i, 0), pipeline_mode=pl.Buffered(1)),
                  pl.BlockSpec((None, 1, d), lambda i, j: (l, 0, 0)),
                  pl.BlockSpec((pl.Element(1), pl.Element(tn), pl.Element(d)), w_rows),
                  pl.BlockSpec((None, tn, d), lambda i, j: (l, jnp.clip(j - n_main, 0, n_small - 1), 0),
                               pipeline_mode=pl.Buffered(1))],
        out_specs=[pl.BlockSpec((tm, tn), lambda i, j: (i, jnp.minimum(j, n_main + n_small - 1))),
                   pl.BlockSpec((tm, tn), lambda i, j: (i, jnp.maximum(j - n_main - n_small, 0)))],
        out_shape=[jax.ShapeDtypeStruct((m, (n_main + n_small) * tn), F32),
                   jax.ShapeDtypeStruct((m, n_gate * tn), BF16)],
        scratch_shapes=[pltpu.VMEM((tm, d), BF16)],
        compiler_params=_cp(("parallel", "arbitrary")),
        name="in_proj",
    )(x, g3, w_t, w_small_t)


def _merge_body(og_ref, or_ref, ox_ref, gg_ref, gr_ref, gx_ref, w_ref, o_ref):
    acc = gg_ref[...].astype(F32) * _bdot(og_ref[...], w_ref[0])
    acc += gr_ref[...].astype(F32) * _bdot(or_ref[...], w_ref[1])
    acc += gx_ref[...].astype(F32) * _bdot(ox_ref[...], w_ref[2])
    o_ref[...] = acc.astype(o_ref.dtype)


def _merge(o_gla, o_rw, o_x, gates, w_branch, l, *, tm, tn):
    m, bw = o_gla.shape
    d = w_branch.shape[-1]
    gate_blk = [b * d // tn for b in range(N_BRANCH)]
    o_spec = pl.BlockSpec((tm, bw), lambda i, j: (i, 0))
    return pl.pallas_call(
        _merge_body,
        grid=(m // tm, d // tn),
        in_specs=[o_spec, o_spec, o_spec]
        + [pl.BlockSpec((tm, tn), functools.partial(lambda i, j, off: (i, off + j), off=gate_blk[b]))
           for b in range(N_BRANCH)]
        + [pl.BlockSpec((None, N_BRANCH, bw, tn), lambda i, j: (l, 0, 0, j))],
        out_specs=pl.BlockSpec((tm, tn), lambda i, j: (i, j)),
        out_shape=jax.ShapeDtypeStruct((m, d), BF16),
        compiler_params=_cp(("parallel", "arbitrary")),
        name="branch_merge",
    )(o_gla, o_rw, o_x, gates, gates, gates, w_branch)


def _mm_res_body(a_ref, w_ref, r_ref, o_ref):
    o_ref[...] = r_ref[...] + _bdot(a_ref[...], w_ref[...])


def _mm_res(a, w3, l, res, *, tm, tn, name):
    m, kd = a.shape
    n = w3.shape[-1]
    return pl.pallas_call(
        _mm_res_body,
        grid=(m // tm, n // tn),
        in_specs=[pl.BlockSpec((tm, kd), lambda i, j: (i, 0)),
                  pl.BlockSpec((None, kd, tn), lambda i, j: (l, 0, j)),
                  pl.BlockSpec((tm, tn), lambda i, j: (i, j))],
        out_specs=pl.BlockSpec((tm, tn), lambda i, j: (i, j)),
        out_shape=jax.ShapeDtypeStruct((m, n), F32),
        compiler_params=_cp(("parallel", "arbitrary")),
        name=name,
    )(a, w3, res)


def _ffn_up_body(x_ref, g_ref, wg_ref, wu_ref, o_ref, h_ref, *, rows):
    @pl.when(pl.program_id(1) == 0)
    def _():
        _rms_rows(x_ref, g_ref, h_ref, rows, NORM_EPS)

    h = h_ref[...]
    a = jnp.dot(h, wg_ref[...].astype(BF16), preferred_element_type=F32)
    u = jnp.dot(h, wu_ref[...].astype(BF16), preferred_element_type=F32)
    o_ref[...] = (a * _sigmoid(a) * u).astype(o_ref.dtype)


def _ffn_up(x, g3, wg, wu, l, *, tm, tn):
    m, d = x.shape
    f = wg.shape[-1]
    w_spec = pl.BlockSpec((None, d, tn), lambda i, j: (l, 0, j))
    return pl.pallas_call(
        functools.partial(_ffn_up_body, rows=_pick(tm, 128, SUBLANES)),
        grid=(m // tm, f // tn),
        in_specs=[pl.BlockSpec((tm, d), lambda i, j: (i, 0)),
                  pl.BlockSpec((None, 1, d), lambda i, j: (l, 0, 0)),
                  w_spec, w_spec],
        out_specs=pl.BlockSpec((tm, tn), lambda i, j: (i, j)),
        out_shape=jax.ShapeDtypeStruct((m, f), BF16),
        scratch_shapes=[pltpu.VMEM((tm, d), BF16)],
        compiler_params=_cp(("parallel", "arbitrary")),
        name="ffn_up",
    )(x, g3, wg, wu)


def _final_norm_body(x_ref, g_ref, o_ref):
    x = x_ref[...]
    ms = jnp.mean(x * x, axis=-1, keepdims=True)
    o_ref[...] = x * lax.rsqrt(ms + NORM_EPS) * g_ref[...]


def _final_norm(x, g2, r0, rows):
    d = x.shape[1]
    tm = _pick(rows, 256, SUBLANES)
    assert r0 % tm == 0
    return pl.pallas_call(
        _final_norm_body,
        grid=(rows // tm,),
        in_specs=[pl.BlockSpec((tm, d), lambda i: (r0 // tm + i, 0)), pl.BlockSpec((1, d), lambda i: (0, 0))],
        out_specs=pl.BlockSpec((tm, d), lambda i: (i, 0)),
        out_shape=jax.ShapeDtypeStruct((rows, d), F32),
        compiler_params=_cp(("parallel",)),
        name="final_norm",
    )(x, g2)


def _gla_body(*refs, names, bb, ch, zero_init):
    r = dict(zip(names, refs))
    s_ref = r["s_out"]

    @pl.when(pl.program_id(1) == 0)
    def _():
        if zero_init:
            s_ref[...] = jnp.zeros(s_ref.shape, F32)
        else:
            s_ref[...] = r["s0"][...]

    grp = max(1, SUBLANES // ch)
    ur = grp * ch
    useq = lax.broadcasted_iota(jnp.int32, (ur, 1), 0) // ch
    z = _bdot(r["ga"][...], r["wa2"][...]) + r["ba2"][...]
    la = -_softplus(-z) * (1.0 / GLA_GATE_NORM)
    kk = r["k"][...]
    bc = _cumsum_rows(la, ch)
    tail = _suffix_sum_rows(la, ch)
    q_s = r["q"][...] * (GLA_DK ** -0.5)
    qe = q_s * jnp.exp(bc)
    kl = kk * jnp.exp(tail)
    dec_rows = jnp.exp(bc + tail)

    levels = []
    m = ch // 2
    while m >= SUBLANES:
        levels.append(m)
        m //= 2
    lev_q, lev_k, lev_mask = [], [], []
    if levels:
        assert bb == 1
        row = lax.broadcasted_iota(jnp.int32, (ch, ch), 0)
        col = lax.broadcasted_iota(jnp.int32, (ch, ch), 1)
        bc3 = _split3(bc)
        for m in levels:
            sel = (col == (row // (2 * m)) * (2 * m) + m - 1).astype(BF16)
            ref = sum(jnp.dot(sel, part, preferred_element_type=F32) for part in bc3)
            lev_q.append(q_s * jnp.exp(jnp.minimum(bc - ref, 0.0)))
            lev_k.append(kk * jnp.exp(jnp.minimum(ref - bc, 0.0)))
            lev_mask.append(((row // (2 * m)) == (col // (2 * m))) & (((row // m) % 2) == 1) & (((col // m) % 2) == 0))

    sub = lax.broadcasted_iota(jnp.int32, (SUBLANES, 1), 0)
    tile_seq = sub // min(ch, SUBLANES)

    def tile_attention(rows8):
        q8, k8, b8, v8 = q_s[rows8], kk[rows8], bc[rows8], r["v"][rows8, :]
        acc = [jnp.zeros((SUBLANES, GLA_DV), F32) for _ in range(GLA_HEADS)]
        for j in range(SUBLANES):
            prod = q8 * k8[j:j + 1] * jnp.exp(jnp.minimum(b8 - b8[j:j + 1], 0.0))
            valid = (sub >= j) & (tile_seq == j // min(ch, SUBLANES))
            for h in range(GLA_HEADS):
                a = jnp.sum(prod[:, h * GLA_DK:(h + 1) * GLA_DK], axis=-1, keepdims=True)
                acc[h] = acc[h] + jnp.where(valid, a, 0.0) * v8[j:j + 1, h * GLA_DV:(h + 1) * GLA_DV]
        return acc

    units = [(u, h) for u in range(bb // grp) for h in range(GLA_HEADS)]

    def sl(u, h):
        return (slice(u * ur, (u + 1) * ur), slice(h * GLA_DK, (h + 1) * GLA_DK),
                slice(h * GLA_DV, (h + 1) * GLA_DV))

    def own(q, x):
        return x if grp == 1 else jnp.where(useq == q, x, 0.0)

    att, o_int, o_near, upd = {}, {}, {}, {}
    for u in range(bb // grp):
        tiles = [tile_attention(slice(u * ur + t * SUBLANES, u * ur + (t + 1) * SUBLANES))
                 for t in range(ur // SUBLANES)]
        for h in range(GLA_HEADS):
            o_near[u, h] = tiles[0][h] if len(tiles) == 1 else jnp.concatenate([t[h] for t in tiles], axis=0)
    for u, h in units:
        rows, ks, vs = sl(u, h)
        if levels:
            att[u, h] = sum(jnp.where(mk, _bdot_nt(ql[rows, ks], kl_[rows, ks]), 0.0)
                            for ql, kl_, mk in zip(lev_q, lev_k, lev_mask))
        o_int[u, h] = sum(own(q, _bdot(qe[rows, ks], s_ref[u * grp + q, h])) for q in range(grp))
    for u, h in units:
        rows, ks, vs = sl(u, h)
        vh = r["v"][rows, vs]
        o = o_int[u, h] + o_near[u, h]
        if levels:
            o = o + _bdot(att[u, h], vh)
        for q in range(grp):
            upd[u * grp + q, h] = _bdot_tn(own(q, kl[rows, ks]), vh)
        ms = jnp.mean(o * o, axis=-1, keepdims=True)
        gh = r["g"][rows, vs]
        r["of"][rows, vs] = o * lax.rsqrt(ms + NORM_EPS) * r["on"][...] * (gh * _sigmoid(gh))
    for s in range(bb):
        for h in range(GLA_HEADS):
            ks = slice(h * GLA_DK, (h + 1) * GLA_DK)
            dec_col = jnp.broadcast_to(dec_rows[s * ch:s * ch + 1, ks], (GLA_DK, GLA_DK)).T
            dec = jnp.concatenate([dec_col] * (GLA_DV // GLA_DK), axis=1)
            s_ref[s, h] = s_ref[s, h] * dec + upd[s, h]
    r["o"][...] = r["of"][...].astype(r["o"].dtype)


def _gla(proj, wa2p, ba2, onorm, s0, o_buf, st_buf, l, *, depth, m_total, r0, nseq, t, ch, bb):
    rb = bb * ch
    nc = t // ch
    assert (bb == 1 or nc == 1) and r0 % rb == 0

    def rblk(ib, c):
        return r0 // rb + ib * nc + c

    def sec(width, off):
        return pl.BlockSpec((rb, width), lambda ib, c: (rblk(ib, c), off // width))

    st_spec = pl.BlockSpec((None, bb, GLA_HEADS, GLA_DK, GLA_DV), lambda ib, c: (l, ib, 0, 0, 0))
    names = ["q", "k", "v", "g", "ga", "wa2", "ba2", "on"]
    args = [proj, proj, proj, proj, proj, wa2p, ba2, onorm]
    in_specs = [sec(GLA_KW, C_GQ), sec(GLA_KW, C_GK), sec(GLA_VW, C_GV), sec(GLA_VW, C_GG),
                sec(LANES, proj.shape[1] - SMALL_W + S_GA),
                pl.BlockSpec((None, LANES, GLA_KW), lambda ib, c: (l, 0, 0)),
                pl.BlockSpec((None, 1, GLA_KW), lambda ib, c: (l, 0, 0)),
                pl.BlockSpec((None, 1, GLA_DV), lambda ib, c: (l, 0, 0))]
    if s0 is not None:
        names.append("s0")
        args.append(s0)
        in_specs.append(st_spec)
    al_arrays, al_specs, alias = _alias_inputs([o_buf, st_buf])
    alias = {len(args) + k: v for k, v in alias.items()}
    names += ["alias%d" % i for i in range(len(al_arrays))] + ["o", "s_out", "of"]
    return pl.pallas_call(
        functools.partial(_gla_body, names=names, bb=bb, ch=ch, zero_init=s0 is None),
        grid=(nseq // bb, nc),
        in_specs=in_specs + al_specs,
        out_specs=[pl.BlockSpec((rb, GLA_VW), lambda ib, c: (rblk(ib, c), 0)), st_spec],
        out_shape=[jax.ShapeDtypeStruct((m_total, GLA_VW), BF16),
                   jax.ShapeDtypeStruct((depth, nseq, GLA_HEADS, GLA_DK, GLA_DV), F32)],
        scratch_shapes=[pltpu.VMEM((rb, GLA_VW), F32)],
        input_output_aliases=alias,
        compiler_params=_cp(("parallel", "arbitrary")),
        name="gla",
    )(*args, *al_arrays)


def _rwkv_body(*refs, names, bb, ch, first, zero_init):
    r = dict(zip(names, refs))
    st_ref, carry_ref = r["st"], r["carry"]
    chained = bb == 1
    nr = bb * ch
    n2 = 2 * nr
    npair = RW_HEADS // 2
    c_id = pl.program_id(1)
    last_chunk = c_id == pl.num_programs(1) - 1
    lane = lax.broadcasted_iota(jnp.int32, (nr, LANES), 1)
    head_a = lane < RW_N
    r2 = lax.broadcasted_iota(jnp.int32, (n2, n2), 0)
    c2 = lax.broadcasted_iota(jnp.int32, (n2, n2), 1)
    same_blk = (r2 // ch) == (c2 // ch)
    strict = same_blk & (c2 < r2)
    incl = same_blk & (c2 <= r2)
    eye = (r2 == c2).astype(F32)
    blockdiag = (lax.broadcasted_iota(jnp.int32, (LANES, LANES), 0) < RW_N) == \
                (lax.broadcasted_iota(jnp.int32, (LANES, LANES), 1) < RW_N)
    zeros64 = jnp.zeros((RW_N, RW_N), F32)

    @pl.when(c_id == 0)
    def _():
        if chained:
            carry_ref[...] = r["sh"][...]
        for s in range(bb):
            for p in range(npair):
                if zero_init:
                    st_ref[s, p] = jnp.zeros((LANES, LANES), F32)
                else:
                    top = jnp.concatenate([r["s0"][s, 2 * p], zeros64], axis=1)
                    bot = jnp.concatenate([zeros64, r["s0"][s, 2 * p + 1]], axis=1)
                    st_ref[s, p] = jnp.concatenate([top, bot], axis=0)

    rowpos = lax.broadcasted_iota(jnp.int32, (nr, RW_W), 0) % ch
    sections = (("pr", 0), ("pk", RW_W), ("pv", 2 * RW_W), ("ps", 3 * RW_W))

    def shifted(name, off):
        p = r[name][...]
        prev = pltpu.roll(p, 1, axis=0)
        if chained:
            first_rows = jnp.broadcast_to(carry_ref[:, off:off + RW_W], (ch, RW_W))
        else:
            first_rows = r["sh"][:, off:off + RW_W]
        prev = jnp.where(rowpos == 0, first_rows, prev)
        return p, p + r["mu"][:, off:off + RW_W] * (prev - p)

    raw, feat = {}, {}
    for name, off in sections:
        raw[name], feat[name] = shifted(name, off)
    if chained:
        for name, off in sections:
            carry_ref[:, off:off + RW_W] = raw[name][ch - 1:ch, :]

    @pl.when(last_chunk)
    def _():
        for name, off in sections:
            for s in range(bb):
                last = s * ch + ch - 1
                r["sh_out"][s, :, off:off + RW_W] = raw[name][last:last + 1, :]

    f_r, f_k, f_v, f_s = feat["pr"], feat["pk"], feat["pv"], feat["ps"]
    xw = f_s[:, S_XW:S_XW + LANES]
    xa = f_s[:, S_XA:S_XA + LANES]
    xg = f_s[:, S_XG:S_XG + 2 * LANES]
    w_log = -_softplus(-(r["w0"][...] + _bdot(jnp.tanh(xw), r["w2"][...]))) - 0.5
    lw = -jnp.exp(w_log)
    a = _sigmoid(r["a0"][...] + _bdot(xa, r["a2"][...]))
    gate = _bdot(_sigmoid(xg), r["g2"][...])
    if first:
        vr = f_v
        r["vf_out"][...] = vr
    else:
        nu = _sigmoid(r["v0"][...] + _bdot(raw["ps"][:, S_V1:S_V1 + LANES], r["v2"][...]))
        vr = f_v + (r["vf"][...] - f_v) * nu
    kk = f_k * r["kkw"][...]
    kmod = f_k * (1.0 + (a - 1.0) * r["kaw"][...])
    g_in = _cumsum_rows(lw, ch)
    g_ex = g_in - lw
    g_tail = _suffix_sum_rows(lw, ch)
    e_in, e_ex, e_neg, e_tail = jnp.exp(g_in), jnp.exp(g_ex), jnp.exp(-g_in), jnp.exp(g_tail)
    e_end = jnp.exp(g_in + g_tail)

    def stack(x):
        return jnp.concatenate([jnp.where(head_a, x, 0.0), jnp.where(head_a, 0.0, x)], axis=0)

    def unstack(x):
        return x[:nr] + x[nr:]

    def pair_sum(x):
        sa = jnp.sum(jnp.where(head_a, x, 0.0), axis=-1, keepdims=True)
        sb = jnp.sum(jnp.where(head_a, 0.0, x), axis=-1, keepdims=True)
        return jnp.where(head_a, sa, sb)

    def seq_rows(s):
        return slice(s * ch, (s + 1) * ch)

    for p0 in range(0, npair, RW_PAIR_GROUP):
        group = range(p0, p0 + RW_PAIR_GROUP)
        q = {p: {} for p in group}
        for p in group:
            ls = slice(p * LANES, (p + 1) * LANES)
            d = q[p]
            kk_p = kk[:, ls]
            kap = kk_p / jnp.maximum(jnp.sqrt(pair_sum(kk_p * kk_p)), 1e-12)
            d["r"], d["k"], d["v"] = f_r[:, ls], kmod[:, ls], vr[:, ls]
            d["kap_t"] = kap * e_ex[:, ls]
            d["r_t"] = d["r"] * e_in[:, ls]
            ba = kap * a[:, ls]
            d["b_h"] = ba * e_tail[:, ls]
            d["k_h"] = d["k"] * e_tail[:, ls]
            lhs4 = jnp.concatenate([stack(d["kap_t"]), stack(d["r_t"])], axis=0)
            rhs4 = jnp.concatenate([stack(ba * e_neg[:, ls]), stack(d["k"] * e_neg[:, ls])], axis=0)
            m4 = _bdot_nt(lhs4, rhs4)
            d["n"] = jnp.where(strict, m4[:n2, :n2], 0.0)
            d["a_ak"] = jnp.where(strict, m4[:n2, n2:], 0.0)
            d["a_r"] = jnp.concatenate([jnp.where(incl, m4[n2:, :n2], 0.0), jnp.where(incl, m4[n2:, n2:], 0.0)], axis=1)
            d["dinv"] = eye
            d["v_st"] = stack(d["v"])
        for p in group:
            d = q[p]
            ks_parts, rs_parts = [], []
            for s in range(bb):
                both = _bdot_nt(jnp.concatenate([d["kap_t"][seq_rows(s)], d["r_t"][seq_rows(s)]], axis=0), st_ref[s, p])
                ks_parts.append(both[:ch])
                rs_parts.append(both[ch:])
            d["ks"] = ks_parts[0] if bb == 1 else jnp.concatenate(ks_parts, axis=0)
            d["rs"] = rs_parts[0] if bb == 1 else jnp.concatenate(rs_parts, axis=0)
            d["rhs"] = stack(d["ks"]) + _bdot(d["a_ak"], d["v_st"])
        m = 1
        while m < ch:
            low = same_blk & ((r2 // (2 * m)) == (c2 // (2 * m))) & (((r2 // m) % 2) == 1) & (((c2 // m) % 2) == 0)
            tmp = {p: _bdot(q[p]["dinv"], jnp.where(low, q[p]["n"], 0.0)) for p in group}
            for p in group:
                q[p]["dinv"] = q[p]["dinv"] - _bdot(tmp[p], q[p]["dinv"])
            m *= 2
        for p in group:
            d = q[p]
            d["u_st"] = -_bdot(d["dinv"], d["rhs"])
        for p in group:
            d = q[p]
            ls = slice(p * LANES, (p + 1) * LANES)
            y = unstack(stack(d["rs"]) + _bdot(d["a_r"], jnp.concatenate([d["u_st"], d["v_st"]], axis=0)))
            u_p = unstack(d["u_st"])
            for s in range(bb):
                rows = seq_rows(s)
                upd = _bdot_tn(jnp.concatenate([u_p[rows], d["v"][rows]], axis=0),
                               jnp.concatenate([d["b_h"][rows], d["k_h"][rows]], axis=0))
                st_ref[s, p] = st_ref[s, p] * e_end[s * ch:s * ch + 1, ls] + jnp.where(blockdiag, upd, 0.0)
            mean = pair_sum(y) * (1.0 / RW_N)
            yc = y - mean
            var = pair_sum(yc * yc) * (1.0 / RW_N)
            yn = yc * lax.rsqrt(var + RW_GN_EPS) * r["lnw"][:, ls] + r["lnb"][:, ls]
            bonus = pair_sum(d["r"] * d["k"] * r["rk"][:, ls]) * d["v"]
            r["o"][:, ls] = ((yn + bonus) * gate[:, ls]).astype(r["o"].dtype)

    @pl.when(last_chunk)
    def _():
        for s in range(bb):
            for p in range(npair):
                st = st_ref[s, p]
                r["s_out"][s, 2 * p] = st[:RW_N, :RW_N]
                r["s_out"][s, 2 * p + 1] = st[RW_N:, RW_N:]


def _rwkv(proj, vf, shift, s0, o_buf, st_buf, wts, l, *, depth, m_total, r0, nseq, t, ch, bb):
    first = vf is None
    rb = bb * ch
    nc = t // ch
    assert rb == RW_SLAB and (bb == 1 or nc == 1) and r0 % rb == 0
    small_off = proj.shape[1] - SMALL_W

    def rblk(ib, c):
        return r0 // rb + ib * nc + c

    def sec(off):
        return pl.BlockSpec((rb, RW_W), lambda ib, c: (rblk(ib, c), off // RW_W))

    def vec(width, dl=0):
        return pl.BlockSpec((None, 1, width), lambda ib, c: (l - dl, 0, 0))

    def mat(rows, dl=0):
        return pl.BlockSpec((None, rows, RW_W), lambda ib, c: (l - dl, 0, 0))

    group_rows = pl.BlockSpec((rb, RW_W), lambda ib, c: (ib * nc + c, 0))
    st_spec = pl.BlockSpec((None, bb, RW_HEADS, RW_N, RW_N), lambda ib, c: (l, ib, 0, 0, 0))
    sh_spec = pl.BlockSpec((bb, 1, SHIFT_W), lambda ib, c: (ib, 0, 0))
    names = ["pr", "pk", "pv", "ps", "mu", "sh", "w0", "w2", "a0", "a2", "g2", "kkw", "kaw", "rk", "lnw", "lnb"]
    args = [proj, proj, proj, proj, wts["mu"], shift, wts["w0"], wts["w2"], wts["a0"], wts["a2"], wts["g2"],
            wts["kk"], wts["ka"], wts["rk"], wts["lnw"], wts["lnb"]]
    sh_in = (pl.BlockSpec((None, 1, SHIFT_W), lambda ib, c: (ib, 0, 0)) if bb == 1
             else pl.BlockSpec((rb, SHIFT_W), lambda ib, c: (ib, 0)))
    in_specs = [sec(C_RR), sec(C_RK), sec(C_RV), sec(small_off), vec(SHIFT_W), sh_in,
                vec(RW_W), mat(LANES), vec(RW_W), mat(LANES), mat(2 * LANES)] + [vec(RW_W)] * 5
    if not first:
        names += ["vf", "v0", "v2"]
        args += [vf, wts["v0"], wts["v2"]]
        in_specs += [group_rows, vec(RW_W, 1), mat(LANES, 1)]
    if s0 is not None:
        names.append("s0")
        args.append(s0)
        in_specs.append(st_spec)
    al_arrays, al_specs, alias = _alias_inputs([o_buf, st_buf])
    alias = {len(args) + k: v for k, v in alias.items()}
    names += ["alias%d" % i for i in range(len(al_arrays))] + ["o", "s_out", "sh_out"]
    out_specs = [pl.BlockSpec((rb, RW_W), lambda ib, c: (rblk(ib, c), 0)), st_spec, sh_spec]
    out_shape = [jax.ShapeDtypeStruct((m_total, RW_W), BF16),
                 jax.ShapeDtypeStruct((depth, nseq, RW_HEADS, RW_N, RW_N), F32),
                 jax.ShapeDtypeStruct((nseq, 1, SHIFT_W), F32)]
    if first:
        names.append("vf_out")
        out_specs.append(group_rows)
        out_shape.append(jax.ShapeDtypeStruct((nseq * t, RW_W), F32))
    names += ["st", "carry"]
    res = pl.pallas_call(
        functools.partial(_rwkv_body, names=names, bb=bb, ch=ch, first=first,
                          zero_init=s0 is None),
        grid=(nseq // bb, nc),
        in_specs=in_specs + al_specs,
        out_specs=out_specs,
        out_shape=out_shape,
        scratch_shapes=[pltpu.VMEM((bb, RW_HEADS // 2, LANES, LANES), F32), pltpu.VMEM((1, SHIFT_W), F32)],
        input_output_aliases=alias,
        compiler_params=_cp(("parallel", "arbitrary")),
        name="rwkv7",
    )(*args, *al_arrays)
    return res[0], res[1], res[2], (res[3] if first else None)


def _xattn_body(*refs, names, bb, tq):
    r = dict(zip(names, refs))

    def kv(ref, s, h):
        return ref[s, :, h * X_DH:(h + 1) * X_DH]

    units = [(s, h) for s in range(bb) for h in range(X_HEADS)]
    pr = {}
    for s, h in units:
        sc = _bdot_nt(r["q"][s * tq:(s + 1) * tq, h * X_DH:(h + 1) * X_DH], kv(r["k"], s, h)) * (X_DH ** -0.5)
        e = jnp.exp(sc - jnp.max(sc, axis=-1, keepdims=True))
        pr[s, h] = e / jnp.sum(e, axis=-1, keepdims=True)
    for s, h in units:
        r["of"][s * tq:(s + 1) * tq, h * X_DH:(h + 1) * X_DH] = _bdot(pr[s, h], kv(r["v"], s, h))
    r["o"][...] = r["of"][...].astype(r["o"].dtype)


def _xattn(proj, mem_k, mem_v, o_buf, l, *, m_total, r0, nseq, t, tq, bb):
    rb = bb * tq
    nq = t // tq
    assert (bb == 1 or nq == 1) and r0 % rb == 0
    kv_spec = pl.BlockSpec((None, bb, MEM_LEN, X_W), lambda ib, c: (l, ib, 0, 0))
    rows = lambda ib, c: r0 // rb + ib * nq + c
    al_arrays, al_specs, alias = _alias_inputs([o_buf])
    alias = {3 + k: v for k, v in alias.items()}
    names = ["q", "k", "v"] + ["alias%d" % i for i in range(len(al_arrays))] + ["o", "of"]
    return pl.pallas_call(
        functools.partial(_xattn_body, names=names, bb=bb, tq=tq),
        grid=(nseq // bb, nq),
        in_specs=[pl.BlockSpec((rb, X_W), lambda ib, c: (rows(ib, c), C_XQ // X_W)), kv_spec, kv_spec] + al_specs,
        out_specs=pl.BlockSpec((rb, X_W), lambda ib, c: (rows(ib, c), 0)),
        out_shape=jax.ShapeDtypeStruct((m_total, X_W), BF16),
        scratch_shapes=[pltpu.VMEM((rb, X_W), F32)],
        input_output_aliases=alias,
        compiler_params=_cp(("parallel", "arbitrary")),
        name="mem_xattn",
    )(proj, mem_k, mem_v, *al_arrays)


def _xattn_cache_body(q_ref, k_hbm, v_hbm, alias_ref, o_ref, of_ref, kbuf, vbuf, sem, *, l, bb, tq):
    del alias_ref
    i = pl.program_id(0)
    n = pl.num_programs(0)

    def copies(step, slot):
        out = []
        for h in range(X_HEADS):
            for t, (src, dst) in enumerate(((k_hbm, kbuf), (v_hbm, vbuf))):
                out.append(pltpu.make_async_copy(src.at[l, pl.ds(step * bb, bb), :, h, :], dst.at[slot, h],
                                                 sem.at[slot, t, h]))
        return out

    @pl.when(i == 0)
    def _():
        for c in copies(0, 0):
            c.start()

    @pl.when(i + 1 < n)
    def _():
        for c in copies(i + 1, (i + 1) % 2):
            c.start()

    slot = i % 2
    for c in copies(i, slot):
        c.wait()

    grp = max(1, SUBLANES // tq)
    ur = grp * tq
    useq = lax.broadcasted_iota(jnp.int32, (ur, 1), 0) // tq
    units = [(u, h) for u in range(bb // grp) for h in range(X_HEADS)]
    pr = {}
    for u, h in units:
        qu = q_ref[u * ur:(u + 1) * ur, h * X_DH:(h + 1) * X_DH]
        for g in range(grp):
            sc = _bdot_nt(qu, kbuf[slot, h, u * grp + g]) * (X_DH ** -0.5)
            e = jnp.exp(sc - jnp.max(sc, axis=-1, keepdims=True))
            pr[u, h, g] = e / jnp.sum(e, axis=-1, keepdims=True)
    for u, h in units:
        o = _bdot(pr[u, h, 0], vbuf[slot, h, u * grp])
        for g in range(1, grp):
            o = jnp.where(useq == g, _bdot(pr[u, h, g], vbuf[slot, h, u * grp + g]), o)
        of_ref[u * ur:(u + 1) * ur, h * X_DH:(h + 1) * X_DH] = o
    o_ref[...] = of_ref[...].astype(o_ref.dtype)


def _xattn_cache(proj, cache_k, cache_v, o_buf, l, *, r0, nseq, tq, bb):
    rb = bb * tq
    assert r0 % rb == 0 and nseq % bb == 0
    rows = lambda ib: (r0 // rb + ib, 0)
    any_spec = pl.BlockSpec(memory_space=pl.ANY)
    buf = pltpu.VMEM((2, X_HEADS, bb, MEM_LEN, X_DH), F32)
    return pl.pallas_call(
        functools.partial(_xattn_cache_body, l=l, bb=bb, tq=tq),
        grid=(nseq // bb,),
        in_specs=[pl.BlockSpec((rb, X_W), lambda ib: (r0 // rb + ib, C_XQ // X_W)), any_spec, any_spec, any_spec],
        out_specs=pl.BlockSpec((rb, X_W), rows),
        out_shape=jax.ShapeDtypeStruct(o_buf.shape, o_buf.dtype),
        scratch_shapes=[pltpu.VMEM((rb, X_W), F32), buf, buf, pltpu.SemaphoreType.DMA((2, 2, X_HEADS))],
        input_output_aliases={3: 0},
        compiler_params=_cp(("arbitrary",)),
        name="cache_xattn",
    )(proj, cache_k, cache_v, o_buf)


def _pad_last(a, width):
    return jnp.pad(a, [(0, 0)] * (a.ndim - 1) + [(0, width - a.shape[-1])])


def _pad_rows(a, rows):
    return jnp.pad(a, [(0, 0)] * (a.ndim - 2) + [(0, rows - a.shape[-2]), (0, 0)])


def _small_section(ga, xw, xa, xg, v1):
    return jnp.concatenate([_pad_last(ga, S_XW - S_GA), _pad_last(xw, S_XA - S_XW), _pad_last(xa, S_XG - S_XA),
                            _pad_last(xg, S_V1 - S_XG), _pad_last(v1, SMALL_W - S_V1)], axis=-1)


def _small_weight_t(w_t, v1_t):
    sm = GLA_COLS + 3 * RW_W
    xq = GLA_COLS + RW_COLS
    pieces = ((S_GA, w_t[:, 3072:GLA_COLS]), (S_XW, w_t[:, sm:sm + RW_DECAY_R]),
              (S_XA, w_t[:, sm + RW_DECAY_R:sm + RW_DECAY_R + RW_A_R]),
              (S_XG, w_t[:, sm + RW_DECAY_R + RW_A_R:xq]), (S_V1, v1_t))
    rows = lax.optimization_barrier(tuple(p for _, p in pieces))
    out = jnp.zeros((w_t.shape[0], SMALL_W, w_t.shape[2]), BF16)
    for (off, _), piece in zip(pieces, rows):
        out = lax.dynamic_update_slice(out, piece.astype(BF16), (0, off, 0))
    return out


def _shift_layout(a):
    z16 = jnp.zeros(a.shape[:-1] + (GLA_RANK,), a.dtype)
    z32 = jnp.zeros(a.shape[:-1] + (RW_V_R,), a.dtype)
    o = 3 * RW_W
    small = _small_section(z16, a[..., o:o + RW_DECAY_R], a[..., o + RW_DECAY_R:o + RW_DECAY_R + RW_A_R],
                           a[..., o + RW_DECAY_R + RW_A_R:], z32)
    return jnp.concatenate([a[..., :o], small], axis=-1)


def _shift_unlayout(a):
    o = 3 * RW_W
    return jnp.concatenate([a[..., :o], a[..., o + S_XW:o + S_XW + RW_DECAY_R], a[..., o + S_XA:o + S_XA + RW_A_R],
                            a[..., o + S_XG:o + S_XG + RW_G_R]], axis=-1)


def kernel(x_prompt, x_sample, mem_prompt, state_gla, state_rwkv, state_rwkv_shift, cache_mem_k, cache_mem_v,
           g_norm1, w_in, gla_wa2, gla_ba2, gla_onorm, rw_mu, rw_w0, rw_w2, rw_a0, rw_a2, rw_g2, rw_kk, rw_ka,
           rw_rk, rw_lnx_w, rw_lnx_b, rw_v0, rw_v1, rw_v2, g_mem, w_mk, w_mv, w_branch, w_out, g_norm2,
           w_ff_gate, w_ff_up, w_ff_down, g_final):
    depth = w_in.shape[0]
    bp, tp, d = x_prompt.shape
    bs, ts, _ = x_sample.shape
    assert d % 1024 == 0 and ts in (1, 2, 4, 8) and tp % RW_SLAB == 0 and bs % (RW_SLAB // ts) == 0
    mp, ms = bp * tp, bs * ts
    m = mp + ms
    f = w_ff_gate.shape[-1]

    v1_all = jnp.concatenate([jnp.zeros((1, d, RW_V_R), F32), rw_v1], axis=0)
    w_t = jnp.swapaxes(w_in, 1, 2)
    w_small_t = _small_weight_t(w_t, jnp.swapaxes(v1_all, 1, 2)).astype(BF16)
    r3 = lambda a: a.reshape(a.shape[0], 1, -1)
    rw_wts = dict(mu=r3(_shift_layout(rw_mu)), w0=r3(rw_w0), w2=_pad_rows(rw_w2, LANES), a0=r3(rw_a0),
                  a2=_pad_rows(rw_a2, LANES), g2=_pad_rows(rw_g2, 2 * LANES), kk=r3(rw_kk), ka=r3(rw_ka),
                  rk=r3(rw_rk), lnw=r3(rw_lnx_w), lnb=r3(rw_lnx_b), v0=r3(rw_v0), v2=_pad_rows(rw_v2, LANES))
    wa2p = _pad_rows(gla_wa2, LANES)
    ba2, onorm = r3(gla_ba2), r3(gla_onorm)
    g1, g2n, gm = r3(g_norm1), r3(g_norm2), r3(g_mem)

    x = jnp.concatenate([x_prompt.reshape(mp, d), x_sample.reshape(ms, d)], axis=0)

    tm = _pick(m, 1152, 4 * SUBLANES)
    tm_wide = _pick(m, 2 * tm, 4 * SUBLANES)
    tn_d = _pick(d, 512, LANES)
    tn_f = _pick(f, 512, LANES)

    mem_rows = mem_prompt.reshape(bp * MEM_LEN, d)
    tmm = _pick(bp * MEM_LEN, 1024, SUBLANES)
    p_mem_k, p_mem_v = _mem_kv(mem_rows, gm, w_mk, w_mv, tm=tmm, tn=tn_d)
    p_mem_k = p_mem_k.reshape(depth, bp, MEM_LEN, X_W)
    p_mem_v = p_mem_v.reshape(depth, bp, MEM_LEN, X_W)

    sh0 = jnp.zeros((bp, 1, SHIFT_W), F32)
    s_shift_in = jnp.repeat(_shift_layout(state_rwkv_shift), ts, axis=1)

    bb_s = RW_SLAB // ts
    tq_p = _pick(tp, 512, SUBLANES)
    common = dict(depth=depth, m_total=m)
    prompt = dict(r0=0, nseq=bp, t=tp)
    sample = dict(r0=mp, nseq=bs, t=ts)
    vf_p = vf_s = None
    pg_all = sg_all = pr_all = sr_all = None
    p_shift, s_shift = [], []
    for l in range(depth):
        proj, gates = _in_proj(x, g1, w_t, w_small_t, l, tm=tm)

        o_gla, pg_all = _gla(proj, wa2p, ba2, onorm, None, None, pg_all, l, **common, **prompt,
                             ch=GLA_CHUNK, bb=1)
        o_gla, sg_all = _gla(proj, wa2p, ba2, onorm, state_gla, o_gla, sg_all, l, **common, **sample,
                             ch=ts, bb=bb_s)
        o_rw, pr_all, sh_p, vf_new_p = _rwkv(proj, vf_p, sh0, None, None, pr_all, rw_wts, l, **common, **prompt,
                                             ch=RW_SLAB, bb=1)
        o_rw, sr_all, sh_s, vf_new_s = _rwkv(proj, vf_s, s_shift_in[l], state_rwkv, o_rw, sr_all, rw_wts, l,
                                             **common, **sample, ch=ts, bb=bb_s)
        if l == 0:
            vf_p, vf_s = vf_new_p, vf_new_s
        o_x = _xattn(proj, p_mem_k, p_mem_v, None, l, m_total=m, **prompt, tq=tq_p, bb=1)
        o_x = _xattn_cache(proj, cache_mem_k, cache_mem_v, o_x, l, r0=mp, nseq=bs, tq=ts, bb=XC_SEQS)

        merged = _merge(o_gla, o_rw, o_x, gates, w_branch, l, tm=tm, tn=tn_d)
        x = _mm_res(merged, w_out, l, x, tm=tm_wide, tn=tn_d, name="out_proj")
        act = _ffn_up(x, g2n, w_ff_gate, w_ff_up, l, tm=tm, tn=tn_f)
        x = _mm_res(act, w_ff_down, l, x, tm=tm, tn=_pick(d, 256, LANES), name="ffn_down")
        p_shift.append(_shift_unlayout(sh_p[:, 0]))
        s_shift.append(_shift_unlayout(sh_s[:, 0]))

    y_prompt = _final_norm(x, g_final.reshape(1, d), 0, mp).reshape(bp, tp, d)
    y_sample = _final_norm(x, g_final.reshape(1, d), mp, ms).reshape(bs, ts, d)
    return (y_prompt, y_sample, pg_all, pr_all, jnp.stack(p_shift),
            p_mem_k.reshape(depth, bp, MEM_LEN, X_HEADS, X_DH), p_mem_v.reshape(depth, bp, MEM_LEN, X_HEADS, X_DH),
            sg_all, sr_all, jnp.stack(s_shift))
```
